```python
import math
import jax, jax.numpy as jnp
from jax import lax
import numpy as np

D_MODEL = 1024
BATCH = 8
SEQ = 2048
DEPTH = 4

GRID_W = 64
CTX_LEN = 256
MIX_WIDTH = D_MODEL
DN_WIDTH = MIX_WIDTH // 2
DN_HEADS = 4
DN_HEAD_DIM = DN_WIDTH // DN_HEADS
N_DIR = 2
CONV_K = 4
CHUNK = 64
POOL_WIDTH = MIX_WIDTH - DN_WIDTH
POOL_WINDOWS = (2, 4, 8, 16)
POOL_GROUPS = len(POOL_WINDOWS)
POOL_GROUP_DIM = POOL_WIDTH // POOL_GROUPS
N_EXPERTS = 16
EC_CAPACITY = 2
EXPERT_FF = D_MODEL
EPS = 1e-6
QKV_COLS = 3 * DN_WIDTH
GATE_COLS = N_DIR * DN_HEADS
STATE_COLS = QKV_COLS + 2 * GATE_COLS
IN_COLS = STATE_COLS + DN_WIDTH + POOL_WIDTH

kernel_name = 'hybrid_pool_gdn_ec_flow_trunk'


def rmsnorm(x, g):
    xf = x.astype(jnp.float32)
    y = xf * lax.rsqrt(jnp.mean(xf * xf, axis=-1, keepdims=True) + EPS)
    return (y * g.astype(jnp.float32)).astype(x.dtype)


def l2norm(x):
    xf = x.astype(jnp.float32)
    return xf * lax.rsqrt(jnp.sum(xf * xf, axis=-1, keepdims=True) + EPS)


def adaln_params(cond, w_mod, b_mod):
    return jnp.split(jax.nn.silu(cond) @ w_mod + b_mod, 6, axis=-1)


def modulate(h, shift, scale):
    return h * (1 + scale) + shift


def window_bounds(n, w):
    pos = jnp.arange(n)
    return jnp.clip(pos - w // 2, 0, n), jnp.clip(pos - w // 2 + w, 0, n)


def centred_box_mean(u, rows):
    B_, L, C = u.shape
    uf = u.astype(jnp.float32)
    outs = []
    if rows is None:
        s = jnp.pad(jnp.cumsum(uf, axis=1), ((0, 0), (1, 0), (0, 0)))
        for i, w in enumerate(POOL_WINDOWS):
            sg = s[..., i * POOL_GROUP_DIM:(i + 1) * POOL_GROUP_DIM]
            lo, hi = window_bounds(L, w)
            cnt = (hi - lo).astype(jnp.float32)
            outs.append((jnp.take(sg, hi, axis=1) - jnp.take(sg, lo, axis=1)) / cnt[None, :, None])
        return jnp.concatenate(outs, axis=-1)
    grid = uf.reshape(B_, rows, GRID_W, C)
    s = jnp.pad(jnp.cumsum(jnp.cumsum(grid, axis=1), axis=2), ((0, 0), (1, 0), (1, 0), (0, 0)))
    for i, w in enumerate(POOL_WINDOWS):
        sg = s[..., i * POOL_GROUP_DIM:(i + 1) * POOL_GROUP_DIM]
        lr, hr = window_bounds(rows, w)
        lc, hc = window_bounds(GRID_W, w)
        s_hi = jnp.take(sg, hr, axis=1)
        s_lo = jnp.take(sg, lr, axis=1)
        tot = (jnp.take(s_hi, hc, axis=2) - jnp.take(s_hi, lc, axis=2)
               - jnp.take(s_lo, hc, axis=2) + jnp.take(s_lo, lc, axis=2))
        cnt = ((hr - lr)[:, None] * (hc - lc)[None, :]).astype(jnp.float32)
        outs.append(tot / cnt[None, :, :, None])
    return jnp.concatenate(outs, axis=-1).reshape(B_, L, C)


def pool_mixer(u, pool_w, pool_scale, rows):
    B_, L, _ = u.shape
    m = (centred_box_mean(u, rows) - u.astype(jnp.float32)).astype(u.dtype)
    m = m.reshape(B_, L, POOL_GROUPS, POOL_GROUP_DIM)
    y = jnp.einsum('blgc,gcd->blgd', m, pool_w).reshape(B_, L, POOL_WIDTH)
    return y * pool_scale


def short_conv(x, w):
    C = x.shape[-1]
    return lax.conv_general_dilated(
        x, w[:, None, :], window_strides=(1,),
        padding=[(CONV_K // 2, CONV_K - 1 - CONV_K // 2)],
        dimension_numbers=('NWC', 'WIO', 'NWC'), feature_group_count=C)


def deltanet_inputs(feat_pre, ab_pre, conv_w, a_log, dt_bias):
    B_, L, _ = feat_pre.shape
    feats = jax.nn.silu(short_conv(feat_pre, conv_w))
    heads = lambda t: jnp.transpose(t.reshape(B_, L, DN_HEADS, DN_HEAD_DIM), (0, 2, 1, 3))
    parts = [heads(t) for t in jnp.split(feats, feats.shape[-1] // DN_WIDTH, axis=-1)]
    ab = ab_pre.astype(jnp.float32).reshape(B_, L, 2, N_DIR, DN_HEADS)
    log_a = -jnp.exp(a_log.astype(jnp.float32)) * jax.nn.softplus(ab[:, :, 0] + dt_bias.astype(jnp.float32))
    beta = jax.nn.sigmoid(ab[:, :, 1])
    to_dir = lambda t: jnp.transpose(t, (2, 0, 3, 1))
    return parts, to_dir(log_a), to_dir(beta)


def chunk_gated_delta(q, k, v, log_a, beta, s0):
    B_, H, L, _ = k.shape
    n = L // CHUNK
    f32 = jnp.float32
    blocks = lambda t: t.astype(f32).reshape(B_, H, n, CHUNK, t.shape[-1])
    kc, vc = blocks(k), blocks(v)
    g = jnp.cumsum(log_a.astype(f32).reshape(B_, H, n, CHUNK), axis=-1)
    bc = beta.astype(f32).reshape(B_, H, n, CHUNK, 1)
    idx = jnp.arange(CHUNK)
    incl = idx[:, None] >= idx[None, :]
    decay = jnp.where(incl, jnp.exp(jnp.minimum(g[..., :, None] - g[..., None, :], 0.0)), 0.0)
    k_beta = kc * bc
    m = jnp.where(idx[:, None] > idx[None, :], jnp.einsum('bhnik,bhnjk->bhnij', k_beta, kc) * decay, 0.0)
    eye = jnp.eye(CHUNK, dtype=f32)
    p = -m
    t_inv = eye + p
    for _ in range(CHUNK.bit_length() - 2):
        p = p @ p
        t_inv = t_inv @ (eye + p)
    w_blk = jnp.einsum('bhnij,bhnjk->bhnik', t_inv, k_beta * jnp.exp(g)[..., None])
    u_blk = jnp.einsum('bhnij,bhnjv->bhniv', t_inv, vc * bc)
    g_last = g[..., -1:]
    k_tail = kc * jnp.exp(g_last - g)[..., None]
    s_decay = jnp.exp(g_last)[..., None]
    first = lambda t: jnp.moveaxis(t, 2, 0)

    def advance(s, w_n, u_n, kt_n, sd_n):
        u_new = u_n - jnp.einsum('bhik,bhkv->bhiv', w_n, s)
        return s * sd_n + jnp.einsum('bhik,bhiv->bhkv', kt_n, u_new), u_new

    if q is None:
        def state_step(s, xs):
            s_next, _ = advance(s, *xs)
            return s_next, None
        s_fin, _ = lax.scan(state_step, s0, tuple(map(first, (w_blk, u_blk, k_tail, s_decay))))
        return None, s_fin

    qc = blocks(q)
    q_dec = qc * jnp.exp(g)[..., None]
    qk = jnp.einsum('bhnik,bhnjk->bhnij', qc, kc) * decay

    def out_step(s, xs):
        w_n, u_n, kt_n, sd_n, qd_n, qk_n = xs
        s_next, u_new = advance(s, w_n, u_n, kt_n, sd_n)
        o_n = jnp.einsum('bhik,bhkv->bhiv', qd_n, s) + jnp.einsum('bhij,bhjv->bhiv', qk_n, u_new)
        return s_next, o_n

    s_fin, o = lax.scan(out_step, s0, tuple(map(first, (w_blk, u_blk, k_tail, s_decay, q_dec, qk))))
    return jnp.moveaxis(o, 0, 2).reshape(B_, H, L, -1), s_fin


def bidir_delta(q, k, v, log_a, beta, s0):
    rev = lambda t: None if t is None else jnp.flip(t, axis=2)
    o_f, s_f = chunk_gated_delta(q, k, v, log_a[0], beta[0], s0[0])
    o_b, s_b = chunk_gated_delta(rev(q), rev(k), rev(v), rev(log_a[1]), rev(beta[1]), s0[1])
    o = None if q is None else o_f + rev(o_b)
    return o, (s_f, s_b)


def token_mixer(h, w_in, conv_w, a_log, dt_bias, dn_norm, pool_w, pool_scale, w_out, s0, rows):
    B_, L, _ = h.shape
    proj = h @ w_in
    (q, k, v), log_a, beta = deltanet_inputs(proj[..., :QKV_COLS], proj[..., QKV_COLS:STATE_COLS],
                                             conv_w, a_log, dt_bias)
    q = l2norm(q) * DN_HEAD_DIM ** -0.5
    k = l2norm(k)
    o, states = bidir_delta(q, k, v, log_a, beta, s0)
    o = jnp.transpose(o, (0, 2, 1, 3))
    gate = proj[..., STATE_COLS:STATE_COLS + DN_WIDTH].reshape(B_, L, DN_HEADS, DN_HEAD_DIM)
    o = (rmsnorm(o, dn_norm) * jax.nn.silu(gate.astype(jnp.float32))).astype(h.dtype)
    o = o.reshape(B_, L, DN_WIDTH)
    pooled = pool_mixer(proj[..., STATE_COLS + DN_WIDTH:], pool_w, pool_scale, rows)
    return jnp.concatenate([o, pooled], axis=-1) @ w_out, states


def context_states(h, w_in, conv_w, a_log, dt_bias, s0):
    proj = h @ w_in[:, DN_WIDTH:STATE_COLS]
    (k, v), log_a, beta = deltanet_inputs(proj[..., :2 * DN_WIDTH], proj[..., 2 * DN_WIDTH:],
                                          conv_w[:, DN_WIDTH:], a_log, dt_bias)
    _, states = bidir_delta(None, l2norm(k), v, log_a, beta, s0)
    return states


def expert_choice_ffn(h, w_router, w_gate, w_up, w_down):
    B_, L, D = h.shape
    cap = EC_CAPACITY * L // N_EXPERTS
    aff = jax.nn.softmax((h @ w_router).astype(jnp.float32), axis=-1)
    top_aff, top_idx = lax.top_k(jnp.swapaxes(aff, 1, 2), cap)
    xe = jax.vmap(lambda hb, ib: hb[ib])(h, top_idx)
    hid = jax.nn.silu(jnp.einsum('becd,edf->becf', xe, w_gate)) * jnp.einsum('becd,edf->becf', xe, w_up)
    ye = jnp.einsum('becf,efd->becd', hid, w_down) * top_aff[..., None].astype(h.dtype)
    combine = lambda ib, yb: jnp.zeros((L, D), yb.dtype).at[ib.reshape(-1)].add(yb.reshape(-1, D))
    return jax.vmap(combine)(top_idx, ye)


def setup_inputs(seed: int = 0) -> dict:
    key = jax.random.key(seed)
    ks = jax.random.split(key, 21)
    nrm = lambda k, shape, scale: jax.random.normal(k, shape, jnp.float32) * scale
    D = D_MODEL
    dt = jnp.exp(jax.random.uniform(ks[11], (DEPTH, N_DIR, DN_HEADS), jnp.float32,
                                    minval=math.log(1e-3), maxval=math.log(1e-1)))
    return {
        'x': nrm(ks[0], (BATCH, SEQ, D), 1.0),
        'c': nrm(ks[1], (BATCH, D), 1.0),
        'ctx': nrm(ks[2], (BATCH, CTX_LEN, D), 1.0),
        'c_ctx': nrm(ks[3], (D,), 1.0),
        'w_mod': nrm(ks[4], (DEPTH, D, 6 * D), 0.5 * D ** -0.5),
        'b_mod': nrm(ks[5], (DEPTH, 6 * D), 0.01),
        'norm1': 1.0 + nrm(ks[6], (DEPTH, D), 0.02),
        'norm2': 1.0 + nrm(ks[7], (DEPTH, D), 0.02),
        'w_in': nrm(ks[8], (DEPTH, D, IN_COLS), D ** -0.5),
        'conv_w': nrm(ks[9], (DEPTH, CONV_K, QKV_COLS), CONV_K ** -0.5),
        'a_log': jnp.log(jax.random.uniform(ks[10], (DEPTH, N_DIR, DN_HEADS), jnp.float32, minval=1.0, maxval=16.0)),
        'dt_bias': dt + jnp.log(-jnp.expm1(-dt)),
        'dn_norm': 1.0 + nrm(ks[12], (DEPTH, DN_HEAD_DIM), 0.02),
        'pool_w': nrm(ks[13], (DEPTH, POOL_GROUPS, POOL_GROUP_DIM, POOL_GROUP_DIM), POOL_GROUP_DIM ** -0.5),
        'pool_scale': 1.0 + nrm(ks[14], (DEPTH, POOL_WIDTH), 0.1),
        'w_out': nrm(ks[15], (DEPTH, MIX_WIDTH, D), MIX_WIDTH ** -0.5),
        'w_router': nrm(ks[16], (DEPTH, D, N_EXPERTS), D ** -0.5),
        'w_gate': nrm(ks[17], (DEPTH, N_EXPERTS, D, EXPERT_FF), D ** -0.5),
        'w_up': nrm(ks[18], (DEPTH, N_EXPERTS, D, EXPERT_FF), D ** -0.5),
        'w_down': nrm(ks[19], (DEPTH, N_EXPERTS, EXPERT_FF, D), EXPERT_FF ** -0.5),
        'norm_f': 1.0 + nrm(ks[20], (D,), 0.02),
    }


def reference(x, c, ctx, c_ctx, w_mod, b_mod, norm1, norm2, w_in, conv_w, a_log, dt_bias, dn_norm,
              pool_w, pool_scale, w_out, w_router, w_gate, w_up, w_down, norm_f):
    rows = x.shape[1] // GRID_W
    zero = jnp.zeros((ctx.shape[0], DN_HEADS, DN_HEAD_DIM, DN_HEAD_DIM), jnp.float32)
    z = ctx
    for l in range(DEPTH):
        mix_w = (w_in[l], conv_w[l], a_log[l], dt_bias[l], dn_norm[l], pool_w[l], pool_scale[l], w_out[l])
        csh1, csc1, cg1, csh2, csc2, cg2 = adaln_params(c_ctx, w_mod[l], b_mod[l])
        sh1, sc1, g1, sh2, sc2, g2 = [t[:, None, :] for t in adaln_params(c, w_mod[l], b_mod[l])]
        hz = modulate(rmsnorm(z, norm1[l]), csh1, csc1)
        if l < DEPTH - 1:
            yz, ctx_st = token_mixer(hz, *mix_w, (zero, zero), None)
            z = z + cg1 * yz
            hz2 = modulate(rmsnorm(z, norm2[l]), csh2, csc2)
            z = z + cg2 * expert_choice_ffn(hz2, w_router[l], w_gate[l], w_up[l], w_down[l])
        else:
            ctx_st = context_states(hz, w_in[l], conv_w[l], a_log[l], dt_bias[l], (zero, zero))
        h = modulate(rmsnorm(x, norm1[l]), sh1, sc1)
        y, _ = token_mixer(h, *mix_w, ctx_st, rows)
        x = x + g1 * y
        h = modulate(rmsnorm(x, norm2[l]), sh2, sc2)
        x = x + g2 * expert_choice_ffn(h, w_router[l], w_gate[l], w_up[l], w_down[l])
    return rmsnorm(x, norm_f)
```

```python
import functools

import jax
import jax.numpy as jnp
from jax import lax
from jax.experimental import pallas as pl
from jax.experimental.pallas import tpu as pltpu

F32 = jnp.float32
BF16 = jnp.bfloat16
HIGHEST = lax.Precision.HIGHEST

LANES = 128
GRID_W = 64
DN_HEADS = 4
N_DIR = 2
CHUNK = 64
POOL_WINDOWS = (2, 4, 8, 16)
N_EXPERTS = 16
EC_CAPACITY = 2
EPS = 1e-6
VMEM_LIMIT = 56 * 1024 * 1024

NT_DIMS = (((1,), (1,)), ((), ()))
TN_DIMS = (((0,), (0,)), ((), ()))


def _params(*semantics):
    return pltpu.CompilerParams(dimension_semantics=semantics, vmem_limit_bytes=VMEM_LIMIT)


def _silu(x):
    return x * jax.nn.sigmoid(x)


def _dot(a, b):
    return jnp.dot(a, b, preferred_element_type=F32)


def _indicator(mask, dtype=F32):
    return jnp.where(mask, 1.0, 0.0).astype(dtype)


def _mod_kernel(cond_ref, w_ref, b_ref, o_ref):
    s = _silu(cond_ref[...])
    s_hi = s.astype(BF16)
    s_lo = (s - s_hi.astype(F32)).astype(BF16)
    w = w_ref[0]
    w_hi = w.astype(BF16)
    w_lo = (w - w_hi.astype(F32)).astype(BF16)
    rows = s.shape[0]
    both = _dot(jnp.concatenate([s_hi, s_lo], axis=0), w_hi)
    o_ref[0] = both[:rows] + both[rows:] + _dot(s_hi, w_lo) + b_ref[0]


def _adaln_all(cond, w_mod, b_mod):
    depth, d, n = w_mod.shape
    rows = cond.shape[0]
    tn = 1536
    return pl.pallas_call(
        _mod_kernel,
        grid=(depth, n // tn),
        in_specs=[pl.BlockSpec((rows, d), lambda l, j: (0, 0)),
                  pl.BlockSpec((1, d, tn), lambda l, j: (l, 0, j)),
                  pl.BlockSpec((1, 1, tn), lambda l, j: (l, 0, j))],
        out_specs=pl.BlockSpec((1, rows, tn), lambda l, j: (l, 0, j)),
        out_shape=jax.ShapeDtypeStruct((depth, rows, n), F32),
        compiler_params=_params("arbitrary", "arbitrary"),
        name="adaln",
    )(cond, w_mod, b_mod.reshape(depth, 1, n))


def _norm_mod(x, nw, shift, scale):
    ms = jnp.mean(x * x, axis=-1, keepdims=True)
    return (x * lax.rsqrt(ms + EPS) * nw) * (1.0 + scale) + shift


def _in_kernel(x_ref, sh_ref, sc_ref, nw_ref, wqkv_ref, wab_ref, wgate_ref, wpool_ref,
               qkv_ref, ab_ref, gate_ref, pool_ref):
    h = _norm_mod(x_ref[0], nw_ref[...], sh_ref[0], sc_ref[0]).astype(BF16)
    qkv_ref[0] = _dot(h, wqkv_ref[...])
    ab_ref[0] = _dot(h, wab_ref[...])
    gate_ref[0] = _dot(h, wgate_ref[...])
    pool_ref[0] = _dot(h, wpool_ref[...])


def _in_proj(x, shift, scale, nw, wqkv, wab, wgate, wpool, tm):
    b, l, d = x.shape
    row = lambda i, j: (i, 0, 0)
    tok = lambda i, j: (i, j, 0)
    fixed = lambda i, j: (0, 0)
    widths = (wqkv.shape[1], wab.shape[1], wgate.shape[1], wpool.shape[1])
    return pl.pallas_call(
        _in_kernel,
        grid=(b, l // tm),
        in_specs=[pl.BlockSpec((1, tm, d), tok),
                  pl.BlockSpec((1, 1, d), row), pl.BlockSpec((1, 1, d), row),
                  pl.BlockSpec((1, d), fixed)]
                 + [pl.BlockSpec((d, n), fixed) for n in widths],
        out_specs=[pl.BlockSpec((1, tm, n), tok) for n in widths],
        out_shape=[jax.ShapeDtypeStruct((b, l, n), F32) for n in widths],
        compiler_params=_params("arbitrary", "arbitrary"),
        name="in_proj",
    )(x, shift, scale, nw, wqkv, wab, wgate, wpool)


def _dn_kernel(alog_ref, dtb_ref, q_ref, k_ref, v_ref, ab_ref, gate_ref, cwq_ref, cwk_ref, cwv_ref,
               dnw_ref, s0f_ref, s0b_ref, o_ref, sf_ref, sb_ref,
               qs, ks, vs, gf, gb, gl, bet, of_s, ob_s):
    head = pl.program_id(1)
    seq = q_ref.shape[1]
    n_chunks = seq // CHUNK
    row = lax.broadcasted_iota(jnp.int32, (seq, LANES), 0)

    def conv_silu(x_ref, cw_ref):
        x = x_ref[0]
        w = cw_ref[...]
        y = (jnp.where(row >= 2, pltpu.roll(x, 2, 0), 0.0) * w[0:1]
             + jnp.where(row >= 1, pltpu.roll(x, 1, 0), 0.0) * w[1:2]
             + x * w[2:3]
             + jnp.where(row < seq - 1, pltpu.roll(x, seq - 1, 0), 0.0) * w[3:4])
        return _silu(y)

    def l2norm(x):
        return x * lax.rsqrt(jnp.sum(x * x, axis=-1, keepdims=True) + EPS)

    qs[...] = l2norm(conv_silu(q_ref, cwq_ref)) * (LANES ** -0.5)
    ks[...] = l2norm(conv_silu(k_ref, cwk_ref))
    vs[...] = conv_silu(v_ref, cwv_ref)

    lane = lax.broadcasted_iota(jnp.int32, (1, LANES), 1)
    abh = pltpu.roll(ab_ref[0], jnp.where(head == 0, 0, LANES - head), 1)
    a_log = jnp.where(lane == DN_HEADS, alog_ref[1, head], alog_ref[0, head])
    dt_b = jnp.where(lane == DN_HEADS, dtb_ref[1, head], dtb_ref[0, head])
    bet[...] = jax.nn.sigmoid(abh)
    log_a = -jnp.exp(a_log) * jax.nn.softplus(abh + dt_b)
    gl[...] = log_a

    ii = lax.broadcasted_iota(jnp.int32, (CHUNK, CHUNK), 0)
    jj = lax.broadcasted_iota(jnp.int32, (CHUNK, CHUNK), 1)
    sums = jnp.concatenate([_indicator(ii >= jj), _indicator(ii <= jj), jnp.ones((CHUNK, CHUNK), F32)], axis=0)

    def cumsum_chunk(n, carry):
        rows = pl.ds(pl.multiple_of(n * CHUNK, CHUNK), CHUNK)
        r = jnp.dot(sums, gl[rows, :], preferred_element_type=F32, precision=HIGHEST)
        gf[rows, :] = r[:CHUNK]
        gb[rows, :] = r[CHUNK:2 * CHUNK]
        gl[rows, :] = r[2 * CHUNK:]
        return carry

    lax.fori_loop(0, n_chunks, cumsum_chunk, 0)

    eye = _indicator(ii == jj)

    def chunk_step(n, state, direction):
        rows = pl.ds(pl.multiple_of(n * CHUNK, CHUNK), CHUNK)
        col = DN_HEADS * direction
        q, k, v = qs[rows, :], ks[rows, :], vs[rows, :]
        g = (gf if direction == 0 else gb)[rows, :][:, col:col + 1]
        g_last = gl[rows, :][:, col:col + 1]
        beta = bet[rows, :][:, 2 * DN_HEADS + col:2 * DN_HEADS + col + 1]
        if direction == 0:
            incl, strict = ii >= jj, ii > jj
        else:
            incl, strict = ii <= jj, ii < jj
        g_cols = jnp.broadcast_to(g, (CHUNK, LANES)).T[:CHUNK, :]
        decay = jnp.exp(jnp.minimum(g - g_cols, 0.0))
        e_g = jnp.exp(g)
        k_beta = k * beta
        both = lax.dot_general(jnp.concatenate([k_beta, q], axis=0).astype(BF16), k.astype(BF16), NT_DIMS,
                               preferred_element_type=F32)
        m = jnp.where(strict, both[:CHUNK] * decay, 0.0)
        qk = jnp.where(incl, both[CHUNK:] * decay, 0.0)
        p = -m
        t_inv = eye + p
        for _ in range(CHUNK.bit_length() - 2):
            pb = p.astype(BF16)
            p = _dot(pb, pb)
            t_inv = _dot(t_inv.astype(BF16), (eye + p).astype(BF16))
        wu = _dot(t_inv.astype(BF16), jnp.concatenate([k_beta * e_g, v * beta], axis=1).astype(BF16))
        w_blk, u_blk = wu[:, :LANES], wu[:, LANES:]
        ws = _dot(jnp.concatenate([w_blk, q * e_g], axis=0).astype(BF16), state.astype(BF16))
        u_new = (u_blk - ws[:CHUNK]).astype(BF16)
        out = ws[CHUNK:] + _dot(qk.astype(BF16), u_new)
        k_tail = (k * jnp.exp(g_last - g)).astype(BF16)
        new_state = state * jnp.exp(g_last[0:1, :]) + lax.dot_general(k_tail, u_new, TN_DIMS,
                                                                     preferred_element_type=F32)
        return out, new_state

    def pair(i, states):
        s_f, s_b = states
        out_f, s_f = chunk_step(i, s_f, 0)
        of_s[pl.ds(pl.multiple_of(i * CHUNK, CHUNK), CHUNK), :] = out_f
        nb = n_chunks - 1 - i
        out_b, s_b = chunk_step(nb, s_b, 1)
        ob_s[pl.ds(pl.multiple_of(nb * CHUNK, CHUNK), CHUNK), :] = out_b
        return s_f, s_b

    s_f, s_b = lax.fori_loop(0, n_chunks, pair, (s0f_ref[0, 0], s0b_ref[0, 0]))
    sf_ref[0, 0] = s_f
    sb_ref[0, 0] = s_b

    o = of_s[...] + ob_s[...]
    y = o * lax.rsqrt(jnp.mean(o * o, axis=-1, keepdims=True) + EPS) * dnw_ref[...]
    o_ref[0] = (y * _silu(gate_ref[0])).astype(BF16)


def _delta_net(qkv, ab, gate, conv_w, a_log, dt_bias, dn_norm, s0f, s0b):
    b, seq, _ = qkv.shape
    hd = LANES
    smem = pl.BlockSpec(memory_space=pltpu.SMEM)
    col = lambda off: pl.BlockSpec((1, seq, hd), lambda i, h: (i, 0, off + h))
    cw = lambda off: pl.BlockSpec((conv_w.shape[0], hd), lambda i, h: (0, off + h))
    st = pl.BlockSpec((1, 1, hd, hd), lambda i, h: (i, h, 0, 0))
    seq_buf = pltpu.VMEM((seq, hd), F32)
    return pl.pallas_call(
        _dn_kernel,
        grid=(b, DN_HEADS),
        in_specs=[smem, smem, col(0), col(DN_HEADS), col(2 * DN_HEADS),
                  pl.BlockSpec((1, seq, LANES), lambda i, h: (i, 0, 0)),
                  col(0), cw(0), cw(DN_HEADS), cw(2 * DN_HEADS),
                  pl.BlockSpec((1, hd), lambda i, h: (0, 0)), st, st],
        out_specs=[col(0), st, st],
        out_shape=[jax.ShapeDtypeStruct((b, seq, DN_HEADS * hd), BF16),
                   jax.ShapeDtypeStruct((b, DN_HEADS, hd, hd), F32),
                   jax.ShapeDtypeStruct((b, DN_HEADS, hd, hd), F32)],
        scratch_shapes=[seq_buf] * 9,
        compiler_params=_params("arbitrary", "arbitrary"),
        name="delta_net",
    )(a_log, dt_bias, qkv, qkv, qkv, ab, gate, conv_w, conv_w, conv_w, dn_norm, s0f, s0b)


def _window_sum(x, pos, limit, half, stride, seq):
    left = jnp.where(pos >= 1, pltpu.roll(x, stride, 0), 0.0)
    right = x
    k = 1
    while k < half:
        left = left + jnp.where(pos >= k, pltpu.roll(left, k * stride, 0), 0.0)
        right = right + jnp.where(pos + k < limit, pltpu.roll(right, seq - k * stride, 0), 0.0)
        k *= 2
    return left + right


def _window_count(pos, limit, half):
    return (jnp.minimum(pos + half, limit) - jnp.maximum(pos - half, 0)).astype(F32)


def _pool_kernel(u_ref, pw_ref, ps_ref, o_ref, *, rows):
    seq = u_ref.shape[1]
    t = lax.broadcasted_iota(jnp.int32, (seq, LANES), 0)
    for g, w in enumerate(POOL_WINDOWS):
        lanes = slice(g * LANES, (g + 1) * LANES)
        x = u_ref[0, :, lanes]
        half = w // 2
        if rows is None:
            total = _window_sum(x, t, seq, half, 1, seq)
            count = _window_count(t, seq, half)
        else:
            c, r = t & (GRID_W - 1), t >> (GRID_W.bit_length() - 1)
            total = _window_sum(_window_sum(x, c, GRID_W, half, 1, seq), r, rows, half, GRID_W, seq)
            count = _window_count(r, rows, half) * _window_count(c, GRID_W, half)
        m = (total / count - x).astype(BF16)
        o_ref[0, :, lanes] = (_dot(m, pw_ref[g]) * ps_ref[:, lanes]).astype(BF16)


def _pool_mixer(u, pool_w, pool_scale, rows):
    b, seq, width = u.shape
    blk = pl.BlockSpec((1, seq, width), lambda i: (i, 0, 0))
    return pl.pallas_call(
        functools.partial(_pool_kernel, rows=rows),
        grid=(b,),
        in_specs=[blk, pl.BlockSpec(pool_w.shape, lambda i: (0, 0, 0)), pl.BlockSpec((1, width), lambda i: (0, 0))],
        out_specs=blk,
        out_shape=jax.ShapeDtypeStruct((b, seq, width), BF16),
        compiler_params=_params("arbitrary"),
        name="pool_mixer",
    )(u, pool_w, pool_scale)


def _out_kernel(o_ref, p_ref, x_ref, wo_ref, g1_ref, nw_ref, sh_ref, sc_ref, wr_ref, x1_ref, h2_ref, lg_ref):
    half = o_ref.shape[2]
    y = _dot(o_ref[0], wo_ref[:half, :]) + _dot(p_ref[0], wo_ref[half:, :])
    x1 = x_ref[0] + g1_ref[0] * y
    x1_ref[0] = x1
    h2 = _norm_mod(x1, nw_ref[...], sh_ref[0], sc_ref[0]).astype(BF16)
    h2_ref[0] = h2
    lg_ref[0] = _dot(h2, wr_ref[...])


def _out_proj(o, pooled, x, w_out, g1, nw, shift, scale, w_router, tm):
    b, l, d = x.shape
    half = o.shape[2]
    row = lambda i, j: (i, 0, 0)
    tok = lambda i, j: (i, j, 0)
    fixed = lambda i, j: (0, 0)
    vec = pl.BlockSpec((1, 1, d), row)
    return pl.pallas_call(
        _out_kernel,
        grid=(b, l // tm),
        in_specs=[pl.BlockSpec((1, tm, half), tok), pl.BlockSpec((1, tm, half), tok), pl.BlockSpec((1, tm, d), tok),
                  pl.BlockSpec(w_out.shape, fixed), vec, pl.BlockSpec((1, d), fixed), vec, vec,
                  pl.BlockSpec(w_router.shape, fixed)],
        out_specs=[pl.BlockSpec((1, tm, d), tok), pl.BlockSpec((1, tm, d), tok), pl.BlockSpec((1, tm, LANES), tok)],
        out_shape=[jax.ShapeDtypeStruct((b, l, d), F32), jax.ShapeDtypeStruct((b, l, d), BF16),
                   jax.ShapeDtypeStruct((b, l, LANES), F32)],
        compiler_params=_params("arbitrary", "arbitrary"),
        name="out_proj",
    )(o, pooled, x, w_out, g1, nw, shift, scale, w_router)


TRI_ROWS = 256


def _route_kernel(lg_ref, posc_ref, posr_ref, aff_ref, tri_ref, *, cap):
    seq = lg_ref.shape[1]

    @pl.when(pl.program_id(0) == 0)
    def _():
        def fill(i, carry):
            r0 = pl.multiple_of(i * TRI_ROWS, TRI_ROWS)
            r = lax.broadcasted_iota(jnp.int32, (TRI_ROWS, seq), 0) + r0
            c = lax.broadcasted_iota(jnp.int32, (TRI_ROWS, seq), 1)
            tri_ref[pl.ds(r0, TRI_ROWS), :] = _indicator(r > c, BF16)
            return carry
        lax.fori_loop(0, seq // TRI_ROWS, fill, 0)

    lane = lax.broadcasted_iota(jnp.int32, (1, LANES), 1)
    is_expert = lane < N_EXPERTS
    lg = jnp.where(is_expert, lg_ref[0], -jnp.inf)
    ex = jnp.exp(lg - jnp.max(lg, axis=-1, keepdims=True))
    aff = ex / jnp.sum(ex, axis=-1, keepdims=True)
    aff_ref[0] = aff

    def bit_step(it, lo_bits):
        cand_bits = lo_bits | jnp.left_shift(jnp.int32(1), 30 - it)
        cand = lax.bitcast_convert_type(cand_bits, F32)
        count = jnp.sum(_indicator(aff >= cand), axis=0, keepdims=True)
        return jnp.where(count >= cap, cand_bits, lo_bits)

    lo_bits = lax.fori_loop(0, 31, bit_step, jnp.zeros((1, LANES), jnp.int32))
    lo = lax.bitcast_convert_type(lo_bits, F32)
    hi = lax.bitcast_convert_type(lo_bits + 1, F32)
    above = (aff >= hi) & is_expert
    tied = (aff >= lo) & (aff < hi) & is_expert
    need = cap - jnp.sum(_indicator(above), axis=0, keepdims=True)
    flags = (_indicator(above) + pltpu.roll(_indicator(tied), N_EXPERTS, 1)).astype(BF16)
    before = _dot(tri_ref[...], flags)
    tied_before = pltpu.roll(before, LANES - N_EXPERTS, 1)
    chosen = above | (tied & (tied_before < need))
    slot = before + jnp.minimum(tied_before, need)
    slot = jnp.where(chosen, slot, -1.0)
    posc_ref[0] = slot
    posr_ref[0] = slot.T[:N_EXPERTS, :]


def _route(logits, cap):
    b, seq, _ = logits.shape
    blk = pl.BlockSpec((1, seq, LANES), lambda i: (i, 0, 0))
    return pl.pallas_call(
        functools.partial(_route_kernel, cap=cap),
        grid=(b,),
        in_specs=[blk],
        out_specs=[blk, pl.BlockSpec((1, N_EXPERTS, seq), lambda i: (i, 0, 0)), blk],
        out_shape=[jax.ShapeDtypeStruct((b, seq, LANES), F32), jax.ShapeDtypeStruct((b, N_EXPERTS, seq), F32),
                   jax.ShapeDtypeStruct((b, seq, LANES), F32)],
        scratch_shapes=[pltpu.VMEM((seq, seq), BF16)],
        compiler_params=_params("arbitrary"),
        name="route",
    )(logits)


def _expert_kernel(h_ref, pos_ref, wg_ref, wu_ref, wd_ref, ye_ref, wg_s, wu_s, wd_s, *, cap):
    @pl.when(pl.program_id(1) == 0)
    def _():
        wg_s[...] = wg_ref[0].astype(BF16)
        wu_s[...] = wu_ref[0].astype(BF16)
        wd_s[...] = wd_ref[0].astype(BF16)

    seq = h_ref.shape[1]
    slots = lax.broadcasted_iota(jnp.int32, (cap, seq), 0).astype(F32)
    pick = _indicator(slots == pos_ref[0], BF16)
    xe = _dot(pick, h_ref[0]).astype(BF16)
    hid = (_silu(_dot(xe, wg_s[...])) * _dot(xe, wu_s[...])).astype(BF16)
    ye_ref[0, 0] = _dot(hid, wd_s[...]).astype(BF16)


def _experts(h2, pos_rows, w_gate, w_up, w_down, cap):
    b, seq, d = h2.shape
    n_exp, _, ff = w_gate.shape
    wspec = lambda shape: pl.BlockSpec((1,) + shape, lambda e, i: (e, 0, 0))
    return pl.pallas_call(
        functools.partial(_expert_kernel, cap=cap),
        grid=(n_exp, b),
        in_specs=[pl.BlockSpec((1, seq, d), lambda e, i: (i, 0, 0)),
                  pl.BlockSpec((1, 1, seq), lambda e, i: (i * n_exp + e, 0, 0)),
                  wspec((d, ff)), wspec((d, ff)), wspec((ff, d))],
        out_specs=pl.BlockSpec((1, 1, cap, d), lambda e, i: (i, e, 0, 0)),
        out_shape=jax.ShapeDtypeStruct((b, n_exp, cap, d), BF16),
        scratch_shapes=[pltpu.VMEM((d, ff), BF16), pltpu.VMEM((d, ff), BF16), pltpu.VMEM((ff, d), BF16)],
        compiler_params=_params("arbitrary", "arbitrary"),
        name="experts",
    )(h2, pos_rows.reshape(b * n_exp, 1, seq), w_gate, w_up, w_down)


def _combine_kernel(pos_ref, aff_ref, ye_ref, x_ref, g2_ref, o_ref, *, cap):
    tm = x_ref.shape[1]
    slots = lax.broadcasted_iota(jnp.int32, (tm, cap), 1).astype(F32)
    pos = pos_ref[0]
    aff = aff_ref[0]
    acc = jnp.zeros(x_ref.shape[1:], F32)
    for e in range(N_EXPERTS):
        place = _indicator(pos[:, e:e + 1] == slots, BF16)
        acc = acc + aff[:, e:e + 1] * _dot(place, ye_ref[0, e])
    o_ref[0] = x_ref[0] + g2_ref[0] * acc


def _combine(pos_cols, aff, ye, x1, g2, cap, tm):
    b, l, d = x1.shape
    tok = lambda i, j: (i, j, 0)
    return pl.pallas_call(
        functools.partial(_combine_kernel, cap=cap),
        grid=(b, l // tm),
        in_specs=[pl.BlockSpec((1, tm, LANES), tok), pl.BlockSpec((1, tm, LANES), tok),
                  pl.BlockSpec((1, N_EXPERTS, cap, d), lambda i, j: (i, 0, 0, 0)),
                  pl.BlockSpec((1, tm, d), tok), pl.BlockSpec((1, 1, d), lambda i, j: (i, 0, 0))],
        out_specs=pl.BlockSpec((1, tm, d), tok),
        out_shape=jax.ShapeDtypeStruct((b, l, d), F32),
        compiler_params=_params("arbitrary", "arbitrary"),
        name="combine",
    )(pos_cols, aff, ye, x1, g2)


def _final_kernel(x_ref, nw_ref, o_ref):
    x = x_ref[0]
    o_ref[0] = x * lax.rsqrt(jnp.mean(x * x, axis=-1, keepdims=True) + EPS) * nw_ref[...]


def _final_norm(x, nw, tm):
    b, l, d = x.shape
    tok = lambda i, j: (i, j, 0)
    return pl.pallas_call(
        _final_kernel,
        grid=(b, l // tm),
        in_specs=[pl.BlockSpec((1, tm, d), tok), pl.BlockSpec((1, d), lambda i, j: (0, 0))],
        out_specs=pl.BlockSpec((1, tm, d), tok),
        out_shape=jax.ShapeDtypeStruct((b, l, d), F32),
        compiler_params=_params("arbitrary", "arbitrary"),
        name="final_norm",
    )(x, nw)


def _token_mixer(x, mod, lw, states, rows, tm):
    sh1, sc1 = mod[0], mod[1]
    qkv, ab, gate, pool = _in_proj(x, sh1, sc1, lw["norm1"], lw["wqkv"], lw["wab"], lw["wgate"], lw["wpool"], tm)
    o, s_f, s_b = _delta_net(qkv, ab, gate, lw["conv_w"], lw["a_log"], lw["dt_bias"], lw["dn_norm"], *states)
    pooled = _pool_mixer(pool, lw["pool_w"], lw["pool_scale"], rows)
    return o, pooled, (s_f, s_b)


def _ffn(o, pooled, x, mod, lw, tm):
    seq = x.shape[1]
    cap = EC_CAPACITY * seq // N_EXPERTS
    x1, h2, logits = _out_proj(o, pooled, x, lw["w_out"], mod[2], lw["norm2"], mod[3], mod[4], lw["w_router"], tm)
    pos_cols, pos_rows, aff = _route(logits, cap)
    ye = _experts(h2, pos_rows, lw["w_gate"], lw["w_up"], lw["w_down"], cap)
    return _combine(pos_cols, aff, ye, x1, mod[5], cap, tm)


def kernel(x, c, ctx, c_ctx, w_mod, b_mod, norm1, norm2, w_in, conv_w, a_log, dt_bias, dn_norm, pool_w, pool_scale,
           w_out, w_router, w_gate, w_up, w_down, norm_f):
    batch, seq, d = x.shape
    depth = w_mod.shape[0]
    ctx_len = ctx.shape[1]
    dn_width = DN_HEADS * LANES
    qkv_cols = 3 * dn_width
    gate_cols = 2 * N_DIR * DN_HEADS
    state_cols = qkv_cols + gate_cols
    rows = seq // GRID_W
    tm_x, tm_z = 512, ctx_len

    cond_rows = 16
    cond = jnp.zeros((cond_rows, d), F32).at[:batch].set(c).at[batch].set(c_ctx)
    mod_all = _adaln_all(cond, w_mod, b_mod)

    zero_state = jnp.zeros((batch, DN_HEADS, LANES, LANES), F32)
    z = ctx
    for l in range(depth):
        wl = w_in[l]
        lw = dict(
            norm1=norm1[l][None], norm2=norm2[l][None],
            wqkv=wl[:, :qkv_cols].astype(BF16),
            wab=jnp.pad(wl[:, qkv_cols:state_cols], ((0, 0), (0, LANES - gate_cols))).astype(BF16),
            wgate=wl[:, state_cols:state_cols + dn_width].astype(BF16),
            wpool=wl[:, state_cols + dn_width:].astype(BF16),
            conv_w=conv_w[l], a_log=a_log[l], dt_bias=dt_bias[l], dn_norm=dn_norm[l][None],
            pool_w=pool_w[l].astype(BF16), pool_scale=pool_scale[l][None],
            w_out=w_out[l].astype(BF16),
            w_router=jnp.pad(w_router[l], ((0, 0), (0, LANES - N_EXPERTS))).astype(BF16),
            w_gate=w_gate[l], w_up=w_up[l], w_down=w_down[l],
        )
        mods = mod_all[l].reshape(cond_rows, 6, d)
        mod_x = [mods[:batch, i][:, None, :] for i in range(6)]
        mod_z = [jnp.broadcast_to(mods[batch, i][None, None, :], (batch, 1, d)) for i in range(6)]

        o_z, pooled_z, ctx_states = _token_mixer(z, mod_z, lw, (zero_state, zero_state), None, tm_z)
        if l < depth - 1:
            z = _ffn(o_z, pooled_z, z, mod_z, lw, tm_z)
        o_x, pooled_x, _ = _token_mixer(x, mod_x, lw, ctx_states, rows, tm_x)
        x = _ffn(o_x, pooled_x, x, mod_x, lw, tm_x)
    return _final_norm(x, norm_f[None], tm_x)
```

```python
import functools

import jax
import jax.numpy as jnp
from jax import lax
from jax.experimental import pallas as pl
from jax.experimental.pallas import tpu as pltpu

F32 = jnp.float32
BF16 = jnp.bfloat16
HIGHEST = lax.Precision.HIGHEST

LANES = 128
GRID_W = 64
DN_HEADS = 4
N_DIR = 2
CHUNK = 64
DN_GROUP = 8
POOL_WINDOWS = (2, 4, 8, 16)
N_EXPERTS = 16
EC_CAPACITY = 2
EPS = 1e-6
VMEM_LIMIT = 56 * 1024 * 1024

NT_DIMS = (((1,), (1,)), ((), ()))
TN_DIMS = (((0,), (0,)), ((), ()))


def _params(*semantics):
    return pltpu.CompilerParams(dimension_semantics=semantics, vmem_limit_bytes=VMEM_LIMIT)


def _silu(x):
    return x * jax.nn.sigmoid(x)


def _dot(a, b):
    return jnp.dot(a, b, preferred_element_type=F32)


def _indicator(mask, dtype=F32):
    return jnp.where(mask, 1.0, 0.0).astype(dtype)


def _mod_kernel(cond_ref, w_ref, b_ref, o_ref):
    s = _silu(cond_ref[...])
    s_hi = s.astype(BF16)
    s_lo = (s - s_hi.astype(F32)).astype(BF16)
    w = w_ref[0]
    w_hi = w.astype(BF16)
    w_lo = (w - w_hi.astype(F32)).astype(BF16)
    rows = s.shape[0]
    both = _dot(jnp.concatenate([s_hi, s_lo], axis=0), w_hi)
    o_ref[0] = both[:rows] + both[rows:] + _dot(s_hi, w_lo) + b_ref[0]


def _adaln_all(cond, w_mod, b_mod):
    depth, d, n = w_mod.shape
    rows = cond.shape[0]
    tn = 1536
    return pl.pallas_call(
        _mod_kernel,
        grid=(depth, n // tn),
        in_specs=[pl.BlockSpec((rows, d), lambda l, j: (0, 0)),
                  pl.BlockSpec((1, d, tn), lambda l, j: (l, 0, j)),
                  pl.BlockSpec((1, 1, tn), lambda l, j: (l, 0, j))],
        out_specs=pl.BlockSpec((1, rows, tn), lambda l, j: (l, 0, j)),
        out_shape=jax.ShapeDtypeStruct((depth, rows, n), F32),
        compiler_params=_params("arbitrary", "arbitrary"),
        name="adaln",
    )(cond, w_mod, b_mod.reshape(depth, 1, n))


def _norm_mod(x, nw, shift, scale):
    ms = jnp.mean(x * x, axis=-1, keepdims=True)
    return (x * lax.rsqrt(ms + EPS) * nw) * (1.0 + scale) + shift


def _in_kernel(x_ref, sh_ref, sc_ref, nw_ref, wqkv_ref, wab_ref, wgate_ref, wpool_ref,
               qkv_ref, ab_ref, gate_ref, pool_ref):
    h = _norm_mod(x_ref[0], nw_ref[...], sh_ref[0], sc_ref[0]).astype(BF16)
    qkv_ref[0] = _dot(h, wqkv_ref[...])
    ab_ref[0] = _dot(h, wab_ref[...])
    gate_ref[0] = _dot(h, wgate_ref[...])
    pool_ref[0] = _dot(h, wpool_ref[...])


def _in_proj(x, shift, scale, nw, wqkv, wab, wgate, wpool, tm):
    b, l, d = x.shape
    row = lambda i, j: (i, 0, 0)
    tok = lambda i, j: (i, j, 0)
    fixed = lambda i, j: (0, 0)
    widths = (wqkv.shape[1], wab.shape[1], wgate.shape[1], wpool.shape[1])
    return pl.pallas_call(
        _in_kernel,
        grid=(b, l // tm),
        in_specs=[pl.BlockSpec((1, tm, d), tok),
                  pl.BlockSpec((1, 1, d), row), pl.BlockSpec((1, 1, d), row),
                  pl.BlockSpec((1, d), fixed)]
                 + [pl.BlockSpec((d, n), fixed) for n in widths],
        out_specs=[pl.BlockSpec((1, tm, n), tok) for n in widths],
        out_shape=[jax.ShapeDtypeStruct((b, l, n), F32) for n in widths],
        compiler_params=_params("arbitrary", "arbitrary"),
        name="in_proj",
    )(x, shift, scale, nw, wqkv, wab, wgate, wpool)


def _bdot(a, b):
    return jnp.einsum('gij,gjk->gik', a, b, preferred_element_type=F32)


def _dn_kernel(alog_ref, dtb_ref, q_ref, k_ref, v_ref, ab_ref, gate_ref, cwq_ref, cwk_ref, cwv_ref,
               dnw_ref, s0f_ref, s0b_ref, o_ref, sf_ref, sb_ref,
               qs, ks, vs, la, bet, gl, kn_f, kn_b, bn_f, bn_b, qn_f, qn_b, on_f, on_b, *, group):
    head = pl.program_id(1)
    seq = q_ref.shape[1]
    n_chunks = seq // CHUNK
    row = lax.broadcasted_iota(jnp.int32, (seq, LANES), 0)

    def conv_silu(x_ref, cw_ref):
        x = x_ref[0]
        w = cw_ref[...]
        y = (jnp.where(row >= 2, pltpu.roll(x, 2, 0), 0.0) * w[0:1]
             + jnp.where(row >= 1, pltpu.roll(x, 1, 0), 0.0) * w[1:2]
             + x * w[2:3]
             + jnp.where(row < seq - 1, pltpu.roll(x, seq - 1, 0), 0.0) * w[3:4])
        return _silu(y)

    def l2norm(x):
        return x * lax.rsqrt(jnp.sum(x * x, axis=-1, keepdims=True) + EPS)

    qs[...] = l2norm(conv_silu(q_ref, cwq_ref)) * (LANES ** -0.5)
    ks[...] = l2norm(conv_silu(k_ref, cwk_ref))
    vs[...] = conv_silu(v_ref, cwv_ref)

    lane = lax.broadcasted_iota(jnp.int32, (1, LANES), 1)
    abh = pltpu.roll(ab_ref[0], jnp.where(head == 0, 0, LANES - head), 1)
    a_log = jnp.where(lane == DN_HEADS, alog_ref[1, head], alog_ref[0, head])
    dt_b = jnp.where(lane == DN_HEADS, dtb_ref[1, head], dtb_ref[0, head])
    bet[...] = jax.nn.sigmoid(abh)
    la[...] = -jnp.exp(a_log) * jax.nn.softplus(abh + dt_b)

    ii = lax.broadcasted_iota(jnp.int32, (CHUNK, CHUNK), 0)
    jj = lax.broadcasted_iota(jnp.int32, (CHUNK, CHUNK), 1)
    sums = jnp.concatenate([_indicator(ii >= jj), _indicator(ii <= jj), jnp.ones((CHUNK, CHUNK), F32)], axis=0)
    eye = _indicator(ii == jj)
    span = group * CHUNK

    def wy_group(gi, carry):
        rows = pl.ds(pl.multiple_of(gi * span, span), span)
        state_rows = pl.ds(pl.multiple_of(gi * group * LANES, group * LANES), group * LANES)
        chunks = lambda a: a.reshape(group, CHUNK, a.shape[-1])
        q, k, v = chunks(qs[rows, :]), chunks(ks[rows, :]), chunks(vs[rows, :])
        log_a = chunks(la[rows, :])
        cums = jnp.stack([jnp.dot(sums, log_a[c], preferred_element_type=F32, precision=HIGHEST)
                          for c in range(group)])
        gl[rows, :] = cums[:, 2 * CHUNK:].reshape(span, LANES)
        beta_all = chunks(bet[rows, :])
        k16 = k.astype(BF16)
        for direction, (kn, bn, qn, on) in enumerate(((kn_f, bn_f, qn_f, on_f), (kn_b, bn_b, qn_b, on_b))):
            col = DN_HEADS * direction
            g = cums[:, direction * CHUNK:(direction + 1) * CHUNK, col:col + 1]
            g_last = cums[:, 2 * CHUNK:, col:col + 1]
            beta = beta_all[:, :, 2 * DN_HEADS + col:2 * DN_HEADS + col + 1]
            if direction == 0:
                incl, strict = ii >= jj, ii > jj
            else:
                incl, strict = ii <= jj, ii < jj
            g_cols = jnp.swapaxes(jnp.broadcast_to(g, (group, CHUNK, LANES)), 1, 2)[:, :CHUNK, :]
            decay = jnp.exp(jnp.minimum(g - g_cols, 0.0))
            e_g = jnp.exp(g)
            k_beta = k * beta
            both = jnp.einsum('gik,gjk->gij', jnp.concatenate([k_beta, q], axis=1).astype(BF16), k16,
                              preferred_element_type=F32)
            m = jnp.where(strict, both[:, :CHUNK] * decay, 0.0)
            qk = jnp.where(incl, both[:, CHUNK:] * decay, 0.0).astype(BF16)
            p = -m
            t_inv = eye + p
            pb = p.astype(BF16)
            p = _bdot(pb, pb)
            for _ in range(CHUNK.bit_length() - 3):
                pb = p.astype(BF16)
                r = _bdot(jnp.concatenate([pb, t_inv.astype(BF16)], axis=1), pb)
                p, t_inv = r[:, :CHUNK], t_inv + r[:, CHUNK:]
            t_inv = t_inv + _bdot(t_inv.astype(BF16), p.astype(BF16))
            wu = _bdot(t_inv.astype(BF16),
                       jnp.concatenate([k_beta * e_g, v * beta], axis=2).astype(BF16)).astype(BF16)
            k_tail = (k * jnp.exp(g_last - g)).astype(BF16)
            kb = jnp.einsum('gik,gin->gkn', k_tail, wu, preferred_element_type=F32)
            qb = _bdot(qk, wu)
            kn[state_rows, :] = kb[:, :, :LANES].reshape(group * LANES, LANES).astype(BF16)
            bn[state_rows, :] = kb[:, :, LANES:].reshape(group * LANES, LANES)
            qn[rows, :] = (q * e_g - qb[:, :, :LANES]).reshape(span, LANES).astype(BF16)
            on[rows, :] = qb[:, :, LANES:].reshape(span, LANES)
        return carry

    lax.fori_loop(0, n_chunks // group, wy_group, 0)

    def scan_chunk(n, state, direction, kn, bn, qn, on):
        col = DN_HEADS * direction
        rows = pl.ds(pl.multiple_of(n * CHUNK, CHUNK), CHUNK)
        state_rows = pl.ds(pl.multiple_of(n * LANES, LANES), LANES)
        s16 = state.astype(BF16)
        on[rows, :] = on[rows, :] + _dot(qn[rows, :], s16)
        s_decay = jnp.exp(gl[pl.ds(pl.multiple_of(n * CHUNK, CHUNK), 8), :][0:1, col:col + 1])
        return state * s_decay - _dot(kn[state_rows, :], s16) + bn[state_rows, :]

    def pair(i, states):
        s_f, s_b = states
        return (scan_chunk(i, s_f, 0, kn_f, bn_f, qn_f, on_f),
                scan_chunk(n_chunks - 1 - i, s_b, 1, kn_b, bn_b, qn_b, on_b))

    s_f, s_b = lax.fori_loop(0, n_chunks, pair, (s0f_ref[0, 0], s0b_ref[0, 0]))
    sf_ref[0, 0] = s_f
    sb_ref[0, 0] = s_b

    o = on_f[...] + on_b[...]
    y = o * lax.rsqrt(jnp.mean(o * o, axis=-1, keepdims=True) + EPS) * dnw_ref[...]
    o_ref[0] = (y * _silu(gate_ref[0])).astype(BF16)


def _delta_net(qkv, ab, gate, conv_w, a_log, dt_bias, dn_norm, s0f, s0b):
    b, seq, _ = qkv.shape
    hd = LANES
    smem = pl.BlockSpec(memory_space=pltpu.SMEM)
    col = lambda off: pl.BlockSpec((1, seq, hd), lambda i, h: (i, 0, off + h))
    cw = lambda off: pl.BlockSpec((conv_w.shape[0], hd), lambda i, h: (0, off + h))
    st = pl.BlockSpec((1, 1, hd, hd), lambda i, h: (i, h, 0, 0))
    n_chunks = seq // CHUNK
    group = min(DN_GROUP, n_chunks)
    seq_f32 = pltpu.VMEM((seq, hd), F32)
    seq_bf16 = pltpu.VMEM((seq, hd), BF16)
    mats_f32 = pltpu.VMEM((n_chunks * hd, hd), F32)
    mats_bf16 = pltpu.VMEM((n_chunks * hd, hd), BF16)
    return pl.pallas_call(
        functools.partial(_dn_kernel, group=group),
        grid=(b, DN_HEADS),
        in_specs=[smem, smem, col(0), col(DN_HEADS), col(2 * DN_HEADS),
                  pl.BlockSpec((1, seq, LANES), lambda i, h: (i, 0, 0)),
                  col(0), cw(0), cw(DN_HEADS), cw(2 * DN_HEADS),
                  pl.BlockSpec((1, hd), lambda i, h: (0, 0)), st, st],
        out_specs=[col(0), st, st],
        out_shape=[jax.ShapeDtypeStruct((b, seq, DN_HEADS * hd), BF16),
                   jax.ShapeDtypeStruct((b, DN_HEADS, hd, hd), F32),
                   jax.ShapeDtypeStruct((b, DN_HEADS, hd, hd), F32)],
        scratch_shapes=[seq_f32] * 6 + [mats_bf16] * 2 + [mats_f32] * 2 + [seq_bf16] * 2 + [seq_f32] * 2,
        compiler_params=_params("arbitrary", "arbitrary"),
        name="delta_net",
    )(a_log, dt_bias, qkv, qkv, qkv, ab, gate, conv_w, conv_w, conv_w, dn_norm, s0f, s0b)


def _window_sum(x, pos, limit, half, stride, seq):
    left = jnp.where(pos >= 1, pltpu.roll(x, stride, 0), 0.0)
    right = x
    k = 1
    while k < half:
        left = left + jnp.where(pos >= k, pltpu.roll(left, k * stride, 0), 0.0)
        right = right + jnp.where(pos + k < limit, pltpu.roll(right, seq - k * stride, 0), 0.0)
        k *= 2
    return left + right


def _window_count(pos, limit, half):
    return (jnp.minimum(pos + half, limit) - jnp.maximum(pos - half, 0)).astype(F32)


def _pool_kernel(u_ref, pw_ref, ps_ref, o_ref, *, rows):
    seq = u_ref.shape[1]
    t = lax.broadcasted_iota(jnp.int32, (seq, LANES), 0)
    for g, w in enumerate(POOL_WINDOWS):
        lanes = slice(g * LANES, (g + 1) * LANES)
        x = u_ref[0, :, lanes]
        half = w // 2
        if rows is None:
            total = _window_sum(x, t, seq, half, 1, seq)
            count = _window_count(t, seq, half)
        else:
            c, r = t & (GRID_W - 1), t >> (GRID_W.bit_length() - 1)
            total = _window_sum(_window_sum(x, c, GRID_W, half, 1, seq), r, rows, half, GRID_W, seq)
            count = _window_count(r, rows, half) * _window_count(c, GRID_W, half)
        m = (total / count - x).astype(BF16)
        o_ref[0, :, lanes] = (_dot(m, pw_ref[g]) * ps_ref[:, lanes]).astype(BF16)


def _pool_mixer(u, pool_w, pool_scale, rows):
    b, seq, width = u.shape
    blk = pl.BlockSpec((1, seq, width), lambda i: (i, 0, 0))
    return pl.pallas_call(
        functools.partial(_pool_kernel, rows=rows),
        grid=(b,),
        in_specs=[blk, pl.BlockSpec(pool_w.shape, lambda i: (0, 0, 0)), pl.BlockSpec((1, width), lambda i: (0, 0))],
        out_specs=blk,
        out_shape=jax.ShapeDtypeStruct((b, seq, width), BF16),
        compiler_params=_params("arbitrary"),
        name="pool_mixer",
    )(u, pool_w, pool_scale)


def _out_kernel(o_ref, p_ref, x_ref, wo_ref, g1_ref, nw_ref, sh_ref, sc_ref, wr_ref, x1_ref, h2_ref, lg_ref):
    half = o_ref.shape[2]
    y = _dot(o_ref[0], wo_ref[:half, :]) + _dot(p_ref[0], wo_ref[half:, :])
    x1 = x_ref[0] + g1_ref[0] * y
    x1_ref[0] = x1
    h2 = _norm_mod(x1, nw_ref[...], sh_ref[0], sc_ref[0]).astype(BF16)
    h2_ref[0] = h2
    lg_ref[0] = _dot(h2, wr_ref[...])


def _out_proj(o, pooled, x, w_out, g1, nw, shift, scale, w_router, tm):
    b, l, d = x.shape
    half = o.shape[2]
    row = lambda i, j: (i, 0, 0)
    tok = lambda i, j: (i, j, 0)
    fixed = lambda i, j: (0, 0)
    vec = pl.BlockSpec((1, 1, d), row)
    return pl.pallas_call(
        _out_kernel,
        grid=(b, l // tm),
        in_specs=[pl.BlockSpec((1, tm, half), tok), pl.BlockSpec((1, tm, half), tok), pl.BlockSpec((1, tm, d), tok),
                  pl.BlockSpec(w_out.shape, fixed), vec, pl.BlockSpec((1, d), fixed), vec, vec,
                  pl.BlockSpec(w_router.shape, fixed)],
        out_specs=[pl.BlockSpec((1, tm, d), tok), pl.BlockSpec((1, tm, d), tok), pl.BlockSpec((1, tm, LANES), tok)],
        out_shape=[jax.ShapeDtypeStruct((b, l, d), F32), jax.ShapeDtypeStruct((b, l, d), BF16),
                   jax.ShapeDtypeStruct((b, l, LANES), F32)],
        compiler_params=_params("arbitrary", "arbitrary"),
        name="out_proj",
    )(o, pooled, x, w_out, g1, nw, shift, scale, w_router)


TRI_ROWS = 256


def _route_kernel(lg_ref, posc_ref, posr_ref, aff_ref, tri_ref, *, cap):
    seq = lg_ref.shape[1]

    @pl.when(pl.program_id(0) == 0)
    def _():
        def fill(i, carry):
            r0 = pl.multiple_of(i * TRI_ROWS, TRI_ROWS)
            r = lax.broadcasted_iota(jnp.int32, (TRI_ROWS, seq), 0) + r0
            c = lax.broadcasted_iota(jnp.int32, (TRI_ROWS, seq), 1)
            tri_ref[pl.ds(r0, TRI_ROWS), :] = _indicator(r > c, BF16)
            return carry
        lax.fori_loop(0, seq // TRI_ROWS, fill, 0)

    lane = lax.broadcasted_iota(jnp.int32, (1, LANES), 1)
    is_expert = lane < N_EXPERTS
    lg = jnp.where(is_expert, lg_ref[0], -jnp.inf)
    ex = jnp.exp(lg - jnp.max(lg, axis=-1, keepdims=True))
    aff = ex / jnp.sum(ex, axis=-1, keepdims=True)
    aff_ref[0] = aff

    def bit_step(it, lo_bits):
        cand_bits = lo_bits | jnp.left_shift(jnp.int32(1), 30 - it)
        cand = lax.bitcast_convert_type(cand_bits, F32)
        count = jnp.sum(_indicator(aff >= cand), axis=0, keepdims=True)
        return jnp.where(count >= cap, cand_bits, lo_bits)

    lo_bits = lax.fori_loop(0, 31, bit_step, jnp.zeros((1, LANES), jnp.int32))
    lo = lax.bitcast_convert_type(lo_bits, F32)
    hi = lax.bitcast_convert_type(lo_bits + 1, F32)
    above = (aff >= hi) & is_expert
    tied = (aff >= lo) & (aff < hi) & is_expert
    need = cap - jnp.sum(_indicator(above), axis=0, keepdims=True)
    flags = (_indicator(above) + pltpu.roll(_indicator(tied), N_EXPERTS, 1)).astype(BF16)
    before = _dot(tri_ref[...], flags)
    tied_before = pltpu.roll(before, LANES - N_EXPERTS, 1)
    chosen = above | (tied & (tied_before < need))
    slot = before + jnp.minimum(tied_before, need)
    slot = jnp.where(chosen, slot, -1.0)
    posc_ref[0] = slot
    posr_ref[0] = slot.T[:N_EXPERTS, :]


def _route(logits, cap):
    b, seq, _ = logits.shape
    blk = pl.BlockSpec((1, seq, LANES), lambda i: (i, 0, 0))
    return pl.pallas_call(
        functools.partial(_route_kernel, cap=cap),
        grid=(b,),
        in_specs=[blk],
        out_specs=[blk, pl.BlockSpec((1, N_EXPERTS, seq), lambda i: (i, 0, 0)), blk],
        out_shape=[jax.ShapeDtypeStruct((b, seq, LANES), F32), jax.ShapeDtypeStruct((b, N_EXPERTS, seq), F32),
                   jax.ShapeDtypeStruct((b, seq, LANES), F32)],
        scratch_shapes=[pltpu.VMEM((seq, seq), BF16)],
        compiler_params=_params("arbitrary"),
        name="route",
    )(logits)


def _expert_kernel(h_ref, pos_ref, wg_ref, wu_ref, wd_ref, ye_ref, wg_s, wu_s, wd_s, *, cap):
    @pl.when(pl.program_id(1) == 0)
    def _():
        wg_s[...] = wg_ref[0].astype(BF16)
        wu_s[...] = wu_ref[0].astype(BF16)
        wd_s[...] = wd_ref[0].astype(BF16)

    seq = h_ref.shape[1]
    slots = lax.broadcasted_iota(jnp.int32, (cap, seq), 0).astype(F32)
    pick = _indicator(slots == pos_ref[0], BF16)
    xe = _dot(pick, h_ref[0]).astype(BF16)
    hid = (_silu(_dot(xe, wg_s[...])) * _dot(xe, wu_s[...])).astype(BF16)
    ye_ref[0, 0] = _dot(hid, wd_s[...]).astype(BF16)


def _experts(h2, pos_rows, w_gate, w_up, w_down, cap):
    b, seq, d = h2.shape
    n_exp, _, ff = w_gate.shape
    wspec = lambda shape: pl.BlockSpec((1,) + shape, lambda e, i: (e, 0, 0))
    return pl.pallas_call(
        functools.partial(_expert_kernel, cap=cap),
        grid=(n_exp, b),
        in_specs=[pl.BlockSpec((1, seq, d), lambda e, i: (i, 0, 0)),
                  pl.BlockSpec((1, 1, seq), lambda e, i: (i * n_exp + e, 0, 0)),
                  wspec((d, ff)), wspec((d, ff)), wspec((ff, d))],
        out_specs=pl.BlockSpec((1, 1, cap, d), lambda e, i: (i, e, 0, 0)),
        out_shape=jax.ShapeDtypeStruct((b, n_exp, cap, d), BF16),
        scratch_shapes=[pltpu.VMEM((d, ff), BF16), pltpu.VMEM((d, ff), BF16), pltpu.VMEM((ff, d), BF16)],
        compiler_params=_params("arbitrary", "arbitrary"),
        name="experts",
    )(h2, pos_rows.reshape(b * n_exp, 1, seq), w_gate, w_up, w_down)


def _combine_kernel(pos_ref, aff_ref, ye_ref, x_ref, g2_ref, o_ref, *, cap):
    tm = x_ref.shape[1]
    slots = lax.broadcasted_iota(jnp.int32, (tm, cap), 1).astype(F32)
    pos = pos_ref[0]
    aff = aff_ref[0]
    acc = jnp.zeros(x_ref.shape[1:], F32)
    for e in range(N_EXPERTS):
        place = _indicator(pos[:, e:e + 1] == slots, BF16)
        acc = acc + aff[:, e:e + 1] * _dot(place, ye_ref[0, e])
    o_ref[0] = x_ref[0] + g2_ref[0] * acc


def _combine(pos_cols, aff, ye, x1, g2, cap, tm):
    b, l, d = x1.shape
    tok = lambda i, j: (i, j, 0)
    return pl.pallas_call(
        functools.partial(_combine_kernel, cap=cap),
        grid=(b, l // tm),
        in_specs=[pl.BlockSpec((1, tm, LANES), tok), pl.BlockSpec((1, tm, LANES), tok),
                  pl.BlockSpec((1, N_EXPERTS, cap, d), lambda i, j: (i, 0, 0, 0)),
                  pl.BlockSpec((1, tm, d), tok), pl.BlockSpec((1, 1, d), lambda i, j: (i, 0, 0))],
        out_specs=pl.BlockSpec((1, tm, d), tok),
        out_shape=jax.ShapeDtypeStruct((b, l, d), F32),
        compiler_params=_params("arbitrary", "arbitrary"),
        name="combine",
    )(pos_cols, aff, ye, x1, g2)


def _final_kernel(x_ref, nw_ref, o_ref):
    x = x_ref[0]
    o_ref[0] = x * lax.rsqrt(jnp.mean(x * x, axis=-1, keepdims=True) + EPS) * nw_ref[...]


def _final_norm(x, nw, tm):
    b, l, d = x.shape
    tok = lambda i, j: (i, j, 0)
    return pl.pallas_call(
        _final_kernel,
        grid=(b, l // tm),
        in_specs=[pl.BlockSpec((1, tm, d), tok), pl.BlockSpec((1, d), lambda i, j: (0, 0))],
        out_specs=pl.BlockSpec((1, tm, d), tok),
        out_shape=jax.ShapeDtypeStruct((b, l, d), F32),
        compiler_params=_params("arbitrary", "arbitrary"),
        name="final_norm",
    )(x, nw)


def _token_mixer(x, mod, lw, states, rows, tm):
    sh1, sc1 = mod[0], mod[1]
    qkv, ab, gate, pool = _in_proj(x, sh1, sc1, lw["norm1"], lw["wqkv"], lw["wab"], lw["wgate"], lw["wpool"], tm)
    o, s_f, s_b = _delta_net(qkv, ab, gate, lw["conv_w"], lw["a_log"], lw["dt_bias"], lw["dn_norm"], *states)
    pooled = _pool_mixer(pool, lw["pool_w"], lw["pool_scale"], rows)
    return o, pooled, (s_f, s_b)


def _ffn(o, pooled, x, mod, lw, tm):
    seq = x.shape[1]
    cap = EC_CAPACITY * seq // N_EXPERTS
    x1, h2, logits = _out_proj(o, pooled, x, lw["w_out"], mod[2], lw["norm2"], mod[3], mod[4], lw["w_router"], tm)
    pos_cols, pos_rows, aff = _route(logits, cap)
    ye = _experts(h2, pos_rows, lw["w_gate"], lw["w_up"], lw["w_down"], cap)
    return _combine(pos_cols, aff, ye, x1, mod[5], cap, tm)


def kernel(x, c, ctx, c_ctx, w_mod, b_mod, norm1, norm2, w_in, conv_w, a_log, dt_bias, dn_norm, pool_w, pool_scale,
           w_out, w_router, w_gate, w_up, w_down, norm_f):
    batch, seq, d = x.shape
    depth = w_mod.shape[0]
    ctx_len = ctx.shape[1]
    dn_width = DN_HEADS * LANES
    qkv_cols = 3 * dn_width
    gate_cols = 2 * N_DIR * DN_HEADS
    state_cols = qkv_cols + gate_cols
    rows = seq // GRID_W
    tm_x, tm_z = 512, ctx_len

    cond_rows = 16
    cond = jnp.zeros((cond_rows, d), F32).at[:batch].set(c).at[batch].set(c_ctx)
    mod_all = _adaln_all(cond, w_mod, b_mod)

    zero_state = jnp.zeros((batch, DN_HEADS, LANES, LANES), F32)
    z = ctx
    for l in range(depth):
        wl = w_in[l]
        lw = dict(
            norm1=norm1[l][None], norm2=norm2[l][None],
            wqkv=wl[:, :qkv_cols].astype(BF16),
            wab=jnp.pad(wl[:, qkv_cols:state_cols], ((0, 0), (0, LANES - gate_cols))).astype(BF16),
            wgate=wl[:, state_cols:state_cols + dn_width].astype(BF16),
            wpool=wl[:, state_cols + dn_width:].astype(BF16),
            conv_w=conv_w[l], a_log=a_log[l], dt_bias=dt_bias[l], dn_norm=dn_norm[l][None],
            pool_w=pool_w[l].astype(BF16), pool_scale=pool_scale[l][None],
            w_out=w_out[l].astype(BF16),
            w_router=jnp.pad(w_router[l], ((0, 0), (0, LANES - N_EXPERTS))).astype(BF16),
            w_gate=w_gate[l], w_up=w_up[l], w_down=w_down[l],
        )
        mods = mod_all[l].reshape(cond_rows, 6, d)
        mod_x = [mods[:batch, i][:, None, :] for i in range(6)]
        mod_z = [jnp.broadcast_to(mods[batch, i][None, None, :], (batch, 1, d)) for i in range(6)]

        o_z, pooled_z, ctx_states = _token_mixer(z, mod_z, lw, (zero_state, zero_state), None, tm_z)
        if l < depth - 1:
            z = _ffn(o_z, pooled_z, z, mod_z, lw, tm_z)
        o_x, pooled_x, _ = _token_mixer(x, mod_x, lw, ctx_states, rows, tm_x)
        x = _ffn(o_x, pooled_x, x, mod_x, lw, tm_x)
    return _final_norm(x, norm_f[None], tm_x)
```

```python
import functools

import jax
import jax.numpy as jnp
from jax import lax
from jax.experimental import pallas as pl
from jax.experimental.pallas import tpu as pltpu

F32 = jnp.float32
BF16 = jnp.bfloat16
HIGHEST = lax.Precision.HIGHEST

LANES = 128
GRID_W = 64
DN_HEADS = 4
N_DIR = 2
CHUNK = 64
DN_GROUP = 8
POOL_WINDOWS = (2, 4, 8, 16)
N_EXPERTS = 16
EC_CAPACITY = 2
EPS = 1e-6
VMEM_LIMIT = 56 * 1024 * 1024

NT_DIMS = (((1,), (1,)), ((), ()))
TN_DIMS = (((0,), (0,)), ((), ()))


def _params(*semantics):
    return pltpu.CompilerParams(dimension_semantics=semantics, vmem_limit_bytes=VMEM_LIMIT)


def _silu(x):
    return x * jax.nn.sigmoid(x)


def _dot(a, b):
    return jnp.dot(a, b, preferred_element_type=F32)


def _indicator(mask, dtype=F32):
    return jnp.where(mask, 1.0, 0.0).astype(dtype)


def _mod_kernel(cond_ref, w_ref, b_ref, o_ref):
    s = _silu(cond_ref[...])
    s_hi = s.astype(BF16)
    s_lo = (s - s_hi.astype(F32)).astype(BF16)
    w = w_ref[0]
    w_hi = w.astype(BF16)
    w_lo = (w - w_hi.astype(F32)).astype(BF16)
    rows = s.shape[0]
    both = _dot(jnp.concatenate([s_hi, s_lo], axis=0), w_hi)
    o_ref[0] = both[:rows] + both[rows:] + _dot(s_hi, w_lo) + b_ref[0]


def _adaln_all(cond, w_mod, b_mod):
    depth, d, n = w_mod.shape
    rows = cond.shape[0]
    tn = 1536
    return pl.pallas_call(
        _mod_kernel,
        grid=(depth, n // tn),
        in_specs=[pl.BlockSpec((rows, d), lambda l, j: (0, 0)),
                  pl.BlockSpec((1, d, tn), lambda l, j: (l, 0, j)),
                  pl.BlockSpec((1, 1, tn), lambda l, j: (l, 0, j))],
        out_specs=pl.BlockSpec((1, rows, tn), lambda l, j: (l, 0, j)),
        out_shape=jax.ShapeDtypeStruct((depth, rows, n), F32),
        compiler_params=_params("arbitrary", "arbitrary"),
        name="adaln",
    )(cond, w_mod, b_mod.reshape(depth, 1, n))


def _norm_mod(x, nw, shift, scale):
    ms = jnp.mean(x * x, axis=-1, keepdims=True)
    return (x * lax.rsqrt(ms + EPS) * nw) * (1.0 + scale) + shift


def _in_kernel(x_ref, sh_ref, sc_ref, nw_ref, wqkv_ref, wab_ref, wgate_ref, wpool_ref,
               qkv_ref, ab_ref, gate_ref, pool_ref):
    h = _norm_mod(x_ref[0], nw_ref[...], sh_ref[0], sc_ref[0]).astype(BF16)
    qkv_ref[0] = _dot(h, wqkv_ref[...])
    ab_ref[0] = _dot(h, wab_ref[...])
    gate_ref[0] = _dot(h, wgate_ref[...])
    pool_ref[0] = _dot(h, wpool_ref[...])


def _in_proj(x, shift, scale, nw, wqkv, wab, wgate, wpool, tm):
    b, l, d = x.shape
    row = lambda i, j: (i, 0, 0)
    tok = lambda i, j: (i, j, 0)
    fixed = lambda i, j: (0, 0)
    widths = (wqkv.shape[1], wab.shape[1], wgate.shape[1], wpool.shape[1])
    return pl.pallas_call(
        _in_kernel,
        grid=(b, l // tm),
        in_specs=[pl.BlockSpec((1, tm, d), tok),
                  pl.BlockSpec((1, 1, d), row), pl.BlockSpec((1, 1, d), row),
                  pl.BlockSpec((1, d), fixed)]
                 + [pl.BlockSpec((d, n), fixed) for n in widths],
        out_specs=[pl.BlockSpec((1, tm, n), tok) for n in widths],
        out_shape=[jax.ShapeDtypeStruct((b, l, n), F32) for n in widths],
        compiler_params=_params("arbitrary", "arbitrary"),
        name="in_proj",
    )(x, shift, scale, nw, wqkv, wab, wgate, wpool)


def _bdot(a, b):
    return jnp.einsum('gij,gjk->gik', a, b, preferred_element_type=F32)


def _dn_kernel(alog_ref, dtb_ref, q_ref, k_ref, v_ref, ab_ref, gate_ref, cwq_ref, cwk_ref, cwv_ref,
               dnw_ref, s0f_ref, s0b_ref, o_ref, sf_ref, sb_ref,
               qs, ks, vs, bet_all, la_all, bet, gl, la, kn_f, kn_b, bn_f, bn_b, qn_f, qn_b, on_f, on_b, *, group):
    head = pl.program_id(1)
    seq = q_ref.shape[1]
    n_chunks = seq // CHUNK
    row = lax.broadcasted_iota(jnp.int32, (seq, LANES), 0)

    def conv_silu(x_ref, cw_ref):
        x = x_ref[0]
        w = cw_ref[...]
        y = (jnp.where(row >= 2, pltpu.roll(x, 2, 0), 0.0) * w[0:1]
             + jnp.where(row >= 1, pltpu.roll(x, 1, 0), 0.0) * w[1:2]
             + x * w[2:3]
             + jnp.where(row < seq - 1, pltpu.roll(x, seq - 1, 0), 0.0) * w[3:4])
        return _silu(y)

    def l2norm(x):
        return x * lax.rsqrt(jnp.sum(x * x, axis=-1, keepdims=True) + EPS)

    qs[...] = l2norm(conv_silu(q_ref, cwq_ref)) * (LANES ** -0.5)
    ks[...] = l2norm(conv_silu(k_ref, cwk_ref))
    vs[...] = conv_silu(v_ref, cwv_ref)

    @pl.when(head == 0)
    def _():
        ab = ab_ref[0]
        bet_all[...] = jax.nn.sigmoid(ab)
        la_all[...] = -jnp.exp(alog_ref[...]) * jax.nn.softplus(ab + dtb_ref[...])

    lane = lax.broadcasted_iota(jnp.int32, (1, LANES), 1)
    shift = jnp.where(head == 0, 0, LANES - head)
    bet[...] = pltpu.roll(bet_all[...], shift, 1)
    log_a = pltpu.roll(la_all[...], shift, 1)
    hi = log_a.astype(BF16).astype(F32)
    mid = (log_a - hi).astype(BF16).astype(F32)
    low = (log_a - hi - mid).astype(BF16).astype(F32)
    part = lane & (DN_HEADS - 1)
    la[...] = jnp.where(part == 0, hi, jnp.where(part == 1, pltpu.roll(mid, 1, 1),
                                                 pltpu.roll(low, 2, 1))).astype(BF16)

    ii = lax.broadcasted_iota(jnp.int32, (CHUNK, CHUNK), 0)
    jj = lax.broadcasted_iota(jnp.int32, (CHUNK, CHUNK), 1)
    sums = jnp.concatenate([_indicator(ii >= jj), _indicator(ii <= jj), jnp.ones((CHUNK, CHUNK), F32)],
                           axis=0).astype(BF16)
    eye = _indicator(ii == jj)
    span = group * CHUNK

    def wy_group(gi, carry):
        rows = pl.ds(pl.multiple_of(gi * span, span), span)
        state_rows = pl.ds(pl.multiple_of(gi * group * LANES, group * LANES), group * LANES)
        chunks = lambda a: a.reshape(group, CHUNK, a.shape[-1])
        q, k, v = chunks(qs[rows, :]), chunks(ks[rows, :]), chunks(vs[rows, :])
        log_a = chunks(la[rows, :])
        parts = jnp.concatenate([_dot(sums, log_a[c]) for c in range(group)], axis=0)
        cums = parts + pltpu.roll(parts, LANES - 1, 1) + pltpu.roll(parts, LANES - 2, 1)
        cums = cums.reshape(group, 3 * CHUNK, LANES)
        gl[rows, :] = cums[:, 2 * CHUNK:].reshape(span, LANES)
        beta_all = chunks(bet[rows, :])
        k16 = k.astype(BF16)
        for direction, (kn, bn, qn, on) in enumerate(((kn_f, bn_f, qn_f, on_f), (kn_b, bn_b, qn_b, on_b))):
            col = DN_HEADS * direction
            g = cums[:, direction * CHUNK:(direction + 1) * CHUNK, col:col + 1]
            g_last = cums[:, 2 * CHUNK:, col:col + 1]
            beta = beta_all[:, :, 2 * DN_HEADS + col:2 * DN_HEADS + col + 1]
            if direction == 0:
                incl, strict = ii >= jj, ii > jj
            else:
                incl, strict = ii <= jj, ii < jj
            g_cols = jnp.swapaxes(jnp.broadcast_to(g, (group, CHUNK, LANES)), 1, 2)[:, :CHUNK, :]
            decay = jnp.exp(jnp.minimum(g - g_cols, 0.0))
            e_g = jnp.exp(g)
            k_beta = k * beta
            both = jnp.einsum('gik,gjk->gij', jnp.concatenate([k_beta, q], axis=1).astype(BF16), k16,
                              preferred_element_type=F32)
            m = jnp.where(strict, both[:, :CHUNK] * decay, 0.0)
            qk = jnp.where(incl, both[:, CHUNK:] * decay, 0.0).astype(BF16)
            p = -m
            t_inv = eye + p
            pb = p.astype(BF16)
            p = _bdot(pb, pb)
            for _ in range(CHUNK.bit_length() - 3):
                pb = p.astype(BF16)
                r = _bdot(jnp.concatenate([pb, t_inv.astype(BF16)], axis=1), pb)
                p, t_inv = r[:, :CHUNK], t_inv + r[:, CHUNK:]
            t_inv = t_inv + _bdot(t_inv.astype(BF16), p.astype(BF16))
            wu = _bdot(t_inv.astype(BF16),
                       jnp.concatenate([k_beta * e_g, v * beta], axis=2).astype(BF16)).astype(BF16)
            k_tail = (k * jnp.exp(g_last - g)).astype(BF16)
            kb = jnp.einsum('gik,gin->gkn', k_tail, wu, preferred_element_type=F32)
            qb = _bdot(qk, wu)
            kn[state_rows, :] = kb[:, :, :LANES].reshape(group * LANES, LANES).astype(BF16)
            bn[state_rows, :] = kb[:, :, LANES:].reshape(group * LANES, LANES)
            qn[rows, :] = (q * e_g - qb[:, :, :LANES]).reshape(span, LANES).astype(BF16)
            on[rows, :] = qb[:, :, LANES:].reshape(span, LANES)
        return carry

    lax.fori_loop(0, n_chunks // group, wy_group, 0)

    def scan_chunk(n, state, direction, kn, bn, qn, on):
        col = DN_HEADS * direction
        rows = pl.ds(pl.multiple_of(n * CHUNK, CHUNK), CHUNK)
        state_rows = pl.ds(pl.multiple_of(n * LANES, LANES), LANES)
        s16 = state.astype(BF16)
        on[rows, :] = on[rows, :] + _dot(qn[rows, :], s16)
        s_decay = jnp.exp(gl[pl.ds(pl.multiple_of(n * CHUNK, CHUNK), 8), :][0:1, col:col + 1])
        return state * s_decay - _dot(kn[state_rows, :], s16) + bn[state_rows, :]

    def pair(i, states):
        s_f, s_b = states
        return (scan_chunk(i, s_f, 0, kn_f, bn_f, qn_f, on_f),
                scan_chunk(n_chunks - 1 - i, s_b, 1, kn_b, bn_b, qn_b, on_b))

    s_f, s_b = lax.fori_loop(0, n_chunks, pair, (s0f_ref[0, 0], s0b_ref[0, 0]))
    sf_ref[0, 0] = s_f
    sb_ref[0, 0] = s_b

    o = on_f[...] + on_b[...]
    y = o * lax.rsqrt(jnp.mean(o * o, axis=-1, keepdims=True) + EPS) * dnw_ref[...]
    o_ref[0] = (y * _silu(gate_ref[0])).astype(BF16)


def _delta_net(qkv, ab, gate, conv_w, a_log, dt_bias, dn_norm, s0f, s0b):
    b, seq, _ = qkv.shape
    hd = LANES
    lane_vec = lambda a: jnp.pad(a.reshape(1, -1), ((0, 0), (0, LANES - a.size)))
    vec = pl.BlockSpec((1, LANES), lambda i, h: (0, 0))
    col = lambda off: pl.BlockSpec((1, seq, hd), lambda i, h: (i, 0, off + h))
    cw = lambda off: pl.BlockSpec((conv_w.shape[0], hd), lambda i, h: (0, off + h))
    st = pl.BlockSpec((1, 1, hd, hd), lambda i, h: (i, h, 0, 0))
    n_chunks = seq // CHUNK
    group = min(DN_GROUP, n_chunks)
    seq_f32 = pltpu.VMEM((seq, hd), F32)
    seq_bf16 = pltpu.VMEM((seq, hd), BF16)
    mats_f32 = pltpu.VMEM((n_chunks * hd, hd), F32)
    mats_bf16 = pltpu.VMEM((n_chunks * hd, hd), BF16)
    return pl.pallas_call(
        functools.partial(_dn_kernel, group=group),
        grid=(b, DN_HEADS),
        in_specs=[vec, vec, col(0), col(DN_HEADS), col(2 * DN_HEADS),
                  pl.BlockSpec((1, seq, LANES), lambda i, h: (i, 0, 0)),
                  col(0), cw(0), cw(DN_HEADS), cw(2 * DN_HEADS),
                  pl.BlockSpec((1, hd), lambda i, h: (0, 0)), st, st],
        out_specs=[col(0), st, st],
        out_shape=[jax.ShapeDtypeStruct((b, seq, DN_HEADS * hd), BF16),
                   jax.ShapeDtypeStruct((b, DN_HEADS, hd, hd), F32),
                   jax.ShapeDtypeStruct((b, DN_HEADS, hd, hd), F32)],
        scratch_shapes=[seq_f32] * 7 + [seq_bf16] + [mats_bf16] * 2 + [mats_f32] * 2 + [seq_bf16] * 2 + [seq_f32] * 2,
        compiler_params=_params("arbitrary", "arbitrary"),
        name="delta_net",
    )(lane_vec(a_log), lane_vec(dt_bias), qkv, qkv, qkv, ab, gate, conv_w, conv_w, conv_w, dn_norm, s0f, s0b)


def _window_sum(x, pos, limit, half, stride, seq):
    left = jnp.where(pos >= 1, pltpu.roll(x, stride, 0), 0.0)
    right = x
    k = 1
    while k < half:
        left = left + jnp.where(pos >= k, pltpu.roll(left, k * stride, 0), 0.0)
        right = right + jnp.where(pos + k < limit, pltpu.roll(right, seq - k * stride, 0), 0.0)
        k *= 2
    return left + right


def _window_count(pos, limit, half):
    return (jnp.minimum(pos + half, limit) - jnp.maximum(pos - half, 0)).astype(F32)


def _pool_kernel(u_ref, pw_ref, ps_ref, o_ref, *, rows):
    seq = u_ref.shape[1]
    t = lax.broadcasted_iota(jnp.int32, (seq, LANES), 0)
    for g, w in enumerate(POOL_WINDOWS):
        lanes = slice(g * LANES, (g + 1) * LANES)
        x = u_ref[0, :, lanes]
        half = w // 2
        if rows is None:
            total = _window_sum(x, t, seq, half, 1, seq)
            count = _window_count(t, seq, half)
        else:
            c, r = t & (GRID_W - 1), t >> (GRID_W.bit_length() - 1)
            total = _window_sum(_window_sum(x, c, GRID_W, half, 1, seq), r, rows, half, GRID_W, seq)
            count = _window_count(r, rows, half) * _window_count(c, GRID_W, half)
        m = (total / count - x).astype(BF16)
        o_ref[0, :, lanes] = (_dot(m, pw_ref[g]) * ps_ref[:, lanes]).astype(BF16)


def _pool_mixer(u, pool_w, pool_scale, rows):
    b, seq, width = u.shape
    blk = pl.BlockSpec((1, seq, width), lambda i: (i, 0, 0))
    return pl.pallas_call(
        functools.partial(_pool_kernel, rows=rows),
        grid=(b,),
        in_specs=[blk, pl.BlockSpec(pool_w.shape, lambda i: (0, 0, 0)), pl.BlockSpec((1, width), lambda i: (0, 0))],
        out_specs=blk,
        out_shape=jax.ShapeDtypeStruct((b, seq, width), BF16),
        compiler_params=_params("arbitrary"),
        name="pool_mixer",
    )(u, pool_w, pool_scale)


def _out_kernel(o_ref, p_ref, x_ref, wo_ref, g1_ref, nw_ref, sh_ref, sc_ref, wr_ref, x1_ref, h2_ref, lg_ref):
    half = o_ref.shape[2]
    y = _dot(o_ref[0], wo_ref[:half, :]) + _dot(p_ref[0], wo_ref[half:, :])
    x1 = x_ref[0] + g1_ref[0] * y
    x1_ref[0] = x1
    h2 = _norm_mod(x1, nw_ref[...], sh_ref[0], sc_ref[0]).astype(BF16)
    h2_ref[0] = h2
    lg_ref[0] = _dot(h2, wr_ref[...])


def _out_proj(o, pooled, x, w_out, g1, nw, shift, scale, w_router, tm):
    b, l, d = x.shape
    half = o.shape[2]
    row = lambda i, j: (i, 0, 0)
    tok = lambda i, j: (i, j, 0)
    fixed = lambda i, j: (0, 0)
    vec = pl.BlockSpec((1, 1, d), row)
    return pl.pallas_call(
        _out_kernel,
        grid=(b, l // tm),
        in_specs=[pl.BlockSpec((1, tm, half), tok), pl.BlockSpec((1, tm, half), tok), pl.BlockSpec((1, tm, d), tok),
                  pl.BlockSpec(w_out.shape, fixed), vec, pl.BlockSpec((1, d), fixed), vec, vec,
                  pl.BlockSpec(w_router.shape, fixed)],
        out_specs=[pl.BlockSpec((1, tm, d), tok), pl.BlockSpec((1, tm, d), tok), pl.BlockSpec((1, tm, LANES), tok)],
        out_shape=[jax.ShapeDtypeStruct((b, l, d), F32), jax.ShapeDtypeStruct((b, l, d), BF16),
                   jax.ShapeDtypeStruct((b, l, LANES), F32)],
        compiler_params=_params("arbitrary", "arbitrary"),
        name="out_proj",
    )(o, pooled, x, w_out, g1, nw, shift, scale, w_router)


TRI_ROWS = 256


def _route_kernel(lg_ref, posc_ref, posr_ref, aff_ref, tri_ref, *, cap):
    seq = lg_ref.shape[1]

    @pl.when(pl.program_id(0) == 0)
    def _():
        def fill(i, carry):
            r0 = pl.multiple_of(i * TRI_ROWS, TRI_ROWS)
            r = lax.broadcasted_iota(jnp.int32, (TRI_ROWS, seq), 0) + r0
            c = lax.broadcasted_iota(jnp.int32, (TRI_ROWS, seq), 1)
            tri_ref[pl.ds(r0, TRI_ROWS), :] = _indicator(r > c, BF16)
            return carry
        lax.fori_loop(0, seq // TRI_ROWS, fill, 0)

    lane = lax.broadcasted_iota(jnp.int32, (1, LANES), 1)
    is_expert = lane < N_EXPERTS
    lg = jnp.where(is_expert, lg_ref[0], -jnp.inf)
    ex = jnp.exp(lg - jnp.max(lg, axis=-1, keepdims=True))
    aff = ex / jnp.sum(ex, axis=-1, keepdims=True)
    aff_ref[0] = aff

    def bit_step(it, lo_bits):
        cand_bits = lo_bits | jnp.left_shift(jnp.int32(1), 30 - it)
        cand = lax.bitcast_convert_type(cand_bits, F32)
        count = jnp.sum(_indicator(aff >= cand), axis=0, keepdims=True)
        return jnp.where(count >= cap, cand_bits, lo_bits)

    lo_bits = lax.fori_loop(0, 31, bit_step, jnp.zeros((1, LANES), jnp.int32))
    lo = lax.bitcast_convert_type(lo_bits, F32)
    hi = lax.bitcast_convert_type(lo_bits + 1, F32)
    above = (aff >= hi) & is_expert
    tied = (aff >= lo) & (aff < hi) & is_expert
    need = cap - jnp.sum(_indicator(above), axis=0, keepdims=True)
    flags = (_indicator(above) + pltpu.roll(_indicator(tied), N_EXPERTS, 1)).astype(BF16)
    before = _dot(tri_ref[...], flags)
    tied_before = pltpu.roll(before, LANES - N_EXPERTS, 1)
    chosen = above | (tied & (tied_before < need))
    slot = before + jnp.minimum(tied_before, need)
    slot = jnp.where(chosen, slot, -1.0)
    posc_ref[0] = slot
    posr_ref[0] = slot.T[:N_EXPERTS, :]


def _route(logits, cap):
    b, seq, _ = logits.shape
    blk = pl.BlockSpec((1, seq, LANES), lambda i: (i, 0, 0))
    return pl.pallas_call(
        functools.partial(_route_kernel, cap=cap),
        grid=(b,),
        in_specs=[blk],
        out_specs=[blk, pl.BlockSpec((1, N_EXPERTS, seq), lambda i: (i, 0, 0)), blk],
        out_shape=[jax.ShapeDtypeStruct((b, seq, LANES), F32), jax.ShapeDtypeStruct((b, N_EXPERTS, seq), F32),
                   jax.ShapeDtypeStruct((b, seq, LANES), F32)],
        scratch_shapes=[pltpu.VMEM((seq, seq), BF16)],
        compiler_params=_params("arbitrary"),
        name="route",
    )(logits)


def _expert_kernel(h_ref, pos_ref, wg_ref, wu_ref, wd_ref, ye_ref, wg_s, wu_s, wd_s, *, cap):
    @pl.when(pl.program_id(1) == 0)
    def _():
        wg_s[...] = wg_ref[0, 0].astype(BF16)
        wu_s[...] = wu_ref[0, 0].astype(BF16)
        wd_s[...] = wd_ref[0, 0].astype(BF16)

    samples, seq, _ = h_ref.shape
    slots = lax.broadcasted_iota(jnp.int32, (cap, seq), 0).astype(F32)
    xe = jnp.concatenate([_dot(_indicator(slots == pos_ref[s, 0], BF16), h_ref[s]) for s in range(samples)],
                         axis=0).astype(BF16)
    hid = (_silu(_dot(xe, wg_s[...])) * _dot(xe, wu_s[...])).astype(BF16)
    ye = _dot(hid, wd_s[...]).astype(BF16)
    for s in range(samples):
        ye_ref[s, 0] = ye[s * cap:(s + 1) * cap]


def _experts(h2, pos_rows, w_gate, w_up, w_down, layer, cap, samples):
    b, seq, d = h2.shape
    _, n_exp, _, ff = w_gate.shape
    wspec = lambda shape: pl.BlockSpec((1, 1) + shape, lambda e, i: (layer, e, 0, 0))
    wbuf = lambda shape: pltpu.VMEM(shape, BF16)
    return pl.pallas_call(
        functools.partial(_expert_kernel, cap=cap),
        grid=(n_exp, b // samples),
        in_specs=[pl.BlockSpec((samples, seq, d), lambda e, i: (i, 0, 0)),
                  pl.BlockSpec((samples, 1, 1, seq), lambda e, i: (i, e, 0, 0)),
                  wspec((d, ff)), wspec((d, ff)), wspec((ff, d))],
        out_specs=pl.BlockSpec((samples, 1, cap, d), lambda e, i: (i, e, 0, 0)),
        out_shape=jax.ShapeDtypeStruct((b, n_exp, cap, d), BF16),
        scratch_shapes=[wbuf((d, ff)), wbuf((d, ff)), wbuf((ff, d))],
        compiler_params=_params("arbitrary", "arbitrary"),
        name="experts",
    )(h2, pos_rows.reshape(b, n_exp, 1, seq), w_gate, w_up, w_down)


def _combine_kernel(pos_ref, aff_ref, ye_ref, x_ref, g2_ref, o_ref, *, cap):
    tm = x_ref.shape[1]
    slots = lax.broadcasted_iota(jnp.int32, (tm, cap), 1).astype(F32)
    pos = pos_ref[0]
    aff = aff_ref[0]
    acc = jnp.zeros(x_ref.shape[1:], F32)
    for e in range(N_EXPERTS):
        place = _indicator(pos[:, e:e + 1] == slots, BF16)
        acc = acc + aff[:, e:e + 1] * _dot(place, ye_ref[0, e])
    o_ref[0] = x_ref[0] + g2_ref[0] * acc


def _combine(pos_cols, aff, ye, x1, g2, cap, tm):
    b, l, d = x1.shape
    tok = lambda i, j: (i, j, 0)
    return pl.pallas_call(
        functools.partial(_combine_kernel, cap=cap),
        grid=(b, l // tm),
        in_specs=[pl.BlockSpec((1, tm, LANES), tok), pl.BlockSpec((1, tm, LANES), tok),
                  pl.BlockSpec((1, N_EXPERTS, cap, d), lambda i, j: (i, 0, 0, 0)),
                  pl.BlockSpec((1, tm, d), tok), pl.BlockSpec((1, 1, d), lambda i, j: (i, 0, 0))],
        out_specs=pl.BlockSpec((1, tm, d), tok),
        out_shape=jax.ShapeDtypeStruct((b, l, d), F32),
        compiler_params=_params("arbitrary", "arbitrary"),
        name="combine",
    )(pos_cols, aff, ye, x1, g2)


def _final_kernel(x_ref, nw_ref, o_ref):
    x = x_ref[0]
    o_ref[0] = x * lax.rsqrt(jnp.mean(x * x, axis=-1, keepdims=True) + EPS) * nw_ref[...]


def _final_norm(x, nw, tm):
    b, l, d = x.shape
    tok = lambda i, j: (i, j, 0)
    return pl.pallas_call(
        _final_kernel,
        grid=(b, l // tm),
        in_specs=[pl.BlockSpec((1, tm, d), tok), pl.BlockSpec((1, d), lambda i, j: (0, 0))],
        out_specs=pl.BlockSpec((1, tm, d), tok),
        out_shape=jax.ShapeDtypeStruct((b, l, d), F32),
        compiler_params=_params("arbitrary", "arbitrary"),
        name="final_norm",
    )(x, nw)


def _token_mixer(x, mod, lw, states, rows, tm):
    sh1, sc1 = mod[0], mod[1]
    qkv, ab, gate, pool = _in_proj(x, sh1, sc1, lw["norm1"], lw["wqkv"], lw["wab"], lw["wgate"], lw["wpool"], tm)
    o, s_f, s_b = _delta_net(qkv, ab, gate, lw["conv_w"], lw["a_log"], lw["dt_bias"], lw["dn_norm"], *states)
    pooled = _pool_mixer(pool, lw["pool_w"], lw["pool_scale"], rows)
    return o, pooled, (s_f, s_b)


def _ffn(o, pooled, x, mod, lw, tm, samples):
    seq = x.shape[1]
    cap = EC_CAPACITY * seq // N_EXPERTS
    x1, h2, logits = _out_proj(o, pooled, x, lw["w_out"], mod[2], lw["norm2"], mod[3], mod[4], lw["w_router"], tm)
    pos_cols, pos_rows, aff = _route(logits, cap)
    ye = _experts(h2, pos_rows, lw["w_gate"], lw["w_up"], lw["w_down"], lw["layer"], cap, samples)
    return _combine(pos_cols, aff, ye, x1, mod[5], cap, tm)


def kernel(x, c, ctx, c_ctx, w_mod, b_mod, norm1, norm2, w_in, conv_w, a_log, dt_bias, dn_norm, pool_w, pool_scale,
           w_out, w_router, w_gate, w_up, w_down, norm_f):
    batch, seq, d = x.shape
    depth = w_mod.shape[0]
    ctx_len = ctx.shape[1]
    dn_width = DN_HEADS * LANES
    qkv_cols = 3 * dn_width
    gate_cols = 2 * N_DIR * DN_HEADS
    state_cols = qkv_cols + gate_cols
    rows = seq // GRID_W
    tm_x, tm_z = 512, ctx_len

    cond_rows = 16
    cond = jnp.zeros((cond_rows, d), F32).at[:batch].set(c).at[batch].set(c_ctx)
    mod_all = _adaln_all(cond, w_mod, b_mod)

    zero_state = jnp.zeros((batch, DN_HEADS, LANES, LANES), F32)
    z = ctx
    for l in range(depth):
        wl = w_in[l]
        lw = dict(
            norm1=norm1[l][None], norm2=norm2[l][None],
            wqkv=wl[:, :qkv_cols].astype(BF16),
            wab=jnp.pad(wl[:, qkv_cols:state_cols], ((0, 0), (0, LANES - gate_cols))).astype(BF16),
            wgate=wl[:, state_cols:state_cols + dn_width].astype(BF16),
            wpool=wl[:, state_cols + dn_width:].astype(BF16),
            conv_w=conv_w[l], a_log=a_log[l], dt_bias=dt_bias[l], dn_norm=dn_norm[l][None],
            pool_w=pool_w[l].astype(BF16), pool_scale=pool_scale[l][None],
            w_out=w_out[l].astype(BF16),
            w_router=jnp.pad(w_router[l], ((0, 0), (0, LANES - N_EXPERTS))).astype(BF16),
            w_gate=w_gate, w_up=w_up, w_down=w_down, layer=l,
        )
        mods = mod_all[l].reshape(cond_rows, 6, d)
        mod_x = [mods[:batch, i][:, None, :] for i in range(6)]
        mod_z = [jnp.broadcast_to(mods[batch, i][None, None, :], (batch, 1, d)) for i in range(6)]

        o_z, pooled_z, ctx_states = _token_mixer(z, mod_z, lw, (zero_state, zero_state), None, tm_z)
        if l < depth - 1:
            z = _ffn(o_z, pooled_z, z, mod_z, lw, tm_z, batch)
        o_x, pooled_x, _ = _token_mixer(x, mod_x, lw, ctx_states, rows, tm_x)
        x = _ffn(o_x, pooled_x, x, mod_x, lw, tm_x, 1)
    return _final_norm(x, norm_f[None], tm_x)
```

```python
import functools

import jax
import jax.numpy as jnp
from jax import lax
from jax.experimental import pallas as pl
from jax.experimental.pallas import tpu as pltpu

F32 = jnp.float32
BF16 = jnp.bfloat16

LANES = 128
GRID_W = 64
DN_HEADS = 4
N_DIR = 2
CHUNK = 64
DN_GROUP = 16
POOL_WINDOWS = (2, 4, 8, 16)
N_EXPERTS = 16
EC_CAPACITY = 2
EPS = 1e-6
VMEM_LIMIT = 56 * 1024 * 1024

def _params(*semantics):
    return pltpu.CompilerParams(dimension_semantics=semantics, vmem_limit_bytes=VMEM_LIMIT)


def _silu(x):
    return x * jax.nn.sigmoid(x)


def _dot(a, b):
    return jnp.dot(a, b, preferred_element_type=F32)


def _indicator(mask, dtype=F32):
    return jnp.where(mask, 1.0, 0.0).astype(dtype)


def _mod_kernel(cond_ref, w_ref, b_ref, o_ref):
    s = _silu(cond_ref[...])
    s_hi = s.astype(BF16)
    s_lo = (s - s_hi.astype(F32)).astype(BF16)
    w = w_ref[0]
    w_hi = w.astype(BF16)
    w_lo = (w - w_hi.astype(F32)).astype(BF16)
    rows = s.shape[0]
    both = _dot(jnp.concatenate([s_hi, s_lo], axis=0), w_hi)
    o_ref[0] = both[:rows] + both[rows:] + _dot(s_hi, w_lo) + b_ref[0]


def _adaln_all(cond, w_mod, b_mod):
    depth, d, n = w_mod.shape
    rows = cond.shape[0]
    tn = 1536
    return pl.pallas_call(
        _mod_kernel,
        grid=(depth, n // tn),
        in_specs=[pl.BlockSpec((rows, d), lambda l, j: (0, 0)),
                  pl.BlockSpec((1, d, tn), lambda l, j: (l, 0, j)),
                  pl.BlockSpec((1, 1, tn), lambda l, j: (l, 0, j))],
        out_specs=pl.BlockSpec((1, rows, tn), lambda l, j: (l, 0, j)),
        out_shape=jax.ShapeDtypeStruct((depth, rows, n), F32),
        compiler_params=_params("arbitrary", "arbitrary"),
        name="adaln",
    )(cond, w_mod, b_mod.reshape(depth, 1, n))


def _norm_mod(x, nw, shift, scale):
    ms = jnp.mean(x * x, axis=-1, keepdims=True)
    return (x * lax.rsqrt(ms + EPS) * nw) * (1.0 + scale) + shift


def _in_kernel(x_ref, sh_ref, sc_ref, nw_ref, wqkv_ref, wab_ref, wgate_ref, wpool_ref,
               qkv_ref, ab_ref, gate_ref, pool_ref):
    h = _norm_mod(x_ref[0], nw_ref[...], sh_ref[0], sc_ref[0]).astype(BF16)
    qkv_ref[0] = _dot(h, wqkv_ref[...])
    ab_ref[0] = _dot(h, wab_ref[...])
    gate_ref[0] = _dot(h, wgate_ref[...])
    pool_ref[0] = _dot(h, wpool_ref[...])


def _in_proj(x, shift, scale, nw, wqkv, wab, wgate, wpool, tm):
    b, l, d = x.shape
    row = lambda i, j: (i, 0, 0)
    tok = lambda i, j: (i, j, 0)
    fixed = lambda i, j: (0, 0)
    widths = (wqkv.shape[1], wab.shape[1], wgate.shape[1], wpool.shape[1])
    return pl.pallas_call(
        _in_kernel,
        grid=(b, l // tm),
        in_specs=[pl.BlockSpec((1, tm, d), tok),
                  pl.BlockSpec((1, 1, d), row), pl.BlockSpec((1, 1, d), row),
                  pl.BlockSpec((1, d), fixed)]
                 + [pl.BlockSpec((d, n), fixed) for n in widths],
        out_specs=[pl.BlockSpec((1, tm, n), tok) for n in widths],
        out_shape=[jax.ShapeDtypeStruct((b, l, n), F32) for n in widths],
        compiler_params=_params("arbitrary", "arbitrary"),
        name="in_proj",
    )(x, shift, scale, nw, wqkv, wab, wgate, wpool)


def _bdot(a, b):
    return jnp.einsum('gij,gjk->gik', a, b, preferred_element_type=F32)


def _dn_kernel(alog_ref, dtb_ref, q_ref, k_ref, v_ref, ab_ref, gate_ref, cwq_ref, cwk_ref, cwv_ref,
               dnw_ref, s0f_ref, s0b_ref, o_ref, sf_ref, sb_ref,
               qs, ks, vs, bet_all, la_all, bet, gl, la, kn_f, kn_b, bn_f, bn_b, qn_f, qn_b, on_f, on_b, *, group):
    head = pl.program_id(1)
    seq = q_ref.shape[1]
    n_chunks = seq // CHUNK
    row = lax.broadcasted_iota(jnp.int32, (seq, LANES), 0)

    def conv_silu(x_ref, cw_ref):
        x = x_ref[0]
        w = cw_ref[...]
        y = (jnp.where(row >= 2, pltpu.roll(x, 2, 0), 0.0) * w[0:1]
             + jnp.where(row >= 1, pltpu.roll(x, 1, 0), 0.0) * w[1:2]
             + x * w[2:3]
             + jnp.where(row < seq - 1, pltpu.roll(x, seq - 1, 0), 0.0) * w[3:4])
        return _silu(y)

    def l2norm(x):
        return x * lax.rsqrt(jnp.sum(x * x, axis=-1, keepdims=True) + EPS)

    qs[...] = l2norm(conv_silu(q_ref, cwq_ref)) * (LANES ** -0.5)
    ks[...] = l2norm(conv_silu(k_ref, cwk_ref))
    vs[...] = conv_silu(v_ref, cwv_ref)

    @pl.when(head == 0)
    def _():
        ab = ab_ref[0]
        bet_all[...] = jax.nn.sigmoid(ab)
        la_all[...] = -jnp.exp(alog_ref[...]) * jax.nn.softplus(ab + dtb_ref[...])

    lane = lax.broadcasted_iota(jnp.int32, (1, LANES), 1)
    shift = jnp.where(head == 0, 0, LANES - head)
    bet[...] = pltpu.roll(bet_all[...], shift, 1)
    log_a = pltpu.roll(la_all[...], shift, 1)
    hi = log_a.astype(BF16).astype(F32)
    mid = (log_a - hi).astype(BF16).astype(F32)
    low = (log_a - hi - mid).astype(BF16).astype(F32)
    part = lane & (DN_HEADS - 1)
    la[...] = jnp.where(part == 0, hi, jnp.where(part == 1, pltpu.roll(mid, 1, 1),
                                                 pltpu.roll(low, 2, 1))).astype(BF16)

    ii = lax.broadcasted_iota(jnp.int32, (CHUNK, CHUNK), 0)
    jj = lax.broadcasted_iota(jnp.int32, (CHUNK, CHUNK), 1)
    sums = jnp.concatenate([_indicator(ii >= jj), _indicator(ii <= jj), jnp.ones((CHUNK, CHUNK), F32)],
                           axis=0).astype(BF16)
    eye = _indicator(ii == jj)
    span = group * CHUNK

    def wy_group(gi, carry):
        rows = pl.ds(pl.multiple_of(gi * span, span), span)
        state_rows = pl.ds(pl.multiple_of(gi * group * LANES, group * LANES), group * LANES)
        chunks = lambda a: a.reshape(group, CHUNK, a.shape[-1])
        q, k, v = chunks(qs[rows, :]), chunks(ks[rows, :]), chunks(vs[rows, :])
        log_a = chunks(la[rows, :])
        parts = jnp.concatenate([_dot(sums, log_a[c]) for c in range(group)], axis=0)
        cums = parts + pltpu.roll(parts, LANES - 1, 1) + pltpu.roll(parts, LANES - 2, 1)
        cums = cums.reshape(group, 3 * CHUNK, LANES)
        gl[rows, :] = cums[:, 2 * CHUNK:].reshape(span, LANES)
        beta_all = chunks(bet[rows, :])
        k16 = k.astype(BF16)
        for direction, (kn, bn, qn, on) in enumerate(((kn_f, bn_f, qn_f, on_f), (kn_b, bn_b, qn_b, on_b))):
            col = DN_HEADS * direction
            g = cums[:, direction * CHUNK:(direction + 1) * CHUNK, col:col + 1]
            g_last = cums[:, 2 * CHUNK:, col:col + 1]
            beta = beta_all[:, :, 2 * DN_HEADS + col:2 * DN_HEADS + col + 1]
            if direction == 0:
                incl, strict = ii >= jj, ii > jj
            else:
                incl, strict = ii <= jj, ii < jj
            g_cols = jnp.swapaxes(jnp.broadcast_to(g, (group, CHUNK, LANES)), 1, 2)[:, :CHUNK, :]
            decay = jnp.exp(jnp.minimum(g - g_cols, 0.0))
            e_g = jnp.exp(g)
            k_beta = k * beta
            both = jnp.einsum('gik,gjk->gij', jnp.concatenate([k_beta, q], axis=1).astype(BF16), k16,
                              preferred_element_type=F32)
            m = jnp.where(strict, both[:, :CHUNK] * decay, 0.0)
            qk = jnp.where(incl, both[:, CHUNK:] * decay, 0.0).astype(BF16)
            p = -m
            t_inv = eye + p
            pb = p.astype(BF16)
            p = _bdot(pb, pb)
            for _ in range(CHUNK.bit_length() - 3):
                pb = p.astype(BF16)
                r = _bdot(jnp.concatenate([pb, t_inv.astype(BF16)], axis=1), pb)
                p, t_inv = r[:, :CHUNK], t_inv + r[:, CHUNK:]
            t_inv = t_inv + _bdot(t_inv.astype(BF16), p.astype(BF16))
            wu = _bdot(t_inv.astype(BF16),
                       jnp.concatenate([k_beta * e_g, v * beta], axis=2).astype(BF16)).astype(BF16)
            k_tail = (k * jnp.exp(g_last - g)).astype(BF16)
            kb = jnp.einsum('gik,gin->gkn', k_tail, wu, preferred_element_type=F32)
            qb = _bdot(qk, wu)
            kn[state_rows, :] = kb[:, :, :LANES].reshape(group * LANES, LANES).astype(BF16)
            bn[state_rows, :] = kb[:, :, LANES:].reshape(group * LANES, LANES)
            qn[rows, :] = (q * e_g - qb[:, :, :LANES]).reshape(span, LANES).astype(BF16)
            on[rows, :] = qb[:, :, LANES:].reshape(span, LANES)
        return carry

    lax.fori_loop(0, n_chunks // group, wy_group, 0)

    def scan_chunk(n, state, direction, kn, bn, qn, on):
        col = DN_HEADS * direction
        rows = pl.ds(pl.multiple_of(n * CHUNK, CHUNK), CHUNK)
        state_rows = pl.ds(pl.multiple_of(n * LANES, LANES), LANES)
        s16 = state.astype(BF16)
        on[rows, :] = on[rows, :] + _dot(qn[rows, :], s16)
        s_decay = jnp.exp(gl[pl.ds(pl.multiple_of(n * CHUNK, CHUNK), 8), :][0:1, col:col + 1])
        return state * s_decay - _dot(kn[state_rows, :], s16) + bn[state_rows, :]

    def pair(i, states):
        s_f, s_b = states
        return (scan_chunk(i, s_f, 0, kn_f, bn_f, qn_f, on_f),
                scan_chunk(n_chunks - 1 - i, s_b, 1, kn_b, bn_b, qn_b, on_b))

    s_f, s_b = lax.fori_loop(0, n_chunks, pair, (s0f_ref[0, 0], s0b_ref[0, 0]))
    sf_ref[0, 0] = s_f
    sb_ref[0, 0] = s_b

    o = on_f[...] + on_b[...]
    y = o * lax.rsqrt(jnp.mean(o * o, axis=-1, keepdims=True) + EPS) * dnw_ref[...]
    o_ref[0] = (y * _silu(gate_ref[0])).astype(BF16)


def _delta_net(qkv, ab, gate, conv_w, a_log, dt_bias, dn_norm, s0f, s0b):
    b, seq, _ = qkv.shape
    hd = LANES
    lane_vec = lambda a: jnp.pad(a.reshape(1, -1), ((0, 0), (0, LANES - a.size)))
    vec = pl.BlockSpec((1, LANES), lambda i, h: (0, 0))
    col = lambda off: pl.BlockSpec((1, seq, hd), lambda i, h: (i, 0, off + h))
    cw = lambda off: pl.BlockSpec((conv_w.shape[0], hd), lambda i, h: (0, off + h))
    st = pl.BlockSpec((1, 1, hd, hd), lambda i, h: (i, h, 0, 0))
    n_chunks = seq // CHUNK
    group = min(DN_GROUP, n_chunks)
    seq_f32 = pltpu.VMEM((seq, hd), F32)
    seq_bf16 = pltpu.VMEM((seq, hd), BF16)
    mats_f32 = pltpu.VMEM((n_chunks * hd, hd), F32)
    mats_bf16 = pltpu.VMEM((n_chunks * hd, hd), BF16)
    return pl.pallas_call(
        functools.partial(_dn_kernel, group=group),
        grid=(b, DN_HEADS),
        in_specs=[vec, vec, col(0), col(DN_HEADS), col(2 * DN_HEADS),
                  pl.BlockSpec((1, seq, LANES), lambda i, h: (i, 0, 0)),
                  col(0), cw(0), cw(DN_HEADS), cw(2 * DN_HEADS),
                  pl.BlockSpec((1, hd), lambda i, h: (0, 0)), st, st],
        out_specs=[col(0), st, st],
        out_shape=[jax.ShapeDtypeStruct((b, seq, DN_HEADS * hd), BF16),
                   jax.ShapeDtypeStruct((b, DN_HEADS, hd, hd), F32),
                   jax.ShapeDtypeStruct((b, DN_HEADS, hd, hd), F32)],
        scratch_shapes=[seq_f32] * 7 + [seq_bf16] + [mats_bf16] * 2 + [mats_f32] * 2 + [seq_bf16] * 2 + [seq_f32] * 2,
        compiler_params=_params("arbitrary", "arbitrary"),
        name="delta_net",
    )(lane_vec(a_log), lane_vec(dt_bias), qkv, qkv, qkv, ab, gate, conv_w, conv_w, conv_w, dn_norm, s0f, s0b)


def _window_sum(x, pos, limit, half, stride, seq):
    left = jnp.where(pos >= 1, pltpu.roll(x, stride, 0), 0.0)
    right = x
    k = 1
    while k < half:
        left = left + jnp.where(pos >= k, pltpu.roll(left, k * stride, 0), 0.0)
        right = right + jnp.where(pos + k < limit, pltpu.roll(right, seq - k * stride, 0), 0.0)
        k *= 2
    return left + right


def _window_count(pos, limit, half):
    return (jnp.minimum(pos + half, limit) - jnp.maximum(pos - half, 0)).astype(F32)


def _pool_kernel(u_ref, pw_ref, ps_ref, o_ref, *, rows):
    seq = u_ref.shape[1]
    t = lax.broadcasted_iota(jnp.int32, (seq, LANES), 0)
    for g, w in enumerate(POOL_WINDOWS):
        lanes = slice(g * LANES, (g + 1) * LANES)
        x = u_ref[0, :, lanes]
        half = w // 2
        if rows is None:
            total = _window_sum(x, t, seq, half, 1, seq)
            count = _window_count(t, seq, half)
        else:
            c, r = t & (GRID_W - 1), t >> (GRID_W.bit_length() - 1)
            total = _window_sum(_window_sum(x, c, GRID_W, half, 1, seq), r, rows, half, GRID_W, seq)
            count = _window_count(r, rows, half) * _window_count(c, GRID_W, half)
        m = (total / count - x).astype(BF16)
        o_ref[0, :, lanes] = (_dot(m, pw_ref[g]) * ps_ref[:, lanes]).astype(BF16)


def _pool_mixer(u, pool_w, pool_scale, rows):
    b, seq, width = u.shape
    blk = pl.BlockSpec((1, seq, width), lambda i: (i, 0, 0))
    return pl.pallas_call(
        functools.partial(_pool_kernel, rows=rows),
        grid=(b,),
        in_specs=[blk, pl.BlockSpec(pool_w.shape, lambda i: (0, 0, 0)), pl.BlockSpec((1, width), lambda i: (0, 0))],
        out_specs=blk,
        out_shape=jax.ShapeDtypeStruct((b, seq, width), BF16),
        compiler_params=_params("arbitrary"),
        name="pool_mixer",
    )(u, pool_w, pool_scale)


def _out_kernel(o_ref, p_ref, x_ref, wo_ref, g1_ref, nw_ref, sh_ref, sc_ref, wr_ref, x1_ref, h2_ref, aff_ref):
    sample = pl.program_id(1)
    half = o_ref.shape[2]
    y = _dot(o_ref[0], wo_ref[:half, :]) + _dot(p_ref[0], wo_ref[half:, :])
    x1 = x_ref[0] + g1_ref[0] * y
    x1_ref[0] = x1
    h2 = _norm_mod(x1, nw_ref[...], sh_ref[0], sc_ref[0]).astype(BF16)
    h2_ref[0] = h2
    lane = lax.broadcasted_iota(jnp.int32, (1, LANES), 1)
    logits = jnp.where(lane < N_EXPERTS, _dot(h2, wr_ref[...]), -jnp.inf)
    ex = jnp.exp(logits - jnp.max(logits, axis=-1, keepdims=True))
    aff = pltpu.roll(ex / jnp.sum(ex, axis=-1, keepdims=True), sample * N_EXPERTS, 1)

    @pl.when(sample == 0)
    def _():
        aff_ref[...] = aff

    @pl.when(sample > 0)
    def _():
        aff_ref[...] += aff


def _out_proj(o, pooled, x, w_out, g1, nw, shift, scale, w_router, tm):
    b, l, d = x.shape
    assert b * N_EXPERTS <= LANES
    half = o.shape[2]
    row = lambda j, i: (i, 0, 0)
    tok = lambda j, i: (i, j, 0)
    fixed = lambda j, i: (0, 0)
    vec = pl.BlockSpec((1, 1, d), row)
    return pl.pallas_call(
        _out_kernel,
        grid=(l // tm, b),
        in_specs=[pl.BlockSpec((1, tm, half), tok), pl.BlockSpec((1, tm, half), tok), pl.BlockSpec((1, tm, d), tok),
                  pl.BlockSpec(w_out.shape, fixed), vec, pl.BlockSpec((1, d), fixed), vec, vec,
                  pl.BlockSpec(w_router.shape, fixed)],
        out_specs=[pl.BlockSpec((1, tm, d), tok), pl.BlockSpec((1, tm, d), tok),
                   pl.BlockSpec((tm, LANES), lambda j, i: (j, 0))],
        out_shape=[jax.ShapeDtypeStruct((b, l, d), F32), jax.ShapeDtypeStruct((b, l, d), BF16),
                   jax.ShapeDtypeStruct((l, LANES), F32)],
        compiler_params=_params("arbitrary", "arbitrary"),
        name="out_proj",
    )(o, pooled, x, w_out, g1, nw, shift, scale, w_router)


TOKEN_BLOCK = 256
MXU_DEPTH = 256
BF16_ROWS = 16


def _route_kernel(aff_ref, slotc_ref, slotr_ref, affr_ref, starts_ref, *, cap):
    seq = aff_ref.shape[0]
    aff = aff_ref[...]

    def bit_step(it, lo_bits):
        cand_bits = lo_bits | jnp.left_shift(jnp.int32(1), 30 - it)
        cand = lax.bitcast_convert_type(cand_bits, F32)
        count = jnp.sum(_indicator(aff >= cand), axis=0, keepdims=True)
        return jnp.where(count >= cap, cand_bits, lo_bits)

    lo_bits = lax.fori_loop(0, 31, bit_step, jnp.zeros((1, LANES), jnp.int32))
    lo = lax.bitcast_convert_type(lo_bits, F32)
    hi = lax.bitcast_convert_type(lo_bits + 1, F32)
    above = aff >= hi
    tied = (aff >= lo) & (aff < hi)
    need = cap - jnp.sum(_indicator(above), axis=0, keepdims=True)
    flags = jnp.concatenate([_indicator(above), _indicator(tied)], axis=1)

    r = lax.broadcasted_iota(jnp.int32, (TOKEN_BLOCK, TOKEN_BLOCK), 0)
    c = lax.broadcasted_iota(jnp.int32, (TOKEN_BLOCK, TOKEN_BLOCK), 1)
    tri = _indicator(r > c, BF16)
    offset = jnp.zeros((1, 2 * LANES), F32)
    pieces = []
    for j in range(seq // TOKEN_BLOCK):
        blk = flags[j * TOKEN_BLOCK:(j + 1) * TOKEN_BLOCK]
        pieces.append(_dot(tri, blk.astype(BF16)) + offset)
        offset = offset + jnp.sum(blk, axis=0, keepdims=True)
    before = jnp.concatenate(pieces, axis=0)
    tied_before = before[:, LANES:]
    rank = before[:, :LANES] + jnp.minimum(tied_before, need)
    chosen = above | (tied & (tied_before < need))
    slot = jnp.where(chosen, rank, -1.0)
    slotc_ref[...] = slot
    slotr_ref[...] = slot.T
    affr_ref[...] = aff.T
    n_blocks = seq // TOKEN_BLOCK
    for j in range(n_blocks):
        starts_ref[j:j + 1, :] = rank[j * TOKEN_BLOCK:j * TOKEN_BLOCK + 1, :]
    starts_ref[n_blocks:, :] = jnp.full((starts_ref.shape[0] - n_blocks, LANES), cap, F32)


def _route(aff, cap):
    seq = aff.shape[0]
    start_rows = -(-(seq // TOKEN_BLOCK + 1) // 8) * 8
    return pl.pallas_call(
        functools.partial(_route_kernel, cap=cap),
        out_shape=[jax.ShapeDtypeStruct((seq, LANES), F32), jax.ShapeDtypeStruct((LANES, seq), F32),
                   jax.ShapeDtypeStruct((LANES, seq), F32), jax.ShapeDtypeStruct((start_rows, LANES), F32)],
        compiler_params=pltpu.CompilerParams(vmem_limit_bytes=VMEM_LIMIT),
        name="route",
    )(aff)


def _expert_kernel(h_ref, pos_ref, aff_ref, wg_ref, wu_ref, wd_ref, ye_ref, wg_s, wu_s, wd_s, *, cap):
    @pl.when(pl.program_id(1) == 0)
    def _():
        wg_s[...] = wg_ref[0, 0].astype(BF16)
        wu_s[...] = wu_ref[0, 0].astype(BF16)
        wd_s[...] = wd_ref[0, 0].astype(BF16)

    samples, seq, _ = h_ref.shape
    slots = lax.broadcasted_iota(jnp.int32, (cap, seq), 0).astype(F32)
    gathered, weights = [], []
    for s in range(samples):
        hit = slots == pos_ref[s, 0]
        gathered.append(_dot(_indicator(hit, BF16), h_ref[s]))
        weights.append(jnp.sum(jnp.where(hit, aff_ref[s, 0], 0.0), axis=1, keepdims=True))
    xe = jnp.concatenate(gathered, axis=0).astype(BF16)
    hid = (_silu(_dot(xe, wg_s[...])) * _dot(xe, wu_s[...])).astype(BF16)
    ye = (_dot(hid, wd_s[...]) * jnp.concatenate(weights, axis=0)).astype(BF16)
    for s in range(samples):
        ye_ref[s, 0] = ye[s * cap:(s + 1) * cap]


def _experts(h2, slot_rows, aff_rows, w_gate, w_up, w_down, layer, cap, samples):
    b, seq, d = h2.shape
    _, n_exp, _, ff = w_gate.shape
    lanes = b * n_exp
    wspec = lambda shape: pl.BlockSpec((1, 1) + shape, lambda e, i: (layer, e, 0, 0))
    wbuf = lambda shape: pltpu.VMEM(shape, BF16)
    lane_row = pl.BlockSpec((samples, 1, 1, seq), lambda e, i: (i, e, 0, 0))
    return pl.pallas_call(
        functools.partial(_expert_kernel, cap=cap),
        grid=(n_exp, b // samples),
        in_specs=[pl.BlockSpec((samples, seq, d), lambda e, i: (i, 0, 0)), lane_row, lane_row,
                  wspec((d, ff)), wspec((d, ff)), wspec((ff, d))],
        out_specs=pl.BlockSpec((samples, 1, cap, d), lambda e, i: (i, e, 0, 0)),
        out_shape=jax.ShapeDtypeStruct((b, n_exp, cap, d), BF16),
        scratch_shapes=[wbuf((d, ff)), wbuf((d, ff)), wbuf((ff, d))],
        compiler_params=_params("arbitrary", "arbitrary"),
        name="experts",
    )(h2, slot_rows[:lanes].reshape(b, n_exp, 1, seq), aff_rows[:lanes].reshape(b, n_exp, 1, seq),
      w_gate, w_up, w_down)


def _combine_kernel(starts_ref, slot_ref, ye_ref, x_ref, g2_ref, o_ref, *, cap, window):
    sample, tile = pl.program_id(0), pl.program_id(1)
    tm = x_ref.shape[1]
    per_pass = MXU_DEPTH // window
    base = tile * LANES + sample * N_EXPERTS
    first, fits = [], None
    for e in range(N_EXPERTS):
        lo, hi = starts_ref[base + e], starts_ref[base + LANES + e]
        start = jnp.minimum(lo // BF16_ROWS * BF16_ROWS, cap - window)
        first.append(start)
        ok = hi - start <= window
        fits = ok if fits is None else fits & ok
    pos = pltpu.roll(slot_ref[...], jnp.where(sample == 0, 0, LANES - sample * N_EXPERTS), 1)

    @pl.when(fits)
    def _():
        lane = lax.broadcasted_iota(jnp.int32, (1, MXU_DEPTH), 1)
        rel = (lane % window).astype(F32)
        acc = jnp.zeros(x_ref.shape[1:], F32)
        for g in range(N_EXPERTS // per_pass):
            experts = range(g * per_pass, (g + 1) * per_pass)
            val = jnp.broadcast_to(pos[:, experts[0]:experts[0] + 1] - first[experts[0]].astype(F32), (tm, MXU_DEPTH))
            for k, e in enumerate(experts[1:], start=1):
                val = jnp.where(lane >= k * window, pos[:, e:e + 1] - first[e].astype(F32), val)
            rows = jnp.concatenate([ye_ref[0, e, pl.ds(pl.multiple_of(first[e], BF16_ROWS), window), :]
                                    for e in experts], axis=0)
            acc = acc + _dot(_indicator(val == rel, BF16), rows)
        o_ref[0] = x_ref[0] + g2_ref[0] * acc

    @pl.when(jnp.logical_not(fits))
    def _():
        slots = lax.broadcasted_iota(jnp.int32, (tm, cap), 1).astype(F32)
        acc = jnp.zeros(x_ref.shape[1:], F32)
        for e in range(N_EXPERTS):
            acc = acc + _dot(_indicator(pos[:, e:e + 1] == slots, BF16), ye_ref[0, e])
        o_ref[0] = x_ref[0] + g2_ref[0] * acc


def _combine(starts, slot_cols, ye, x1, g2, cap):
    b, l, d = x1.shape
    tm = TOKEN_BLOCK
    window = min(64, cap)
    tok = lambda i, j: (i, j, 0)
    return pl.pallas_call(
        functools.partial(_combine_kernel, cap=cap, window=window),
        grid=(b, l // tm),
        in_specs=[pl.BlockSpec(memory_space=pltpu.SMEM),
                  pl.BlockSpec((tm, LANES), lambda i, j: (j, 0)),
                  pl.BlockSpec((1, N_EXPERTS, cap, d), lambda i, j: (i, 0, 0, 0)),
                  pl.BlockSpec((1, tm, d), tok), pl.BlockSpec((1, 1, d), lambda i, j: (i, 0, 0))],
        out_specs=pl.BlockSpec((1, tm, d), tok),
        out_shape=jax.ShapeDtypeStruct((b, l, d), F32),
        compiler_params=_params("arbitrary", "arbitrary"),
        name="combine",
    )(starts, slot_cols, ye, x1, g2)


def _final_kernel(x_ref, nw_ref, o_ref):
    x = x_ref[0]
    o_ref[0] = x * lax.rsqrt(jnp.mean(x * x, axis=-1, keepdims=True) + EPS) * nw_ref[...]


def _final_norm(x, nw, tm):
    b, l, d = x.shape
    tok = lambda i, j: (i, j, 0)
    return pl.pallas_call(
        _final_kernel,
        grid=(b, l // tm),
        in_specs=[pl.BlockSpec((1, tm, d), tok), pl.BlockSpec((1, d), lambda i, j: (0, 0))],
        out_specs=pl.BlockSpec((1, tm, d), tok),
        out_shape=jax.ShapeDtypeStruct((b, l, d), F32),
        compiler_params=_params("arbitrary", "arbitrary"),
        name="final_norm",
    )(x, nw)


def _token_mixer(x, mod, lw, states, rows, tm):
    sh1, sc1 = mod[0], mod[1]
    qkv, ab, gate, pool = _in_proj(x, sh1, sc1, lw["norm1"], lw["wqkv"], lw["wab"], lw["wgate"], lw["wpool"], tm)
    o, s_f, s_b = _delta_net(qkv, ab, gate, lw["conv_w"], lw["a_log"], lw["dt_bias"], lw["dn_norm"], *states)
    pooled = _pool_mixer(pool, lw["pool_w"], lw["pool_scale"], rows)
    return o, pooled, (s_f, s_b)


def _ffn(o, pooled, x, mod, lw, tm, samples):
    seq = x.shape[1]
    cap = EC_CAPACITY * seq // N_EXPERTS
    x1, h2, aff = _out_proj(o, pooled, x, lw["w_out"], mod[2], lw["norm2"], mod[3], mod[4], lw["w_router"], tm)
    slot_cols, slot_rows, aff_rows, starts = _route(aff, cap)
    ye = _experts(h2, slot_rows, aff_rows, lw["w_gate"], lw["w_up"], lw["w_down"], lw["layer"], cap, samples)
    return _combine(starts.astype(jnp.int32).reshape(-1), slot_cols, ye, x1, mod[5], cap)


def kernel(x, c, ctx, c_ctx, w_mod, b_mod, norm1, norm2, w_in, conv_w, a_log, dt_bias, dn_norm, pool_w, pool_scale,
           w_out, w_router, w_gate, w_up, w_down, norm_f):
    batch, seq, d = x.shape
    depth = w_mod.shape[0]
    ctx_len = ctx.shape[1]
    dn_width = DN_HEADS * LANES
    qkv_cols = 3 * dn_width
    gate_cols = 2 * N_DIR * DN_HEADS
    state_cols = qkv_cols + gate_cols
    rows = seq // GRID_W
    tm_x, tm_z = 512, ctx_len

    cond_rows = 16
    cond = jnp.zeros((cond_rows, d), F32).at[:batch].set(c).at[batch].set(c_ctx)
    mod_all = _adaln_all(cond, w_mod, b_mod)

    zero_state = jnp.zeros((batch, DN_HEADS, LANES, LANES), F32)
    z = ctx
    for l in range(depth):
        wl = w_in[l]
        lw = dict(
            norm1=norm1[l][None], norm2=norm2[l][None],
            wqkv=wl[:, :qkv_cols].astype(BF16),
            wab=jnp.pad(wl[:, qkv_cols:state_cols], ((0, 0), (0, LANES - gate_cols))).astype(BF16),
            wgate=wl[:, state_cols:state_cols + dn_width].astype(BF16),
            wpool=wl[:, state_cols + dn_width:].astype(BF16),
            conv_w=conv_w[l], a_log=a_log[l], dt_bias=dt_bias[l], dn_norm=dn_norm[l][None],
            pool_w=pool_w[l].astype(BF16), pool_scale=pool_scale[l][None],
            w_out=w_out[l].astype(BF16),
            w_router=jnp.pad(w_router[l], ((0, 0), (0, LANES - N_EXPERTS))).astype(BF16),
            w_gate=w_gate, w_up=w_up, w_down=w_down, layer=l,
        )
        mods = mod_all[l].reshape(cond_rows, 6, d)
        mod_x = [mods[:batch, i][:, None, :] for i in range(6)]
        mod_z = [jnp.broadcast_to(mods[batch, i][None, None, :], (batch, 1, d)) for i in range(6)]

        o_z, pooled_z, ctx_states = _token_mixer(z, mod_z, lw, (zero_state, zero_state), None, tm_z)
        if l < depth - 1:
            z = _ffn(o_z, pooled_z, z, mod_z, lw, tm_z, batch)
        o_x, pooled_x, _ = _token_mixer(x, mod_x, lw, ctx_states, rows, tm_x)
        x = _ffn(o_x, pooled_x, x, mod_x, lw, tm_x, 1)
    return _final_norm(x, norm_f[None], tm_x)
```

```python
import functools

import jax
import jax.numpy as jnp
from jax import lax
from jax.experimental import pallas as pl
from jax.experimental.pallas import tpu as pltpu

F32 = jnp.float32
BF16 = jnp.bfloat16

LANES = 128
GRID_W = 64
DN_HEADS = 4
N_DIR = 2
CHUNK = 64
DN_GROUP = 16
POOL_WINDOWS = (2, 4, 8, 16)
N_EXPERTS = 16
EC_CAPACITY = 2
EPS = 1e-6
VMEM_LIMIT = 56 * 1024 * 1024

def _params(*semantics):
    return pltpu.CompilerParams(dimension_semantics=semantics, vmem_limit_bytes=VMEM_LIMIT)


def _silu(x):
    return x * jax.nn.sigmoid(x)


def _dot(a, b):
    return jnp.dot(a, b, preferred_element_type=F32)


def _indicator(mask, dtype=F32):
    return jnp.where(mask, 1.0, 0.0).astype(dtype)


def _mod_kernel(cond_ref, w_ref, b_ref, o_ref):
    s = _silu(cond_ref[...])
    s_hi = s.astype(BF16)
    s_lo = (s - s_hi.astype(F32)).astype(BF16)
    w = w_ref[0]
    w_hi = w.astype(BF16)
    w_lo = (w - w_hi.astype(F32)).astype(BF16)
    rows = s.shape[0]
    both = _dot(jnp.concatenate([s_hi, s_lo], axis=0), w_hi)
    o_ref[0] = both[:rows] + both[rows:] + _dot(s_hi, w_lo) + b_ref[0]


def _adaln_all(cond, w_mod, b_mod):
    depth, d, n = w_mod.shape
    rows = cond.shape[0]
    tn = 1536
    return pl.pallas_call(
        _mod_kernel,
        grid=(depth, n // tn),
        in_specs=[pl.BlockSpec((rows, d), lambda l, j: (0, 0)),
                  pl.BlockSpec((1, d, tn), lambda l, j: (l, 0, j)),
                  pl.BlockSpec((1, 1, tn), lambda l, j: (l, 0, j))],
        out_specs=pl.BlockSpec((1, rows, tn), lambda l, j: (l, 0, j)),
        out_shape=jax.ShapeDtypeStruct((depth, rows, n), F32),
        compiler_params=_params("arbitrary", "arbitrary"),
        name="adaln",
    )(cond, w_mod, b_mod.reshape(depth, 1, n))


def _norm_mod(x, nw, shift, scale):
    ms = jnp.mean(x * x, axis=-1, keepdims=True)
    return (x * lax.rsqrt(ms + EPS) * nw) * (1.0 + scale) + shift


def _in_kernel(x_ref, sh_ref, sc_ref, nw_ref, wqkv_ref, wab_ref, wgate_ref, wpool_ref,
               qkv_ref, ab_ref, gate_ref, pool_ref):
    h = _norm_mod(x_ref[0], nw_ref[...], sh_ref[0], sc_ref[0]).astype(BF16)
    qkv = _dot(h, wqkv_ref[...])
    for c in range(qkv_ref.shape[1]):
        qkv_ref[0, c] = qkv[:, c * LANES:(c + 1) * LANES]
    gate = _dot(h, wgate_ref[...])
    for c in range(gate_ref.shape[1]):
        gate_ref[0, c] = gate[:, c * LANES:(c + 1) * LANES]
    ab_ref[0] = _dot(h, wab_ref[...])
    pool_ref[0] = _dot(h, wpool_ref[...])


def _in_proj(x, shift, scale, nw, wqkv, wab, wgate, wpool, tm):
    b, l, d = x.shape
    row = lambda i, j: (i, 0, 0)
    tok = lambda i, j: (i, j, 0)
    fixed = lambda i, j: (0, 0)
    heads = lambda w: w.shape[1] // LANES
    per_head = lambda w: pl.BlockSpec((1, heads(w), tm, LANES), lambda i, j: (i, 0, j, 0))
    return pl.pallas_call(
        _in_kernel,
        grid=(b, l // tm),
        in_specs=[pl.BlockSpec((1, tm, d), tok),
                  pl.BlockSpec((1, 1, d), row), pl.BlockSpec((1, 1, d), row),
                  pl.BlockSpec((1, d), fixed)]
                 + [pl.BlockSpec(w.shape, fixed) for w in (wqkv, wab, wgate, wpool)],
        out_specs=[per_head(wqkv), pl.BlockSpec((1, tm, wab.shape[1]), tok), per_head(wgate),
                   pl.BlockSpec((1, tm, wpool.shape[1]), tok)],
        out_shape=[jax.ShapeDtypeStruct((b, heads(wqkv), l, LANES), F32),
                   jax.ShapeDtypeStruct((b, l, wab.shape[1]), F32),
                   jax.ShapeDtypeStruct((b, heads(wgate), l, LANES), F32),
                   jax.ShapeDtypeStruct((b, l, wpool.shape[1]), F32)],
        compiler_params=_params("arbitrary", "arbitrary"),
        name="in_proj",
    )(x, shift, scale, nw, wqkv, wab, wgate, wpool)


def _bdot(a, b):
    return jnp.einsum('gij,gjk->gik', a, b, preferred_element_type=F32)


def _dn_kernel(alog_ref, dtb_ref, q_ref, k_ref, v_ref, ab_ref, gate_ref, cwq_ref, cwk_ref, cwv_ref,
               dnw_ref, s0f_ref, s0b_ref, o_ref, sf_ref, sb_ref,
               qs, ks, vs, bet_all, la_all, bet, gl, la, kn_f, kn_b, bn_f, bn_b, qn_f, qn_b, on_f, on_b, *, group):
    step = pl.program_id(1)
    _, heads, seq, _ = q_ref.shape
    n_chunks = seq // CHUNK
    row = lax.broadcasted_iota(jnp.int32, (seq, LANES), 0)
    lane = lax.broadcasted_iota(jnp.int32, (1, LANES), 1)

    def conv_silu(x, w):
        y = (jnp.where(row >= 2, pltpu.roll(x, 2, 0), 0.0) * w[0:1]
             + jnp.where(row >= 1, pltpu.roll(x, 1, 0), 0.0) * w[1:2]
             + x * w[2:3]
             + jnp.where(row < seq - 1, pltpu.roll(x, seq - 1, 0), 0.0) * w[3:4])
        return _silu(y)

    def l2norm(x):
        return x * lax.rsqrt(jnp.sum(x * x, axis=-1, keepdims=True) + EPS)

    @pl.when(step == 0)
    def _():
        ab = ab_ref[0]
        bet_all[...] = jax.nn.sigmoid(ab)
        la_all[...] = -jnp.exp(alog_ref[...]) * jax.nn.softplus(ab + dtb_ref[...])

    for h in range(heads):
        own = pl.ds(h * seq, seq)
        lanes = slice(h * LANES, (h + 1) * LANES)
        qs[own, :] = l2norm(conv_silu(q_ref[0, h], cwq_ref[:, lanes])) * (LANES ** -0.5)
        ks[own, :] = l2norm(conv_silu(k_ref[0, h], cwk_ref[:, lanes]))
        vs[own, :] = conv_silu(v_ref[0, h], cwv_ref[:, lanes])
        head = step * heads + h
        shift = jnp.where(head == 0, 0, LANES - head)
        bet[own, :] = pltpu.roll(bet_all[...], shift, 1)
        log_a = pltpu.roll(la_all[...], shift, 1)
        hi = log_a.astype(BF16).astype(F32)
        mid = (log_a - hi).astype(BF16).astype(F32)
        low = (log_a - hi - mid).astype(BF16).astype(F32)
        part = lane & (DN_HEADS - 1)
        la[own, :] = jnp.where(part == 0, hi, jnp.where(part == 1, pltpu.roll(mid, 1, 1),
                                                        pltpu.roll(low, 2, 1))).astype(BF16)

    ii = lax.broadcasted_iota(jnp.int32, (CHUNK, CHUNK), 0)
    jj = lax.broadcasted_iota(jnp.int32, (CHUNK, CHUNK), 1)
    sums = jnp.concatenate([_indicator(ii >= jj), _indicator(ii <= jj), jnp.ones((CHUNK, CHUNK), F32)],
                           axis=0).astype(BF16)
    eye = _indicator(ii == jj)
    span = group * CHUNK

    def wy_group(gi, carry):
        rows = pl.ds(pl.multiple_of(gi * span, span), span)
        state_rows = pl.ds(pl.multiple_of(gi * group * LANES, group * LANES), group * LANES)
        chunks = lambda a: a.reshape(group, CHUNK, a.shape[-1])
        q, k, v = chunks(qs[rows, :]), chunks(ks[rows, :]), chunks(vs[rows, :])
        log_a = chunks(la[rows, :])
        parts = jnp.concatenate([_dot(sums, log_a[c]) for c in range(group)], axis=0)
        cums = parts + pltpu.roll(parts, LANES - 1, 1) + pltpu.roll(parts, LANES - 2, 1)
        cums = cums.reshape(group, 3 * CHUNK, LANES)
        gl[rows, :] = cums[:, 2 * CHUNK:].reshape(span, LANES)
        beta_all = chunks(bet[rows, :])
        k16 = k.astype(BF16)
        for direction, (kn, bn, qn, on) in enumerate(((kn_f, bn_f, qn_f, on_f), (kn_b, bn_b, qn_b, on_b))):
            col = DN_HEADS * direction
            g = cums[:, direction * CHUNK:(direction + 1) * CHUNK, col:col + 1]
            g_last = cums[:, 2 * CHUNK:, col:col + 1]
            beta = beta_all[:, :, 2 * DN_HEADS + col:2 * DN_HEADS + col + 1]
            if direction == 0:
                incl, strict = ii >= jj, ii > jj
            else:
                incl, strict = ii <= jj, ii < jj
            g_cols = jnp.swapaxes(jnp.broadcast_to(g, (group, CHUNK, LANES)), 1, 2)[:, :CHUNK, :]
            decay = jnp.exp(jnp.minimum(g - g_cols, 0.0))
            e_g = jnp.exp(g)
            k_beta = k * beta
            both = jnp.einsum('gik,gjk->gij', jnp.concatenate([k_beta, q], axis=1).astype(BF16), k16,
                              preferred_element_type=F32)
            m = jnp.where(strict, both[:, :CHUNK] * decay, 0.0)
            qk = jnp.where(incl, both[:, CHUNK:] * decay, 0.0).astype(BF16)
            p = -m
            t_inv = eye + p
            pb = p.astype(BF16)
            p = _bdot(pb, pb)
            for _ in range(CHUNK.bit_length() - 3):
                pb = p.astype(BF16)
                r = _bdot(jnp.concatenate([pb, t_inv.astype(BF16)], axis=1), pb)
                p, t_inv = r[:, :CHUNK], t_inv + r[:, CHUNK:]
            t_inv = t_inv + _bdot(t_inv.astype(BF16), p.astype(BF16))
            wu = _bdot(t_inv.astype(BF16),
                       jnp.concatenate([k_beta * e_g, v * beta], axis=2).astype(BF16)).astype(BF16)
            k_tail = (k * jnp.exp(g_last - g)).astype(BF16)
            kb = jnp.einsum('gik,gin->gkn', k_tail, wu, preferred_element_type=F32)
            qb = _bdot(qk, wu)
            kn[state_rows, :] = kb[:, :, :LANES].reshape(group * LANES, LANES).astype(BF16)
            bn[state_rows, :] = kb[:, :, LANES:].reshape(group * LANES, LANES)
            qn[rows, :] = (q * e_g - qb[:, :, :LANES]).reshape(span, LANES).astype(BF16)
            on[rows, :] = qb[:, :, LANES:].reshape(span, LANES)
        return carry

    lax.fori_loop(0, heads * n_chunks // group, wy_group, 0)

    def scan_chunk(n, state, direction, kn, bn, qn, on):
        col = DN_HEADS * direction
        rows = pl.ds(pl.multiple_of(n * CHUNK, CHUNK), CHUNK)
        state_rows = pl.ds(pl.multiple_of(n * LANES, LANES), LANES)
        s16 = state.astype(BF16)
        on[rows, :] = on[rows, :] + _dot(qn[rows, :], s16)
        s_decay = jnp.exp(gl[pl.ds(pl.multiple_of(n * CHUNK, CHUNK), 8), :][0:1, col:col + 1])
        return state * s_decay - _dot(kn[state_rows, :], s16) + bn[state_rows, :]

    def scan_step(i, states):
        out = []
        for h in range(heads):
            out.append(scan_chunk(h * n_chunks + i, states[2 * h], 0, kn_f, bn_f, qn_f, on_f))
            out.append(scan_chunk(h * n_chunks + n_chunks - 1 - i, states[2 * h + 1], 1, kn_b, bn_b, qn_b, on_b))
        return tuple(out)

    init = tuple(ref[0, h] for h in range(heads) for ref in (s0f_ref, s0b_ref))
    final = lax.fori_loop(0, n_chunks, scan_step, init)
    for h in range(heads):
        sf_ref[0, h] = final[2 * h]
        sb_ref[0, h] = final[2 * h + 1]
        own = pl.ds(h * seq, seq)
        o = on_f[own, :] + on_b[own, :]
        y = o * lax.rsqrt(jnp.mean(o * o, axis=-1, keepdims=True) + EPS) * dnw_ref[...]
        o_ref[0, h] = (y * _silu(gate_ref[0, h])).astype(BF16)


def _delta_net(qkv, ab, gate, conv_w, a_log, dt_bias, dn_norm, s0f, s0b, heads):
    b, _, seq, hd = qkv.shape
    lane_vec = lambda a: jnp.pad(a.reshape(1, -1), ((0, 0), (0, LANES - a.size)))
    vec = pl.BlockSpec((1, LANES), lambda i, s: (0, 0))
    part = DN_HEADS // heads
    per_head = lambda off: pl.BlockSpec((1, heads, seq, hd), lambda i, s: (i, off * part + s, 0, 0))
    cw = lambda off: pl.BlockSpec((conv_w.shape[0], heads * hd), lambda i, s: (0, off * part + s))
    st = pl.BlockSpec((1, heads, hd, hd), lambda i, s: (i, s, 0, 0))
    rows = heads * seq
    group = min(DN_GROUP, rows // CHUNK)
    shared_f32 = pltpu.VMEM((seq, hd), F32)
    rows_f32 = pltpu.VMEM((rows, hd), F32)
    rows_bf16 = pltpu.VMEM((rows, hd), BF16)
    mats_f32 = pltpu.VMEM((rows // CHUNK * hd, hd), F32)
    mats_bf16 = pltpu.VMEM((rows // CHUNK * hd, hd), BF16)
    return pl.pallas_call(
        functools.partial(_dn_kernel, group=group),
        grid=(b, part),
        in_specs=[vec, vec, per_head(0), per_head(1), per_head(2),
                  pl.BlockSpec((1, seq, LANES), lambda i, s: (i, 0, 0)),
                  per_head(0), cw(0), cw(1), cw(2),
                  pl.BlockSpec((1, hd), lambda i, s: (0, 0)), st, st],
        out_specs=[per_head(0), st, st],
        out_shape=[jax.ShapeDtypeStruct((b, DN_HEADS, seq, hd), BF16),
                   jax.ShapeDtypeStruct((b, DN_HEADS, hd, hd), F32),
                   jax.ShapeDtypeStruct((b, DN_HEADS, hd, hd), F32)],
        scratch_shapes=[rows_f32] * 3 + [shared_f32] * 2 + [rows_f32] * 2 + [rows_bf16]
                       + [mats_bf16] * 2 + [mats_f32] * 2 + [rows_bf16] * 2 + [rows_f32] * 2,
        compiler_params=_params("arbitrary", "arbitrary"),
        name="delta_net",
    )(lane_vec(a_log), lane_vec(dt_bias), qkv, qkv, qkv, ab, gate, conv_w, conv_w, conv_w, dn_norm, s0f, s0b)


def _window_sum(x, pos, limit, half, stride, seq):
    left = jnp.where(pos >= 1, pltpu.roll(x, stride, 0), 0.0)
    right = x
    k = 1
    while k < half:
        left = left + jnp.where(pos >= k, pltpu.roll(left, k * stride, 0), 0.0)
        right = right + jnp.where(pos + k < limit, pltpu.roll(right, seq - k * stride, 0), 0.0)
        k *= 2
    return left + right


def _window_count(pos, limit, half):
    return (jnp.minimum(pos + half, limit) - jnp.maximum(pos - half, 0)).astype(F32)


def _pool_kernel(u_ref, pw_ref, ps_ref, o_ref, *, rows):
    seq = u_ref.shape[1]
    t = lax.broadcasted_iota(jnp.int32, (seq, LANES), 0)
    for g, w in enumerate(POOL_WINDOWS):
        lanes = slice(g * LANES, (g + 1) * LANES)
        x = u_ref[0, :, lanes]
        half = w // 2
        if rows is None:
            total = _window_sum(x, t, seq, half, 1, seq)
            count = _window_count(t, seq, half)
        else:
            c, r = t & (GRID_W - 1), t >> (GRID_W.bit_length() - 1)
            total = _window_sum(_window_sum(x, c, GRID_W, half, 1, seq), r, rows, half, GRID_W, seq)
            count = _window_count(r, rows, half) * _window_count(c, GRID_W, half)
        m = (total / count - x).astype(BF16)
        o_ref[0, :, lanes] = (_dot(m, pw_ref[g]) * ps_ref[:, lanes]).astype(BF16)


def _pool_mixer(u, pool_w, pool_scale, rows):
    b, seq, width = u.shape
    blk = pl.BlockSpec((1, seq, width), lambda i: (i, 0, 0))
    return pl.pallas_call(
        functools.partial(_pool_kernel, rows=rows),
        grid=(b,),
        in_specs=[blk, pl.BlockSpec(pool_w.shape, lambda i: (0, 0, 0)), pl.BlockSpec((1, width), lambda i: (0, 0))],
        out_specs=blk,
        out_shape=jax.ShapeDtypeStruct((b, seq, width), BF16),
        compiler_params=_params("arbitrary"),
        name="pool_mixer",
    )(u, pool_w, pool_scale)


def _out_kernel(o_ref, p_ref, x_ref, wo_ref, g1_ref, nw_ref, sh_ref, sc_ref, wr_ref, x1_ref, h2_ref, aff_ref):
    sample = pl.program_id(1)
    o = jnp.concatenate([o_ref[0, h] for h in range(o_ref.shape[1])], axis=1)
    half = o.shape[1]
    y = _dot(o, wo_ref[:half, :]) + _dot(p_ref[0], wo_ref[half:, :])
    x1 = x_ref[0] + g1_ref[0] * y
    x1_ref[0] = x1
    h2 = _norm_mod(x1, nw_ref[...], sh_ref[0], sc_ref[0]).astype(BF16)
    h2_ref[0] = h2
    lane = lax.broadcasted_iota(jnp.int32, (1, LANES), 1)
    logits = jnp.where(lane < N_EXPERTS, _dot(h2, wr_ref[...]), -jnp.inf)
    ex = jnp.exp(logits - jnp.max(logits, axis=-1, keepdims=True))
    aff = pltpu.roll(ex / jnp.sum(ex, axis=-1, keepdims=True), sample * N_EXPERTS, 1)

    @pl.when(sample == 0)
    def _():
        aff_ref[...] = aff

    @pl.when(sample > 0)
    def _():
        aff_ref[...] += aff


def _out_proj(o, pooled, x, w_out, g1, nw, shift, scale, w_router, tm):
    b, l, d = x.shape
    assert b * N_EXPERTS <= LANES
    half = pooled.shape[2]
    row = lambda j, i: (i, 0, 0)
    tok = lambda j, i: (i, j, 0)
    fixed = lambda j, i: (0, 0)
    vec = pl.BlockSpec((1, 1, d), row)
    return pl.pallas_call(
        _out_kernel,
        grid=(l // tm, b),
        in_specs=[pl.BlockSpec((1, o.shape[1], tm, LANES), lambda j, i: (i, 0, j, 0)),
                  pl.BlockSpec((1, tm, half), tok), pl.BlockSpec((1, tm, d), tok),
                  pl.BlockSpec(w_out.shape, fixed), vec, pl.BlockSpec((1, d), fixed), vec, vec,
                  pl.BlockSpec(w_router.shape, fixed)],
        out_specs=[pl.BlockSpec((1, tm, d), tok), pl.BlockSpec((1, tm, d), tok),
                   pl.BlockSpec((tm, LANES), lambda j, i: (j, 0))],
        out_shape=[jax.ShapeDtypeStruct((b, l, d), F32), jax.ShapeDtypeStruct((b, l, d), BF16),
                   jax.ShapeDtypeStruct((l, LANES), F32)],
        compiler_params=_params("arbitrary", "arbitrary"),
        name="out_proj",
    )(o, pooled, x, w_out, g1, nw, shift, scale, w_router)


TOKEN_BLOCK = 256
MXU_DEPTH = 256
BF16_ROWS = 16


def _route_kernel(aff_ref, slotc_ref, slotr_ref, affr_ref, starts_ref, *, cap):
    seq = aff_ref.shape[0]
    aff = aff_ref[...]

    def bit_step(it, lo_bits):
        cand_bits = lo_bits | jnp.left_shift(jnp.int32(1), 30 - it)
        cand = lax.bitcast_convert_type(cand_bits, F32)
        count = jnp.sum(_indicator(aff >= cand), axis=0, keepdims=True)
        return jnp.where(count >= cap, cand_bits, lo_bits)

    lo_bits = lax.fori_loop(0, 31, bit_step, jnp.zeros((1, LANES), jnp.int32))
    lo = lax.bitcast_convert_type(lo_bits, F32)
    hi = lax.bitcast_convert_type(lo_bits + 1, F32)
    above = aff >= hi
    tied = (aff >= lo) & (aff < hi)
    need = cap - jnp.sum(_indicator(above), axis=0, keepdims=True)
    flags = jnp.concatenate([_indicator(above), _indicator(tied)], axis=1)

    r = lax.broadcasted_iota(jnp.int32, (TOKEN_BLOCK, TOKEN_BLOCK), 0)
    c = lax.broadcasted_iota(jnp.int32, (TOKEN_BLOCK, TOKEN_BLOCK), 1)
    tri = _indicator(r > c, BF16)
    offset = jnp.zeros((1, 2 * LANES), F32)
    pieces = []
    for j in range(seq // TOKEN_BLOCK):
        blk = flags[j * TOKEN_BLOCK:(j + 1) * TOKEN_BLOCK]
        pieces.append(_dot(tri, blk.astype(BF16)) + offset)
        offset = offset + jnp.sum(blk, axis=0, keepdims=True)
    before = jnp.concatenate(pieces, axis=0)
    tied_before = before[:, LANES:]
    rank = before[:, :LANES] + jnp.minimum(tied_before, need)
    chosen = above | (tied & (tied_before < need))
    slot = jnp.where(chosen, rank, -1.0)
    slotc_ref[...] = slot
    slotr_ref[...] = slot.T
    affr_ref[...] = aff.T
    n_blocks = seq // TOKEN_BLOCK
    for j in range(n_blocks):
        starts_ref[j:j + 1, :] = rank[j * TOKEN_BLOCK:j * TOKEN_BLOCK + 1, :]
    starts_ref[n_blocks:, :] = jnp.full((starts_ref.shape[0] - n_blocks, LANES), cap, F32)


def _route(aff, cap):
    seq = aff.shape[0]
    start_rows = -(-(seq // TOKEN_BLOCK + 1) // 8) * 8
    return pl.pallas_call(
        functools.partial(_route_kernel, cap=cap),
        out_shape=[jax.ShapeDtypeStruct((seq, LANES), F32), jax.ShapeDtypeStruct((LANES, seq), F32),
                   jax.ShapeDtypeStruct((LANES, seq), F32), jax.ShapeDtypeStruct((start_rows, LANES), F32)],
        compiler_params=pltpu.CompilerParams(vmem_limit_bytes=VMEM_LIMIT),
        name="route",
    )(aff)


def _expert_kernel(h_ref, pos_ref, aff_ref, wg_ref, wu_ref, wd_ref, ye_ref, wg_s, wu_s, wd_s, *, cap):
    @pl.when(pl.program_id(1) == 0)
    def _():
        wg_s[...] = wg_ref[0, 0].astype(BF16)
        wu_s[...] = wu_ref[0, 0].astype(BF16)
        wd_s[...] = wd_ref[0, 0].astype(BF16)

    samples, seq, _ = h_ref.shape
    slots = lax.broadcasted_iota(jnp.int32, (cap, seq), 0).astype(F32)
    gathered, weights = [], []
    for s in range(samples):
        hit = slots == pos_ref[s, 0]
        gathered.append(_dot(_indicator(hit, BF16), h_ref[s]))
        weights.append(jnp.sum(jnp.where(hit, aff_ref[s, 0], 0.0), axis=1, keepdims=True))
    xe = jnp.concatenate(gathered, axis=0).astype(BF16)
    hid = (_silu(_dot(xe, wg_s[...])) * _dot(xe, wu_s[...])).astype(BF16)
    ye = (_dot(hid, wd_s[...]) * jnp.concatenate(weights, axis=0)).astype(BF16)
    for s in range(samples):
        ye_ref[s, 0] = ye[s * cap:(s + 1) * cap]


def _experts(h2, slot_rows, aff_rows, w_gate, w_up, w_down, layer, cap, samples):
    b, seq, d = h2.shape
    _, n_exp, _, ff = w_gate.shape
    lanes = b * n_exp
    wspec = lambda shape: pl.BlockSpec((1, 1) + shape, lambda e, i: (layer, e, 0, 0))
    wbuf = lambda shape: pltpu.VMEM(shape, BF16)
    lane_row = pl.BlockSpec((samples, 1, 1, seq), lambda e, i: (i, e, 0, 0))
    return pl.pallas_call(
        functools.partial(_expert_kernel, cap=cap),
        grid=(n_exp, b // samples),
        in_specs=[pl.BlockSpec((samples, seq, d), lambda e, i: (i, 0, 0)), lane_row, lane_row,
                  wspec((d, ff)), wspec((d, ff)), wspec((ff, d))],
        out_specs=pl.BlockSpec((samples, 1, cap, d), lambda e, i: (i, e, 0, 0)),
        out_shape=jax.ShapeDtypeStruct((b, n_exp, cap, d), BF16),
        scratch_shapes=[wbuf((d, ff)), wbuf((d, ff)), wbuf((ff, d))],
        compiler_params=_params("arbitrary", "arbitrary"),
        name="experts",
    )(h2, slot_rows[:lanes].reshape(b, n_exp, 1, seq), aff_rows[:lanes].reshape(b, n_exp, 1, seq),
      w_gate, w_up, w_down)


def _combine_kernel(starts_ref, slot_ref, ye_ref, x_ref, g2_ref, o_ref, *, cap, window):
    sample, tile = pl.program_id(0), pl.program_id(1)
    tm = x_ref.shape[1]
    per_pass = MXU_DEPTH // window
    base = tile * LANES + sample * N_EXPERTS
    first, fits = [], None
    for e in range(N_EXPERTS):
        lo, hi = starts_ref[base + e], starts_ref[base + LANES + e]
        start = jnp.minimum(lo // BF16_ROWS * BF16_ROWS, cap - window)
        first.append(start)
        ok = hi - start <= window
        fits = ok if fits is None else fits & ok
    pos = pltpu.roll(slot_ref[...], jnp.where(sample == 0, 0, LANES - sample * N_EXPERTS), 1)

    @pl.when(fits)
    def _():
        lane = lax.broadcasted_iota(jnp.int32, (1, MXU_DEPTH), 1)
        rel = (lane % window).astype(F32)
        acc = jnp.zeros(x_ref.shape[1:], F32)
        for g in range(N_EXPERTS // per_pass):
            experts = range(g * per_pass, (g + 1) * per_pass)
            val = jnp.broadcast_to(pos[:, experts[0]:experts[0] + 1] - first[experts[0]].astype(F32), (tm, MXU_DEPTH))
            for k, e in enumerate(experts[1:], start=1):
                val = jnp.where(lane >= k * window, pos[:, e:e + 1] - first[e].astype(F32), val)
            rows = jnp.concatenate([ye_ref[0, e, pl.ds(pl.multiple_of(first[e], BF16_ROWS), window), :]
                                    for e in experts], axis=0)
            acc = acc + _dot(_indicator(val == rel, BF16), rows)
        o_ref[0] = x_ref[0] + g2_ref[0] * acc

    @pl.when(jnp.logical_not(fits))
    def _():
        slots = lax.broadcasted_iota(jnp.int32, (tm, cap), 1).astype(F32)
        acc = jnp.zeros(x_ref.shape[1:], F32)
        for e in range(N_EXPERTS):
            acc = acc + _dot(_indicator(pos[:, e:e + 1] == slots, BF16), ye_ref[0, e])
        o_ref[0] = x_ref[0] + g2_ref[0] * acc


def _combine(starts, slot_cols, ye, x1, g2, cap):
    b, l, d = x1.shape
    tm = TOKEN_BLOCK
    window = min(64, cap)
    tok = lambda i, j: (i, j, 0)
    return pl.pallas_call(
        functools.partial(_combine_kernel, cap=cap, window=window),
        grid=(b, l // tm),
        in_specs=[pl.BlockSpec(memory_space=pltpu.SMEM),
                  pl.BlockSpec((tm, LANES), lambda i, j: (j, 0)),
                  pl.BlockSpec((1, N_EXPERTS, cap, d), lambda i, j: (i, 0, 0, 0)),
                  pl.BlockSpec((1, tm, d), tok), pl.BlockSpec((1, 1, d), lambda i, j: (i, 0, 0))],
        out_specs=pl.BlockSpec((1, tm, d), tok),
        out_shape=jax.ShapeDtypeStruct((b, l, d), F32),
        compiler_params=_params("arbitrary", "arbitrary"),
        name="combine",
    )(starts, slot_cols, ye, x1, g2)


def _final_kernel(x_ref, nw_ref, o_ref):
    x = x_ref[0]
    o_ref[0] = x * lax.rsqrt(jnp.mean(x * x, axis=-1, keepdims=True) + EPS) * nw_ref[...]


def _final_norm(x, nw, tm):
    b, l, d = x.shape
    tok = lambda i, j: (i, j, 0)
    return pl.pallas_call(
        _final_kernel,
        grid=(b, l // tm),
        in_specs=[pl.BlockSpec((1, tm, d), tok), pl.BlockSpec((1, d), lambda i, j: (0, 0))],
        out_specs=pl.BlockSpec((1, tm, d), tok),
        out_shape=jax.ShapeDtypeStruct((b, l, d), F32),
        compiler_params=_params("arbitrary", "arbitrary"),
        name="final_norm",
    )(x, nw)


def _token_mixer(x, mod, lw, states, rows, tm, heads):
    sh1, sc1 = mod[0], mod[1]
    qkv, ab, gate, pool = _in_proj(x, sh1, sc1, lw["norm1"], lw["wqkv"], lw["wab"], lw["wgate"], lw["wpool"], tm)
    o, s_f, s_b = _delta_net(qkv, ab, gate, lw["conv_w"], lw["a_log"], lw["dt_bias"], lw["dn_norm"], *states, heads)
    pooled = _pool_mixer(pool, lw["pool_w"], lw["pool_scale"], rows)
    return o, pooled, (s_f, s_b)


def _ffn(o, pooled, x, mod, lw, tm, samples):
    seq = x.shape[1]
    cap = EC_CAPACITY * seq // N_EXPERTS
    x1, h2, aff = _out_proj(o, pooled, x, lw["w_out"], mod[2], lw["norm2"], mod[3], mod[4], lw["w_router"], tm)
    slot_cols, slot_rows, aff_rows, starts = _route(aff, cap)
    ye = _experts(h2, slot_rows, aff_rows, lw["w_gate"], lw["w_up"], lw["w_down"], lw["layer"], cap, samples)
    return _combine(starts.astype(jnp.int32).reshape(-1), slot_cols, ye, x1, mod[5], cap)


def kernel(x, c, ctx, c_ctx, w_mod, b_mod, norm1, norm2, w_in, conv_w, a_log, dt_bias, dn_norm, pool_w, pool_scale,
           w_out, w_router, w_gate, w_up, w_down, norm_f):
    batch, seq, d = x.shape
    depth = w_mod.shape[0]
    ctx_len = ctx.shape[1]
    dn_width = DN_HEADS * LANES
    qkv_cols = 3 * dn_width
    gate_cols = 2 * N_DIR * DN_HEADS
    state_cols = qkv_cols + gate_cols
    rows = seq // GRID_W
    tm_x, tm_z = 512, ctx_len

    cond_rows = 16
    cond = jnp.zeros((cond_rows, d), F32).at[:batch].set(c).at[batch].set(c_ctx)
    mod_all = _adaln_all(cond, w_mod, b_mod)

    zero_state = jnp.zeros((batch, DN_HEADS, LANES, LANES), F32)
    z = ctx
    for l in range(depth):
        wl = w_in[l]
        lw = dict(
            norm1=norm1[l][None], norm2=norm2[l][None],
            wqkv=wl[:, :qkv_cols].astype(BF16),
            wab=jnp.pad(wl[:, qkv_cols:state_cols], ((0, 0), (0, LANES - gate_cols))).astype(BF16),
            wgate=wl[:, state_cols:state_cols + dn_width].astype(BF16),
            wpool=wl[:, state_cols + dn_width:].astype(BF16),
            conv_w=conv_w[l], a_log=a_log[l], dt_bias=dt_bias[l], dn_norm=dn_norm[l][None],
            pool_w=pool_w[l].astype(BF16), pool_scale=pool_scale[l][None],
            w_out=w_out[l].astype(BF16),
            w_router=jnp.pad(w_router[l], ((0, 0), (0, LANES - N_EXPERTS))).astype(BF16),
            w_gate=w_gate, w_up=w_up, w_down=w_down, layer=l,
        )
        mods = mod_all[l].reshape(cond_rows, 6, d)
        mod_x = [mods[:batch, i][:, None, :] for i in range(6)]
        mod_z = [jnp.broadcast_to(mods[batch, i][None, None, :], (batch, 1, d)) for i in range(6)]

        o_z, pooled_z, ctx_states = _token_mixer(z, mod_z, lw, (zero_state, zero_state), None, tm_z, DN_HEADS)
        if l < depth - 1:
            z = _ffn(o_z, pooled_z, z, mod_z, lw, tm_z, batch)
        o_x, pooled_x, _ = _token_mixer(x, mod_x, lw, ctx_states, rows, tm_x, 1)
        x = _ffn(o_x, pooled_x, x, mod_x, lw, tm_x, 1)
    return _final_norm(x, norm_f[None], tm_x)
```

```python
import functools

import jax
import jax.numpy as jnp
from jax import lax
from jax.experimental import pallas as pl
from jax.experimental.pallas import tpu as pltpu

F32 = jnp.float32
BF16 = jnp.bfloat16

LANES = 128
GRID_W = 64
DN_HEADS = 4
N_DIR = 2
CHUNK = 64
DN_GROUP = 16
POOL_WINDOWS = (2, 4, 8, 16)
N_EXPERTS = 16
EC_CAPACITY = 2
EPS = 1e-6
VMEM_LIMIT = 56 * 1024 * 1024

def _params(*semantics):
    return pltpu.CompilerParams(dimension_semantics=semantics, vmem_limit_bytes=VMEM_LIMIT)


def _silu(x):
    return x * jax.nn.sigmoid(x)


def _dot(a, b):
    return jnp.dot(a, b, preferred_element_type=F32)


def _indicator(mask, dtype=F32):
    return jnp.where(mask, 1.0, 0.0).astype(dtype)


def _mod_kernel(cond_ref, w_ref, b_ref, o_ref):
    s = _silu(cond_ref[...])
    s_hi = s.astype(BF16)
    s_lo = (s - s_hi.astype(F32)).astype(BF16)
    w = w_ref[0]
    w_hi = w.astype(BF16)
    w_lo = (w - w_hi.astype(F32)).astype(BF16)
    rows = s.shape[0]
    both = _dot(jnp.concatenate([s_hi, s_lo], axis=0), w_hi)
    o_ref[0] = both[:rows] + both[rows:] + _dot(s_hi, w_lo) + b_ref[0]


def _adaln_all(cond, w_mod, b_mod):
    depth, d, n = w_mod.shape
    rows = cond.shape[0]
    tn = 1536
    return pl.pallas_call(
        _mod_kernel,
        grid=(depth, n // tn),
        in_specs=[pl.BlockSpec((rows, d), lambda l, j: (0, 0)),
                  pl.BlockSpec((1, d, tn), lambda l, j: (l, 0, j)),
                  pl.BlockSpec((1, 1, tn), lambda l, j: (l, 0, j))],
        out_specs=pl.BlockSpec((1, rows, tn), lambda l, j: (l, 0, j)),
        out_shape=jax.ShapeDtypeStruct((depth, rows, n), F32),
        compiler_params=_params("arbitrary", "arbitrary"),
        name="adaln",
    )(cond, w_mod, b_mod.reshape(depth, 1, n))


def _norm_mod(x, nw, shift, scale):
    ms = jnp.mean(x * x, axis=-1, keepdims=True)
    return (x * lax.rsqrt(ms + EPS) * nw) * (1.0 + scale) + shift


def _in_kernel(x_ref, sh_ref, sc_ref, nw_ref, wqkv_ref, wab_ref, wgate_ref, wpool_ref,
               qkv_ref, ab_ref, gate_ref, pool_ref):
    h = _norm_mod(x_ref[0], nw_ref[...], sh_ref[0], sc_ref[0]).astype(BF16)
    qkv = _dot(h, wqkv_ref[...])
    for c in range(qkv_ref.shape[1]):
        qkv_ref[0, c] = qkv[:, c * LANES:(c + 1) * LANES]
    gate = _dot(h, wgate_ref[...])
    for c in range(gate_ref.shape[1]):
        gate_ref[0, c] = gate[:, c * LANES:(c + 1) * LANES]
    ab_ref[0] = _dot(h, wab_ref[...])
    pool_ref[0] = _dot(h, wpool_ref[...])


def _in_proj(x, shift, scale, nw, wqkv, wab, wgate, wpool, tm):
    b, l, d = x.shape
    row = lambda i, j: (i, 0, 0)
    tok = lambda i, j: (i, j, 0)
    fixed = lambda i, j: (0, 0)
    heads = lambda w: w.shape[1] // LANES
    per_head = lambda w: pl.BlockSpec((1, heads(w), tm, LANES), lambda i, j: (i, 0, j, 0))
    return pl.pallas_call(
        _in_kernel,
        grid=(b, l // tm),
        in_specs=[pl.BlockSpec((1, tm, d), tok),
                  pl.BlockSpec((1, 1, d), row), pl.BlockSpec((1, 1, d), row),
                  pl.BlockSpec((1, d), fixed)]
                 + [pl.BlockSpec(w.shape, fixed) for w in (wqkv, wab, wgate, wpool)],
        out_specs=[per_head(wqkv), pl.BlockSpec((1, tm, wab.shape[1]), tok), per_head(wgate),
                   pl.BlockSpec((1, tm, wpool.shape[1]), tok)],
        out_shape=[jax.ShapeDtypeStruct((b, heads(wqkv), l, LANES), F32),
                   jax.ShapeDtypeStruct((b, l, wab.shape[1]), F32),
                   jax.ShapeDtypeStruct((b, heads(wgate), l, LANES), F32),
                   jax.ShapeDtypeStruct((b, l, wpool.shape[1]), F32)],
        compiler_params=_params("arbitrary", "arbitrary"),
        name="in_proj",
    )(x, shift, scale, nw, wqkv, wab, wgate, wpool)


def _bdot(a, b):
    return jnp.einsum('gij,gjk->gik', a, b, preferred_element_type=F32)


def _dn_kernel(alog_ref, dtb_ref, q_ref, k_ref, v_ref, ab_ref, gate_ref, cwq_ref, cwk_ref, cwv_ref,
               dnw_ref, s0f_ref, s0b_ref, o_ref, sf_ref, sb_ref,
               qs, ks, vs, bet_all, la_all, bet, gl, la, kn_f, kn_b, bn_f, bn_b, qn_f, qn_b, on_f, on_b, *, group):
    step = pl.program_id(1)
    _, heads, seq, _ = q_ref.shape
    n_chunks = seq // CHUNK
    row = lax.broadcasted_iota(jnp.int32, (seq, LANES), 0)
    lane = lax.broadcasted_iota(jnp.int32, (1, LANES), 1)

    def conv_silu(x, w):
        y = (jnp.where(row >= 2, pltpu.roll(x, 2, 0), 0.0) * w[0:1]
             + jnp.where(row >= 1, pltpu.roll(x, 1, 0), 0.0) * w[1:2]
             + x * w[2:3]
             + jnp.where(row < seq - 1, pltpu.roll(x, seq - 1, 0), 0.0) * w[3:4])
        return _silu(y)

    def l2norm(x):
        return x * lax.rsqrt(jnp.sum(x * x, axis=-1, keepdims=True) + EPS)

    @pl.when(step == 0)
    def _():
        ab = ab_ref[0]
        bet_all[...] = jax.nn.sigmoid(ab)
        la_all[...] = -jnp.exp(alog_ref[...]) * jax.nn.softplus(ab + dtb_ref[...])

    for h in range(heads):
        own = pl.ds(h * seq, seq)
        lanes = slice(h * LANES, (h + 1) * LANES)
        qs[own, :] = l2norm(conv_silu(q_ref[0, h], cwq_ref[:, lanes])) * (LANES ** -0.5)
        ks[own, :] = l2norm(conv_silu(k_ref[0, h], cwk_ref[:, lanes]))
        vs[own, :] = conv_silu(v_ref[0, h], cwv_ref[:, lanes])
        head = step * heads + h
        shift = jnp.where(head == 0, 0, LANES - head)
        bet[own, :] = pltpu.roll(bet_all[...], shift, 1)
        log_a = pltpu.roll(la_all[...], shift, 1)
        hi = log_a.astype(BF16).astype(F32)
        mid = (log_a - hi).astype(BF16).astype(F32)
        low = (log_a - hi - mid).astype(BF16).astype(F32)
        part = lane & (DN_HEADS - 1)
        la[own, :] = jnp.where(part == 0, hi, jnp.where(part == 1, pltpu.roll(mid, 1, 1),
                                                        pltpu.roll(low, 2, 1))).astype(BF16)

    ii = lax.broadcasted_iota(jnp.int32, (CHUNK, CHUNK), 0)
    jj = lax.broadcasted_iota(jnp.int32, (CHUNK, CHUNK), 1)
    sums = jnp.concatenate([_indicator(ii >= jj), _indicator(ii <= jj), jnp.ones((CHUNK, CHUNK), F32)],
                           axis=0).astype(BF16)
    eye = _indicator(ii == jj)
    span = group * CHUNK

    def wy_group(gi, carry):
        rows = pl.ds(pl.multiple_of(gi * span, span), span)
        state_rows = pl.ds(pl.multiple_of(gi * group * LANES, group * LANES), group * LANES)
        chunks = lambda a: a.reshape(group, CHUNK, a.shape[-1])
        q, k, v = chunks(qs[rows, :]), chunks(ks[rows, :]), chunks(vs[rows, :])
        log_a = chunks(la[rows, :])
        parts = jnp.concatenate([_dot(sums, log_a[c]) for c in range(group)], axis=0)
        cums = parts + pltpu.roll(parts, LANES - 1, 1) + pltpu.roll(parts, LANES - 2, 1)
        cums = cums.reshape(group, 3 * CHUNK, LANES)
        gl[rows, :] = cums[:, 2 * CHUNK:].reshape(span, LANES)
        beta_all = chunks(bet[rows, :])
        k16 = k.astype(BF16)
        for direction, (kn, bn, qn, on) in enumerate(((kn_f, bn_f, qn_f, on_f), (kn_b, bn_b, qn_b, on_b))):
            col = DN_HEADS * direction
            g = cums[:, direction * CHUNK:(direction + 1) * CHUNK, col:col + 1]
            g_last = cums[:, 2 * CHUNK:, col:col + 1]
            beta = beta_all[:, :, 2 * DN_HEADS + col:2 * DN_HEADS + col + 1]
            if direction == 0:
                incl, strict = ii >= jj, ii > jj
            else:
                incl, strict = ii <= jj, ii < jj
            g_cols = jnp.swapaxes(jnp.broadcast_to(g, (group, CHUNK, LANES)), 1, 2)[:, :CHUNK, :]
            decay = jnp.exp(jnp.minimum(g - g_cols, 0.0))
            e_g = jnp.exp(g)
            k_beta = k * beta
            both = jnp.einsum('gik,gjk->gij', jnp.concatenate([k_beta, q], axis=1).astype(BF16), k16,
                              preferred_element_type=F32)
            m = jnp.where(strict, both[:, :CHUNK] * decay, 0.0)
            qk = jnp.where(incl, both[:, CHUNK:] * decay, 0.0).astype(BF16)
            p = -m
            t_inv = eye + p
            pb = p.astype(BF16)
            p = _bdot(pb, pb)
            for _ in range(CHUNK.bit_length() - 3):
                pb = p.astype(BF16)
                r = _bdot(jnp.concatenate([pb, t_inv.astype(BF16)], axis=1), pb)
                p, t_inv = r[:, :CHUNK], t_inv + r[:, CHUNK:]
            t_inv = t_inv + _bdot(t_inv.astype(BF16), p.astype(BF16))
            wu = _bdot(t_inv.astype(BF16),
                       jnp.concatenate([k_beta * e_g, v * beta], axis=2).astype(BF16)).astype(BF16)
            k_tail = (k * jnp.exp(g_last - g)).astype(BF16)
            kb = jnp.einsum('gik,gin->gkn', k_tail, wu, preferred_element_type=F32)
            qb = _bdot(qk, wu)
            kn[state_rows, :] = kb[:, :, :LANES].reshape(group * LANES, LANES).astype(BF16)
            bn[state_rows, :] = kb[:, :, LANES:].reshape(group * LANES, LANES)
            qn[rows, :] = (q * e_g - qb[:, :, :LANES]).reshape(span, LANES).astype(BF16)
            on[rows, :] = qb[:, :, LANES:].reshape(span, LANES)
        return carry

    lax.fori_loop(0, heads * n_chunks // group, wy_group, 0)

    def scan_chunk(n, state, direction, kn, bn, qn, on):
        col = DN_HEADS * direction
        rows = pl.ds(pl.multiple_of(n * CHUNK, CHUNK), CHUNK)
        state_rows = pl.ds(pl.multiple_of(n * LANES, LANES), LANES)
        s16 = state.astype(BF16)
        on[rows, :] = on[rows, :] + _dot(qn[rows, :], s16)
        s_decay = jnp.exp(gl[pl.ds(pl.multiple_of(n * CHUNK, CHUNK), 8), :][0:1, col:col + 1])
        return state * s_decay - _dot(kn[state_rows, :], s16) + bn[state_rows, :]

    def scan_step(i, states):
        out = []
        for h in range(heads):
            out.append(scan_chunk(h * n_chunks + i, states[2 * h], 0, kn_f, bn_f, qn_f, on_f))
            out.append(scan_chunk(h * n_chunks + n_chunks - 1 - i, states[2 * h + 1], 1, kn_b, bn_b, qn_b, on_b))
        return tuple(out)

    init = tuple(ref[0, h] for h in range(heads) for ref in (s0f_ref, s0b_ref))
    final = lax.fori_loop(0, n_chunks, scan_step, init)
    for h in range(heads):
        sf_ref[0, h] = final[2 * h]
        sb_ref[0, h] = final[2 * h + 1]
        own = pl.ds(h * seq, seq)
        o = on_f[own, :] + on_b[own, :]
        y = o * lax.rsqrt(jnp.mean(o * o, axis=-1, keepdims=True) + EPS) * dnw_ref[...]
        o_ref[0, h] = (y * _silu(gate_ref[0, h])).astype(BF16)


def _delta_net(qkv, ab, gate, conv_w, a_log, dt_bias, dn_norm, s0f, s0b, heads):
    b, _, seq, hd = qkv.shape
    lane_vec = lambda a: jnp.pad(a.reshape(1, -1), ((0, 0), (0, LANES - a.size)))
    vec = pl.BlockSpec((1, LANES), lambda i, s: (0, 0))
    part = DN_HEADS // heads
    per_head = lambda off: pl.BlockSpec((1, heads, seq, hd), lambda i, s: (i, off * part + s, 0, 0))
    cw = lambda off: pl.BlockSpec((conv_w.shape[0], heads * hd), lambda i, s: (0, off * part + s))
    st = pl.BlockSpec((1, heads, hd, hd), lambda i, s: (i, s, 0, 0))
    rows = heads * seq
    group = min(DN_GROUP, rows // CHUNK)
    shared_f32 = pltpu.VMEM((seq, hd), F32)
    rows_f32 = pltpu.VMEM((rows, hd), F32)
    rows_bf16 = pltpu.VMEM((rows, hd), BF16)
    mats_f32 = pltpu.VMEM((rows // CHUNK * hd, hd), F32)
    mats_bf16 = pltpu.VMEM((rows // CHUNK * hd, hd), BF16)
    return pl.pallas_call(
        functools.partial(_dn_kernel, group=group),
        grid=(b, part),
        in_specs=[vec, vec, per_head(0), per_head(1), per_head(2),
                  pl.BlockSpec((1, seq, LANES), lambda i, s: (i, 0, 0)),
                  per_head(0), cw(0), cw(1), cw(2),
                  pl.BlockSpec((1, hd), lambda i, s: (0, 0)), st, st],
        out_specs=[per_head(0), st, st],
        out_shape=[jax.ShapeDtypeStruct((b, DN_HEADS, seq, hd), BF16),
                   jax.ShapeDtypeStruct((b, DN_HEADS, hd, hd), F32),
                   jax.ShapeDtypeStruct((b, DN_HEADS, hd, hd), F32)],
        scratch_shapes=[rows_f32] * 3 + [shared_f32] * 2 + [rows_f32] * 2 + [rows_bf16]
                       + [mats_bf16] * 2 + [mats_f32] * 2 + [rows_bf16] * 2 + [rows_f32] * 2,
        compiler_params=_params("arbitrary", "arbitrary"),
        name="delta_net",
    )(lane_vec(a_log), lane_vec(dt_bias), qkv, qkv, qkv, ab, gate, conv_w, conv_w, conv_w, dn_norm, s0f, s0b)


def _window_sum(x, pos, limit, half, stride, seq):
    left = jnp.where(pos >= 1, pltpu.roll(x, stride, 0), 0.0)
    right = x
    k = 1
    while k < half:
        left = left + jnp.where(pos >= k, pltpu.roll(left, k * stride, 0), 0.0)
        right = right + jnp.where(pos + k < limit, pltpu.roll(right, seq - k * stride, 0), 0.0)
        k *= 2
    return left + right


def _window_count(pos, limit, half):
    return (jnp.minimum(pos + half, limit) - jnp.maximum(pos - half, 0)).astype(F32)


def _pool_kernel(u_ref, pw_ref, ps_ref, o_ref, *, rows):
    seq = u_ref.shape[1]
    t = lax.broadcasted_iota(jnp.int32, (seq, LANES), 0)
    for g, w in enumerate(POOL_WINDOWS):
        lanes = slice(g * LANES, (g + 1) * LANES)
        x = u_ref[0, :, lanes]
        half = w // 2
        if rows is None:
            total = _window_sum(x, t, seq, half, 1, seq)
            count = _window_count(t, seq, half)
        else:
            c, r = t & (GRID_W - 1), t >> (GRID_W.bit_length() - 1)
            total = _window_sum(_window_sum(x, c, GRID_W, half, 1, seq), r, rows, half, GRID_W, seq)
            count = _window_count(r, rows, half) * _window_count(c, GRID_W, half)
        m = (total / count - x).astype(BF16)
        o_ref[0, :, lanes] = (_dot(m, pw_ref[g]) * ps_ref[:, lanes]).astype(BF16)


def _pool_mixer(u, pool_w, pool_scale, rows):
    b, seq, width = u.shape
    blk = pl.BlockSpec((1, seq, width), lambda i: (i, 0, 0))
    return pl.pallas_call(
        functools.partial(_pool_kernel, rows=rows),
        grid=(b,),
        in_specs=[blk, pl.BlockSpec(pool_w.shape, lambda i: (0, 0, 0)), pl.BlockSpec((1, width), lambda i: (0, 0))],
        out_specs=blk,
        out_shape=jax.ShapeDtypeStruct((b, seq, width), BF16),
        compiler_params=_params("arbitrary"),
        name="pool_mixer",
    )(u, pool_w, pool_scale)


def _out_kernel(o_ref, p_ref, x_ref, wo_ref, g1_ref, nw_ref, sh_ref, sc_ref, wr_ref, x1_ref, h2_ref, aff_ref):
    sample = pl.program_id(1)
    o = jnp.concatenate([o_ref[0, h] for h in range(o_ref.shape[1])], axis=1)
    half = o.shape[1]
    y = _dot(o, wo_ref[:half, :]) + _dot(p_ref[0], wo_ref[half:, :])
    x1 = x_ref[0] + g1_ref[0] * y
    x1_ref[0] = x1
    h2 = _norm_mod(x1, nw_ref[...], sh_ref[0], sc_ref[0]).astype(BF16)
    h2_ref[0] = h2
    lane = lax.broadcasted_iota(jnp.int32, (1, LANES), 1)
    logits = jnp.where(lane < N_EXPERTS, _dot(h2, wr_ref[...]), -jnp.inf)
    ex = jnp.exp(logits - jnp.max(logits, axis=-1, keepdims=True))
    aff = pltpu.roll(ex / jnp.sum(ex, axis=-1, keepdims=True), sample * N_EXPERTS, 1)

    @pl.when(sample == 0)
    def _():
        aff_ref[...] = aff

    @pl.when(sample > 0)
    def _():
        aff_ref[...] += aff


def _out_proj(o, pooled, x, w_out, g1, nw, shift, scale, w_router, tm):
    b, l, d = x.shape
    assert b * N_EXPERTS <= LANES
    half = pooled.shape[2]
    row = lambda j, i: (i, 0, 0)
    tok = lambda j, i: (i, j, 0)
    fixed = lambda j, i: (0, 0)
    vec = pl.BlockSpec((1, 1, d), row)
    return pl.pallas_call(
        _out_kernel,
        grid=(l // tm, b),
        in_specs=[pl.BlockSpec((1, o.shape[1], tm, LANES), lambda j, i: (i, 0, j, 0)),
                  pl.BlockSpec((1, tm, half), tok), pl.BlockSpec((1, tm, d), tok),
                  pl.BlockSpec(w_out.shape, fixed), vec, pl.BlockSpec((1, d), fixed), vec, vec,
                  pl.BlockSpec(w_router.shape, fixed)],
        out_specs=[pl.BlockSpec((1, tm, d), tok), pl.BlockSpec((1, tm, d), tok),
                   pl.BlockSpec((tm, LANES), lambda j, i: (j, 0))],
        out_shape=[jax.ShapeDtypeStruct((b, l, d), F32), jax.ShapeDtypeStruct((b, l, d), BF16),
                   jax.ShapeDtypeStruct((l, LANES), F32)],
        compiler_params=_params("arbitrary", "arbitrary"),
        name="out_proj",
    )(o, pooled, x, w_out, g1, nw, shift, scale, w_router)


TOKEN_BLOCK = 256
MXU_DEPTH = 256
BF16_ROWS = 16
F32_ROWS = 8


def _route_kernel(aff_ref, slotc_ref, slotr_ref, affr_ref, starts_ref, *, cap):
    seq = aff_ref.shape[0]
    aff = aff_ref[...]

    def bit_step(it, lo_bits):
        cand_bits = lo_bits | jnp.left_shift(jnp.int32(1), 30 - it)
        cand = lax.bitcast_convert_type(cand_bits, F32)
        count = jnp.sum(_indicator(aff >= cand), axis=0, keepdims=True)
        return jnp.where(count >= cap, cand_bits, lo_bits)

    lo_bits = lax.fori_loop(0, 31, bit_step, jnp.zeros((1, LANES), jnp.int32))
    lo = lax.bitcast_convert_type(lo_bits, F32)
    hi = lax.bitcast_convert_type(lo_bits + 1, F32)
    above = aff >= hi
    tied = (aff >= lo) & (aff < hi)
    need = cap - jnp.sum(_indicator(above), axis=0, keepdims=True)
    flags = jnp.concatenate([_indicator(above), _indicator(tied)], axis=1)

    r = lax.broadcasted_iota(jnp.int32, (TOKEN_BLOCK, TOKEN_BLOCK), 0)
    c = lax.broadcasted_iota(jnp.int32, (TOKEN_BLOCK, TOKEN_BLOCK), 1)
    tri = _indicator(r > c, BF16)
    offset = jnp.zeros((1, 2 * LANES), F32)
    pieces = []
    for j in range(seq // TOKEN_BLOCK):
        blk = flags[j * TOKEN_BLOCK:(j + 1) * TOKEN_BLOCK]
        pieces.append(_dot(tri, blk.astype(BF16)) + offset)
        offset = offset + jnp.sum(blk, axis=0, keepdims=True)
    before = jnp.concatenate(pieces, axis=0)
    tied_before = before[:, LANES:]
    rank = before[:, :LANES] + jnp.minimum(tied_before, need)
    chosen = above | (tied & (tied_before < need))
    slot = jnp.where(chosen, rank, -1.0)
    slotc_ref[...] = slot
    slotr_ref[...] = slot.T
    affr_ref[...] = aff.T
    n_blocks = seq // TOKEN_BLOCK
    for j in range(n_blocks):
        starts_ref[j:j + 1, :] = rank[j * TOKEN_BLOCK:j * TOKEN_BLOCK + 1, :]
    starts_ref[n_blocks:, :] = jnp.full((starts_ref.shape[0] - n_blocks, LANES), cap, F32)


def _route(aff, cap):
    seq = aff.shape[0]
    start_rows = -(-(seq // TOKEN_BLOCK + 1) // 8) * 8
    return pl.pallas_call(
        functools.partial(_route_kernel, cap=cap),
        out_shape=[jax.ShapeDtypeStruct((seq, LANES), F32), jax.ShapeDtypeStruct((LANES, seq), F32),
                   jax.ShapeDtypeStruct((LANES, seq), F32), jax.ShapeDtypeStruct((start_rows, LANES), F32)],
        compiler_params=pltpu.CompilerParams(vmem_limit_bytes=VMEM_LIMIT),
        name="route",
    )(aff)


def _expert_kernel(starts_ref, h_ref, pos_ref, aff_ref, wg_ref, wu_ref, wd_ref, ye_ref,
                   wg_s, wu_s, wd_s, xe_s, w_s, *, cap, window):
    expert, step = pl.program_id(0), pl.program_id(1)

    @pl.when(step == 0)
    def _():
        wg_s[...] = wg_ref[0, 0].astype(BF16)
        wu_s[...] = wu_ref[0, 0].astype(BF16)
        wd_s[...] = wd_ref[0, 0].astype(BF16)

    samples, seq, _ = h_ref.shape
    n_blocks = seq // TOKEN_BLOCK

    def affinity_of(hit, aff):
        return jnp.broadcast_to(jnp.sum(jnp.where(hit, aff, 0.0), axis=1, keepdims=True), (hit.shape[0], LANES))

    def gather_full():
        slots = lax.broadcasted_iota(jnp.int32, (cap, seq), 0).astype(F32)
        for s in range(samples):
            hit = slots == pos_ref[s, 0]
            xe_s[s * cap:(s + 1) * cap, :] = _dot(_indicator(hit, BF16), h_ref[s])
            w_s[s * cap:(s + 1) * cap, :] = affinity_of(hit, aff_ref[s, 0])

    if samples > 1 or n_blocks == 1 or window >= cap:
        gather_full()
    else:
        base = step * N_EXPERTS + expert
        first, fits = [], None
        for j in range(n_blocks):
            lo, hi = starts_ref[j * LANES + base], starts_ref[(j + 1) * LANES + base]
            start = jnp.minimum(lo // F32_ROWS * F32_ROWS, cap - window)
            first.append(start)
            ok = hi - start <= window
            fits = ok if fits is None else fits & ok

        @pl.when(fits)
        def _():
            xe_s[...] = jnp.zeros(xe_s.shape, F32)
            w_s[...] = jnp.zeros(w_s.shape, F32)
            rel = lax.broadcasted_iota(jnp.int32, (window, TOKEN_BLOCK), 0).astype(F32)
            for j in range(n_blocks):
                tokens = slice(j * TOKEN_BLOCK, (j + 1) * TOKEN_BLOCK)
                hit = rel + first[j].astype(F32) == pos_ref[0, 0][:, tokens]
                rows = pl.ds(pl.multiple_of(first[j], F32_ROWS), window)
                xe_s[rows, :] += _dot(_indicator(hit, BF16), h_ref[0, tokens, :])
                w_s[rows, :] += affinity_of(hit, aff_ref[0, 0][:, tokens])

        pl.when(jnp.logical_not(fits))(gather_full)

    xe = xe_s[...].astype(BF16)
    hid = (_silu(_dot(xe, wg_s[...])) * _dot(xe, wu_s[...])).astype(BF16)
    ye = (_dot(hid, wd_s[...]) * w_s[:, 0:1]).astype(BF16)
    for s in range(samples):
        ye_ref[s, 0] = ye[s * cap:(s + 1) * cap]


def _experts(starts, h2, slot_rows, aff_rows, w_gate, w_up, w_down, layer, cap, samples):
    b, seq, d = h2.shape
    _, n_exp, _, ff = w_gate.shape
    lanes = b * n_exp
    wspec = lambda shape: pl.BlockSpec((1, 1) + shape, lambda e, i: (layer, e, 0, 0))
    wbuf = lambda shape: pltpu.VMEM(shape, BF16)
    lane_row = pl.BlockSpec((samples, 1, 1, seq), lambda e, i: (i, e, 0, 0))
    return pl.pallas_call(
        functools.partial(_expert_kernel, cap=cap, window=min(64, cap)),
        grid=(n_exp, b // samples),
        in_specs=[pl.BlockSpec(memory_space=pltpu.SMEM),
                  pl.BlockSpec((samples, seq, d), lambda e, i: (i, 0, 0)), lane_row, lane_row,
                  wspec((d, ff)), wspec((d, ff)), wspec((ff, d))],
        out_specs=pl.BlockSpec((samples, 1, cap, d), lambda e, i: (i, e, 0, 0)),
        out_shape=jax.ShapeDtypeStruct((b, n_exp, cap, d), BF16),
        scratch_shapes=[wbuf((d, ff)), wbuf((d, ff)), wbuf((ff, d)),
                        pltpu.VMEM((samples * cap, d), F32), pltpu.VMEM((samples * cap, LANES), F32)],
        compiler_params=_params("arbitrary", "arbitrary"),
        name="experts",
    )(starts, h2, slot_rows[:lanes].reshape(b, n_exp, 1, seq), aff_rows[:lanes].reshape(b, n_exp, 1, seq),
      w_gate, w_up, w_down)


def _combine_kernel(starts_ref, slot_ref, ye_ref, x_ref, g2_ref, o_ref, *, cap, window):
    sample, tile = pl.program_id(0), pl.program_id(1)
    tm = x_ref.shape[1]
    per_pass = MXU_DEPTH // window
    base = tile * LANES + sample * N_EXPERTS
    first, fits = [], None
    for e in range(N_EXPERTS):
        lo, hi = starts_ref[base + e], starts_ref[base + LANES + e]
        start = jnp.minimum(lo // BF16_ROWS * BF16_ROWS, cap - window)
        first.append(start)
        ok = hi - start <= window
        fits = ok if fits is None else fits & ok
    pos = pltpu.roll(slot_ref[...], jnp.where(sample == 0, 0, LANES - sample * N_EXPERTS), 1)

    @pl.when(fits)
    def _():
        lane = lax.broadcasted_iota(jnp.int32, (1, MXU_DEPTH), 1)
        rel = (lane % window).astype(F32)
        acc = jnp.zeros(x_ref.shape[1:], F32)
        for g in range(N_EXPERTS // per_pass):
            experts = range(g * per_pass, (g + 1) * per_pass)
            val = jnp.broadcast_to(pos[:, experts[0]:experts[0] + 1] - first[experts[0]].astype(F32), (tm, MXU_DEPTH))
            for k, e in enumerate(experts[1:], start=1):
                val = jnp.where(lane >= k * window, pos[:, e:e + 1] - first[e].astype(F32), val)
            rows = jnp.concatenate([ye_ref[0, e, pl.ds(pl.multiple_of(first[e], BF16_ROWS), window), :]
                                    for e in experts], axis=0)
            acc = acc + _dot(_indicator(val == rel, BF16), rows)
        o_ref[0] = x_ref[0] + g2_ref[0] * acc

    @pl.when(jnp.logical_not(fits))
    def _():
        slots = lax.broadcasted_iota(jnp.int32, (tm, cap), 1).astype(F32)
        acc = jnp.zeros(x_ref.shape[1:], F32)
        for e in range(N_EXPERTS):
            acc = acc + _dot(_indicator(pos[:, e:e + 1] == slots, BF16), ye_ref[0, e])
        o_ref[0] = x_ref[0] + g2_ref[0] * acc


def _combine(starts, slot_cols, ye, x1, g2, cap):
    b, l, d = x1.shape
    tm = TOKEN_BLOCK
    window = min(64, cap)
    tok = lambda i, j: (i, j, 0)
    return pl.pallas_call(
        functools.partial(_combine_kernel, cap=cap, window=window),
        grid=(b, l // tm),
        in_specs=[pl.BlockSpec(memory_space=pltpu.SMEM),
                  pl.BlockSpec((tm, LANES), lambda i, j: (j, 0)),
                  pl.BlockSpec((1, N_EXPERTS, cap, d), lambda i, j: (i, 0, 0, 0)),
                  pl.BlockSpec((1, tm, d), tok), pl.BlockSpec((1, 1, d), lambda i, j: (i, 0, 0))],
        out_specs=pl.BlockSpec((1, tm, d), tok),
        out_shape=jax.ShapeDtypeStruct((b, l, d), F32),
        compiler_params=_params("arbitrary", "arbitrary"),
        name="combine",
    )(starts, slot_cols, ye, x1, g2)


def _final_kernel(x_ref, nw_ref, o_ref):
    x = x_ref[0]
    o_ref[0] = x * lax.rsqrt(jnp.mean(x * x, axis=-1, keepdims=True) + EPS) * nw_ref[...]


def _final_norm(x, nw, tm):
    b, l, d = x.shape
    tok = lambda i, j: (i, j, 0)
    return pl.pallas_call(
        _final_kernel,
        grid=(b, l // tm),
        in_specs=[pl.BlockSpec((1, tm, d), tok), pl.BlockSpec((1, d), lambda i, j: (0, 0))],
        out_specs=pl.BlockSpec((1, tm, d), tok),
        out_shape=jax.ShapeDtypeStruct((b, l, d), F32),
        compiler_params=_params("arbitrary", "arbitrary"),
        name="final_norm",
    )(x, nw)


def _token_mixer(x, mod, lw, states, rows, tm, heads):
    sh1, sc1 = mod[0], mod[1]
    qkv, ab, gate, pool = _in_proj(x, sh1, sc1, lw["norm1"], lw["wqkv"], lw["wab"], lw["wgate"], lw["wpool"], tm)
    o, s_f, s_b = _delta_net(qkv, ab, gate, lw["conv_w"], lw["a_log"], lw["dt_bias"], lw["dn_norm"], *states, heads)
    pooled = _pool_mixer(pool, lw["pool_w"], lw["pool_scale"], rows)
    return o, pooled, (s_f, s_b)


def _ffn(o, pooled, x, mod, lw, tm, samples):
    seq = x.shape[1]
    cap = EC_CAPACITY * seq // N_EXPERTS
    x1, h2, aff = _out_proj(o, pooled, x, lw["w_out"], mod[2], lw["norm2"], mod[3], mod[4], lw["w_router"], tm)
    slot_cols, slot_rows, aff_rows, starts = _route(aff, cap)
    starts = starts.astype(jnp.int32).reshape(-1)
    ye = _experts(starts, h2, slot_rows, aff_rows, lw["w_gate"], lw["w_up"], lw["w_down"], lw["layer"], cap, samples)
    return _combine(starts, slot_cols, ye, x1, mod[5], cap)


def kernel(x, c, ctx, c_ctx, w_mod, b_mod, norm1, norm2, w_in, conv_w, a_log, dt_bias, dn_norm, pool_w, pool_scale,
           w_out, w_router, w_gate, w_up, w_down, norm_f):
    batch, seq, d = x.shape
    depth = w_mod.shape[0]
    ctx_len = ctx.shape[1]
    dn_width = DN_HEADS * LANES
    qkv_cols = 3 * dn_width
    gate_cols = 2 * N_DIR * DN_HEADS
    state_cols = qkv_cols + gate_cols
    rows = seq // GRID_W
    tm_x, tm_z = 512, ctx_len

    cond_rows = 16
    cond = jnp.zeros((cond_rows, d), F32).at[:batch].set(c).at[batch].set(c_ctx)
    mod_all = _adaln_all(cond, w_mod, b_mod)

    zero_state = jnp.zeros((batch, DN_HEADS, LANES, LANES), F32)
    z = ctx
    for l in range(depth):
        wl = w_in[l]
        lw = dict(
            norm1=norm1[l][None], norm2=norm2[l][None],
            wqkv=wl[:, :qkv_cols].astype(BF16),
            wab=jnp.pad(wl[:, qkv_cols:state_cols], ((0, 0), (0, LANES - gate_cols))).astype(BF16),
            wgate=wl[:, state_cols:state_cols + dn_width].astype(BF16),
            wpool=wl[:, state_cols + dn_width:].astype(BF16),
            conv_w=conv_w[l], a_log=a_log[l], dt_bias=dt_bias[l], dn_norm=dn_norm[l][None],
            pool_w=pool_w[l].astype(BF16), pool_scale=pool_scale[l][None],
            w_out=w_out[l].astype(BF16),
            w_router=jnp.pad(w_router[l], ((0, 0), (0, LANES - N_EXPERTS))).astype(BF16),
            w_gate=w_gate, w_up=w_up, w_down=w_down, layer=l,
        )
        mods = mod_all[l].reshape(cond_rows, 6, d)
        mod_x = [mods[:batch, i][:, None, :] for i in range(6)]
        mod_z = [jnp.broadcast_to(mods[batch, i][None, None, :], (batch, 1, d)) for i in range(6)]

        o_z, pooled_z, ctx_states = _token_mixer(z, mod_z, lw, (zero_state, zero_state), None, tm_z, DN_HEADS)
        if l < depth - 1:
            z = _ffn(o_z, pooled_z, z, mod_z, lw, tm_z, batch)
        o_x, pooled_x, _ = _token_mixer(x, mod_x, lw, ctx_states, rows, tm_x, 1)
        x = _ffn(o_x, pooled_x, x, mod_x, lw, tm_x, 1)
    return _final_norm(x, norm_f[None], tm_x)
```

```python
import functools

import jax
import jax.numpy as jnp
from jax import lax
from jax.experimental import pallas as pl
from jax.experimental.pallas import tpu as pltpu

F32 = jnp.float32
BF16 = jnp.bfloat16

LANES = 128
F32_ROWS = 8
BF16_ROWS = 16
GRID_W = 64
DN_HEADS = 4
N_DIR = 2
CHUNK = 64
DN_GROUP = 16
POOL_WINDOWS = (2, 4, 8, 16)
N_EXPERTS = 16
EC_CAPACITY = 2
EPS = 1e-6
VMEM_LIMIT = 56 * 1024 * 1024

def _params(*semantics):
    return pltpu.CompilerParams(dimension_semantics=semantics, vmem_limit_bytes=VMEM_LIMIT)


def _silu(x):
    return x * jax.nn.sigmoid(x)


def _dot(a, b):
    return jnp.dot(a, b, preferred_element_type=F32)


def _indicator(mask, dtype=F32):
    return jnp.where(mask, 1.0, 0.0).astype(dtype)


def _mod_kernel(cond_ref, w_ref, b_ref, o_ref):
    s = _silu(cond_ref[...])
    s_hi = s.astype(BF16)
    s_lo = (s - s_hi.astype(F32)).astype(BF16)
    w = w_ref[0]
    w_hi = w.astype(BF16)
    w_lo = (w - w_hi.astype(F32)).astype(BF16)
    rows = s.shape[0]
    both = _dot(jnp.concatenate([s_hi, s_lo], axis=0), w_hi)
    o_ref[0] = both[:rows] + both[rows:] + _dot(s_hi, w_lo) + b_ref[0]


def _adaln_all(cond, w_mod, b_mod):
    depth, d, n = w_mod.shape
    rows = cond.shape[0]
    tn = 1536
    return pl.pallas_call(
        _mod_kernel,
        grid=(depth, n // tn),
        in_specs=[pl.BlockSpec((rows, d), lambda l, j: (0, 0)),
                  pl.BlockSpec((1, d, tn), lambda l, j: (l, 0, j)),
                  pl.BlockSpec((1, 1, tn), lambda l, j: (l, 0, j))],
        out_specs=pl.BlockSpec((1, rows, tn), lambda l, j: (l, 0, j)),
        out_shape=jax.ShapeDtypeStruct((depth, rows, n), F32),
        compiler_params=_params("arbitrary", "arbitrary"),
        name="adaln",
    )(cond, w_mod, b_mod.reshape(depth, 1, n))


def _norm_mod(x, nw, shift, scale):
    ms = jnp.mean(x * x, axis=-1, keepdims=True)
    return (x * lax.rsqrt(ms + EPS) * nw) * (1.0 + scale) + shift


def _in_kernel(x_ref, sh_ref, sc_ref, nw_ref, wqkv_ref, wab_ref, wgate_ref, wpool_ref,
               qkv_ref, ab_ref, gate_ref, pool_ref):
    h = _norm_mod(x_ref[0], nw_ref[...], sh_ref[0], sc_ref[0]).astype(BF16)
    qkv = _dot(h, wqkv_ref[...])
    for c in range(qkv_ref.shape[1]):
        qkv_ref[0, c] = qkv[:, c * LANES:(c + 1) * LANES]
    gate = _dot(h, wgate_ref[...])
    for c in range(gate_ref.shape[1]):
        gate_ref[0, c] = gate[:, c * LANES:(c + 1) * LANES]
    ab_ref[0] = _dot(h, wab_ref[...])
    pool_ref[0] = _dot(h, wpool_ref[...])


def _in_proj(x, shift, scale, nw, wqkv, wab, wgate, wpool, tm):
    b, l, d = x.shape
    row = lambda i, j: (i, 0, 0)
    tok = lambda i, j: (i, j, 0)
    fixed = lambda i, j: (0, 0)
    heads = lambda w: w.shape[1] // LANES
    per_head = lambda w: pl.BlockSpec((1, heads(w), tm, LANES), lambda i, j: (i, 0, j, 0))
    return pl.pallas_call(
        _in_kernel,
        grid=(b, l // tm),
        in_specs=[pl.BlockSpec((1, tm, d), tok),
                  pl.BlockSpec((1, 1, d), row), pl.BlockSpec((1, 1, d), row),
                  pl.BlockSpec((1, d), fixed)]
                 + [pl.BlockSpec(w.shape, fixed) for w in (wqkv, wab, wgate, wpool)],
        out_specs=[per_head(wqkv), pl.BlockSpec((1, tm, wab.shape[1]), tok), per_head(wgate),
                   pl.BlockSpec((1, tm, wpool.shape[1]), tok)],
        out_shape=[jax.ShapeDtypeStruct((b, heads(wqkv), l, LANES), F32),
                   jax.ShapeDtypeStruct((b, l, wab.shape[1]), F32),
                   jax.ShapeDtypeStruct((b, heads(wgate), l, LANES), F32),
                   jax.ShapeDtypeStruct((b, l, wpool.shape[1]), F32)],
        compiler_params=_params("arbitrary", "arbitrary"),
        name="in_proj",
    )(x, shift, scale, nw, wqkv, wab, wgate, wpool)


def _bdot(a, b):
    return jnp.einsum('gij,gjk->gik', a, b, preferred_element_type=F32)


def _dn_kernel(alog_ref, dtb_ref, q_ref, k_ref, v_ref, ab_ref, gate_ref, cwq_ref, cwk_ref, cwv_ref,
               dnw_ref, s0f_ref, s0b_ref, o_ref, sf_ref, sb_ref,
               qs, ks, vs, bet_all, la_all, bet, gl, la, kn_f, kn_b, bn_f, bn_b, qn_f, qn_b, on_f, on_b, pad,
               *, group):
    step = pl.program_id(1)
    _, heads, seq, _ = q_ref.shape
    n_chunks = seq // CHUNK
    lane = lax.broadcasted_iota(jnp.int32, (1, LANES), 1)

    edge = jnp.zeros((F32_ROWS, LANES), F32)
    pad[0:F32_ROWS, :] = edge
    pad[pl.ds(F32_ROWS + seq, F32_ROWS), :] = edge

    def conv_silu(x, w):
        pad[pl.ds(F32_ROWS, seq), :] = x
        y = (pad[pl.ds(F32_ROWS - 2, seq), :] * w[0:1] + pad[pl.ds(F32_ROWS - 1, seq), :] * w[1:2]
             + x * w[2:3] + pad[pl.ds(F32_ROWS + 1, seq), :] * w[3:4])
        return _silu(y)

    def l2norm(x):
        return x * lax.rsqrt(jnp.sum(x * x, axis=-1, keepdims=True) + EPS)

    @pl.when(step == 0)
    def _():
        ab = ab_ref[0]
        bet_all[...] = jax.nn.sigmoid(ab)
        la_all[...] = -jnp.exp(alog_ref[...]) * jax.nn.softplus(ab + dtb_ref[...])

    for h in range(heads):
        own = pl.ds(h * seq, seq)
        lanes = slice(h * LANES, (h + 1) * LANES)
        qs[own, :] = l2norm(conv_silu(q_ref[0, h], cwq_ref[:, lanes])) * (LANES ** -0.5)
        ks[own, :] = l2norm(conv_silu(k_ref[0, h], cwk_ref[:, lanes]))
        vs[own, :] = conv_silu(v_ref[0, h], cwv_ref[:, lanes])
        head = step * heads + h
        shift = jnp.where(head == 0, 0, LANES - head)
        bet[own, :] = pltpu.roll(bet_all[...], shift, 1)
        log_a = pltpu.roll(la_all[...], shift, 1)
        hi = log_a.astype(BF16).astype(F32)
        mid = (log_a - hi).astype(BF16).astype(F32)
        low = (log_a - hi - mid).astype(BF16).astype(F32)
        part = lane & (DN_HEADS - 1)
        la[own, :] = jnp.where(part == 0, hi, jnp.where(part == 1, pltpu.roll(mid, 1, 1),
                                                        pltpu.roll(low, 2, 1))).astype(BF16)

    ii = lax.broadcasted_iota(jnp.int32, (CHUNK, CHUNK), 0)
    jj = lax.broadcasted_iota(jnp.int32, (CHUNK, CHUNK), 1)
    sums = jnp.concatenate([_indicator(ii >= jj), _indicator(ii <= jj), jnp.ones((CHUNK, CHUNK), F32)],
                           axis=0).astype(BF16)
    eye = _indicator(ii == jj)
    span = group * CHUNK

    def wy_group(gi, carry):
        rows = pl.ds(pl.multiple_of(gi * span, span), span)
        state_rows = pl.ds(pl.multiple_of(gi * group * LANES, group * LANES), group * LANES)
        chunks = lambda a: a.reshape(group, CHUNK, a.shape[-1])
        q, k, v = chunks(qs[rows, :]), chunks(ks[rows, :]), chunks(vs[rows, :])
        log_a = chunks(la[rows, :])
        parts = jnp.concatenate([_dot(sums, log_a[c]) for c in range(group)], axis=0)
        cums = parts + pltpu.roll(parts, LANES - 1, 1) + pltpu.roll(parts, LANES - 2, 1)
        cums = cums.reshape(group, 3 * CHUNK, LANES)
        gl[rows, :] = cums[:, 2 * CHUNK:].reshape(span, LANES)
        beta_all = chunks(bet[rows, :])
        k16 = k.astype(BF16)
        for direction, (kn, bn, qn, on) in enumerate(((kn_f, bn_f, qn_f, on_f), (kn_b, bn_b, qn_b, on_b))):
            col = DN_HEADS * direction
            g = cums[:, direction * CHUNK:(direction + 1) * CHUNK, col:col + 1]
            g_last = cums[:, 2 * CHUNK:, col:col + 1]
            beta = beta_all[:, :, 2 * DN_HEADS + col:2 * DN_HEADS + col + 1]
            if direction == 0:
                incl, strict = ii >= jj, ii > jj
            else:
                incl, strict = ii <= jj, ii < jj
            g_cols = jnp.swapaxes(jnp.broadcast_to(g, (group, CHUNK, LANES)), 1, 2)[:, :CHUNK, :]
            decay = jnp.exp(jnp.minimum(g - g_cols, 0.0))
            e_g = jnp.exp(g)
            k_beta = k * beta
            both = jnp.einsum('gik,gjk->gij', jnp.concatenate([k_beta, q], axis=1).astype(BF16), k16,
                              preferred_element_type=F32)
            m = jnp.where(strict, both[:, :CHUNK] * decay, 0.0)
            qk = jnp.where(incl, both[:, CHUNK:] * decay, 0.0).astype(BF16)
            p = -m
            t_inv = eye + p
            pb = p.astype(BF16)
            p = _bdot(pb, pb)
            for _ in range(CHUNK.bit_length() - 3):
                pb = p.astype(BF16)
                r = _bdot(jnp.concatenate([pb, t_inv.astype(BF16)], axis=1), pb)
                p, t_inv = r[:, :CHUNK], t_inv + r[:, CHUNK:]
            t_inv = t_inv + _bdot(t_inv.astype(BF16), p.astype(BF16))
            wu = _bdot(t_inv.astype(BF16),
                       jnp.concatenate([k_beta * e_g, v * beta], axis=2).astype(BF16)).astype(BF16)
            k_tail = (k * jnp.exp(g_last - g)).astype(BF16)
            kb = jnp.einsum('gik,gin->gkn', k_tail, wu, preferred_element_type=F32)
            qb = _bdot(qk, wu)
            kn[state_rows, :] = kb[:, :, :LANES].reshape(group * LANES, LANES).astype(BF16)
            bn[state_rows, :] = kb[:, :, LANES:].reshape(group * LANES, LANES)
            qn[rows, :] = (q * e_g - qb[:, :, :LANES]).reshape(span, LANES).astype(BF16)
            on[rows, :] = qb[:, :, LANES:].reshape(span, LANES)
        return carry

    lax.fori_loop(0, heads * n_chunks // group, wy_group, 0)

    def scan_chunk(n, state, direction, kn, bn, qn, on):
        col = DN_HEADS * direction
        rows = pl.ds(pl.multiple_of(n * CHUNK, CHUNK), CHUNK)
        state_rows = pl.ds(pl.multiple_of(n * LANES, LANES), LANES)
        s16 = state.astype(BF16)
        on[rows, :] = on[rows, :] + _dot(qn[rows, :], s16)
        s_decay = jnp.exp(gl[pl.ds(pl.multiple_of(n * CHUNK, CHUNK), 8), :][0:1, col:col + 1])
        return state * s_decay - _dot(kn[state_rows, :], s16) + bn[state_rows, :]

    def scan_step(i, states):
        out = []
        for h in range(heads):
            out.append(scan_chunk(h * n_chunks + i, states[2 * h], 0, kn_f, bn_f, qn_f, on_f))
            out.append(scan_chunk(h * n_chunks + n_chunks - 1 - i, states[2 * h + 1], 1, kn_b, bn_b, qn_b, on_b))
        return tuple(out)

    init = tuple(ref[0, h] for h in range(heads) for ref in (s0f_ref, s0b_ref))
    final = lax.fori_loop(0, n_chunks, scan_step, init)
    for h in range(heads):
        sf_ref[0, h] = final[2 * h]
        sb_ref[0, h] = final[2 * h + 1]
        own = pl.ds(h * seq, seq)
        o = on_f[own, :] + on_b[own, :]
        y = o * lax.rsqrt(jnp.mean(o * o, axis=-1, keepdims=True) + EPS) * dnw_ref[...]
        o_ref[0, h] = (y * _silu(gate_ref[0, h])).astype(BF16)


def _delta_net(qkv, ab, gate, conv_w, a_log, dt_bias, dn_norm, s0f, s0b, heads):
    b, _, seq, hd = qkv.shape
    lane_vec = lambda a: jnp.pad(a.reshape(1, -1), ((0, 0), (0, LANES - a.size)))
    vec = pl.BlockSpec((1, LANES), lambda i, s: (0, 0))
    part = DN_HEADS // heads
    per_head = lambda off: pl.BlockSpec((1, heads, seq, hd), lambda i, s: (i, off * part + s, 0, 0))
    cw = lambda off: pl.BlockSpec((conv_w.shape[0], heads * hd), lambda i, s: (0, off * part + s))
    st = pl.BlockSpec((1, heads, hd, hd), lambda i, s: (i, s, 0, 0))
    rows = heads * seq
    group = min(DN_GROUP, rows // CHUNK)
    shared_f32 = pltpu.VMEM((seq, hd), F32)
    rows_f32 = pltpu.VMEM((rows, hd), F32)
    rows_bf16 = pltpu.VMEM((rows, hd), BF16)
    mats_f32 = pltpu.VMEM((rows // CHUNK * hd, hd), F32)
    mats_bf16 = pltpu.VMEM((rows // CHUNK * hd, hd), BF16)
    return pl.pallas_call(
        functools.partial(_dn_kernel, group=group),
        grid=(b, part),
        in_specs=[vec, vec, per_head(0), per_head(1), per_head(2),
                  pl.BlockSpec((1, seq, LANES), lambda i, s: (i, 0, 0)),
                  per_head(0), cw(0), cw(1), cw(2),
                  pl.BlockSpec((1, hd), lambda i, s: (0, 0)), st, st],
        out_specs=[per_head(0), st, st],
        out_shape=[jax.ShapeDtypeStruct((b, DN_HEADS, seq, hd), BF16),
                   jax.ShapeDtypeStruct((b, DN_HEADS, hd, hd), F32),
                   jax.ShapeDtypeStruct((b, DN_HEADS, hd, hd), F32)],
        scratch_shapes=[rows_f32] * 3 + [shared_f32] * 2 + [rows_f32] * 2 + [rows_bf16]
                       + [mats_bf16] * 2 + [mats_f32] * 2 + [rows_bf16] * 2 + [rows_f32] * 2
                       + [pltpu.VMEM((seq + 2 * F32_ROWS, hd), F32)],
        compiler_params=_params("arbitrary", "arbitrary"),
        name="delta_net",
    )(lane_vec(a_log), lane_vec(dt_bias), qkv, qkv, qkv, ab, gate, conv_w, conv_w, conv_w, dn_norm, s0f, s0b)


def _window_sum(x, pos, limit, half, stride, seq):
    left = jnp.where(pos >= 1, pltpu.roll(x, stride, 0), 0.0)
    right = x
    k = 1
    while k < half:
        left = left + jnp.where(pos >= k, pltpu.roll(left, k * stride, 0), 0.0)
        right = right + jnp.where(pos + k < limit, pltpu.roll(right, seq - k * stride, 0), 0.0)
        k *= 2
    return left + right


def _window_count(pos, limit, half):
    return (jnp.minimum(pos + half, limit) - jnp.maximum(pos - half, 0)).astype(F32)


def _pool_kernel(u_ref, pw_ref, ps_ref, o_ref, *, rows):
    seq = u_ref.shape[1]
    t = lax.broadcasted_iota(jnp.int32, (seq, LANES), 0)
    for g, w in enumerate(POOL_WINDOWS):
        lanes = slice(g * LANES, (g + 1) * LANES)
        x = u_ref[0, :, lanes]
        half = w // 2
        if rows is None:
            total = _window_sum(x, t, seq, half, 1, seq)
            count = _window_count(t, seq, half)
        else:
            c, r = t & (GRID_W - 1), t >> (GRID_W.bit_length() - 1)
            total = _window_sum(_window_sum(x, c, GRID_W, half, 1, seq), r, rows, half, GRID_W, seq)
            count = _window_count(r, rows, half) * _window_count(c, GRID_W, half)
        m = (total / count - x).astype(BF16)
        o_ref[0, :, lanes] = (_dot(m, pw_ref[g]) * ps_ref[:, lanes]).astype(BF16)


def _pool_mixer(u, pool_w, pool_scale, rows):
    b, seq, width = u.shape
    blk = pl.BlockSpec((1, seq, width), lambda i: (i, 0, 0))
    return pl.pallas_call(
        functools.partial(_pool_kernel, rows=rows),
        grid=(b,),
        in_specs=[blk, pl.BlockSpec(pool_w.shape, lambda i: (0, 0, 0)), pl.BlockSpec((1, width), lambda i: (0, 0))],
        out_specs=blk,
        out_shape=jax.ShapeDtypeStruct((b, seq, width), BF16),
        compiler_params=_params("arbitrary"),
        name="pool_mixer",
    )(u, pool_w, pool_scale)


def _out_kernel(o_ref, p_ref, x_ref, wo_ref, g1_ref, nw_ref, sh_ref, sc_ref, wr_ref, x1_ref, h2_ref, aff_ref):
    sample = pl.program_id(1)
    o = jnp.concatenate([o_ref[0, h] for h in range(o_ref.shape[1])], axis=1)
    half = o.shape[1]
    y = _dot(o, wo_ref[:half, :]) + _dot(p_ref[0], wo_ref[half:, :])
    x1 = x_ref[0] + g1_ref[0] * y
    x1_ref[0] = x1
    h2 = _norm_mod(x1, nw_ref[...], sh_ref[0], sc_ref[0]).astype(BF16)
    h2_ref[0] = h2
    lane = lax.broadcasted_iota(jnp.int32, (1, LANES), 1)
    logits = jnp.where(lane < N_EXPERTS, _dot(h2, wr_ref[...]), -jnp.inf)
    ex = jnp.exp(logits - jnp.max(logits, axis=-1, keepdims=True))
    aff = pltpu.roll(ex / jnp.sum(ex, axis=-1, keepdims=True), sample * N_EXPERTS, 1)

    @pl.when(sample == 0)
    def _():
        aff_ref[...] = aff

    @pl.when(sample > 0)
    def _():
        aff_ref[...] += aff


def _out_proj(o, pooled, x, w_out, g1, nw, shift, scale, w_router, tm):
    b, l, d = x.shape
    assert b * N_EXPERTS <= LANES
    half = pooled.shape[2]
    row = lambda j, i: (i, 0, 0)
    tok = lambda j, i: (i, j, 0)
    fixed = lambda j, i: (0, 0)
    vec = pl.BlockSpec((1, 1, d), row)
    return pl.pallas_call(
        _out_kernel,
        grid=(l // tm, b),
        in_specs=[pl.BlockSpec((1, o.shape[1], tm, LANES), lambda j, i: (i, 0, j, 0)),
                  pl.BlockSpec((1, tm, half), tok), pl.BlockSpec((1, tm, d), tok),
                  pl.BlockSpec(w_out.shape, fixed), vec, pl.BlockSpec((1, d), fixed), vec, vec,
                  pl.BlockSpec(w_router.shape, fixed)],
        out_specs=[pl.BlockSpec((1, tm, d), tok), pl.BlockSpec((1, tm, d), tok),
                   pl.BlockSpec((tm, LANES), lambda j, i: (j, 0))],
        out_shape=[jax.ShapeDtypeStruct((b, l, d), F32), jax.ShapeDtypeStruct((b, l, d), BF16),
                   jax.ShapeDtypeStruct((l, LANES), F32)],
        compiler_params=_params("arbitrary", "arbitrary"),
        name="out_proj",
    )(o, pooled, x, w_out, g1, nw, shift, scale, w_router)


TOKEN_BLOCK = 256
MXU_DEPTH = 256


def _route_kernel(aff_ref, slotc_ref, slotr_ref, affr_ref, starts_ref, *, cap):
    seq = aff_ref.shape[0]
    aff = aff_ref[...]

    def bit_step(it, lo_bits):
        cand_bits = lo_bits | jnp.left_shift(jnp.int32(1), 30 - it)
        cand = lax.bitcast_convert_type(cand_bits, F32)
        count = jnp.sum(_indicator(aff >= cand), axis=0, keepdims=True)
        return jnp.where(count >= cap, cand_bits, lo_bits)

    lo_bits = lax.fori_loop(0, 31, bit_step, jnp.zeros((1, LANES), jnp.int32))
    lo = lax.bitcast_convert_type(lo_bits, F32)
    hi = lax.bitcast_convert_type(lo_bits + 1, F32)
    above = aff >= hi
    tied = (aff >= lo) & (aff < hi)
    need = cap - jnp.sum(_indicator(above), axis=0, keepdims=True)
    flags = jnp.concatenate([_indicator(above), _indicator(tied)], axis=1)

    r = lax.broadcasted_iota(jnp.int32, (TOKEN_BLOCK, TOKEN_BLOCK), 0)
    c = lax.broadcasted_iota(jnp.int32, (TOKEN_BLOCK, TOKEN_BLOCK), 1)
    tri = _indicator(r > c, BF16)
    offset = jnp.zeros((1, 2 * LANES), F32)
    pieces = []
    for j in range(seq // TOKEN_BLOCK):
        blk = flags[j * TOKEN_BLOCK:(j + 1) * TOKEN_BLOCK]
        pieces.append(_dot(tri, blk.astype(BF16)) + offset)
        offset = offset + jnp.sum(blk, axis=0, keepdims=True)
    before = jnp.concatenate(pieces, axis=0)
    tied_before = before[:, LANES:]
    rank = before[:, :LANES] + jnp.minimum(tied_before, need)
    chosen = above | (tied & (tied_before < need))
    slot = jnp.where(chosen, rank, -1.0)
    slotc_ref[...] = slot
    slotr_ref[...] = slot.T
    affr_ref[...] = aff.T
    n_blocks = seq // TOKEN_BLOCK
    for j in range(n_blocks):
        starts_ref[j:j + 1, :] = rank[j * TOKEN_BLOCK:j * TOKEN_BLOCK + 1, :]
    starts_ref[n_blocks:, :] = jnp.full((starts_ref.shape[0] - n_blocks, LANES), cap, F32)


def _route(aff, cap):
    seq = aff.shape[0]
    start_rows = -(-(seq // TOKEN_BLOCK + 1) // 8) * 8
    return pl.pallas_call(
        functools.partial(_route_kernel, cap=cap),
        out_shape=[jax.ShapeDtypeStruct((seq, LANES), F32), jax.ShapeDtypeStruct((LANES, seq), F32),
                   jax.ShapeDtypeStruct((LANES, seq), F32), jax.ShapeDtypeStruct((start_rows, LANES), F32)],
        compiler_params=pltpu.CompilerParams(vmem_limit_bytes=VMEM_LIMIT),
        name="route",
    )(aff)


def _affinity_of(hit, aff):
    return jnp.broadcast_to(jnp.sum(jnp.where(hit, aff, 0.0), axis=1, keepdims=True), (hit.shape[0], LANES))


def _gather_full(h_ref, pos_ref, aff_ref, xe_s, w_s, cap):
    samples, seq, _ = h_ref.shape
    slots = lax.broadcasted_iota(jnp.int32, (cap, seq), 0).astype(F32)
    for s in range(samples):
        hit = slots == pos_ref[s, 0]
        xe_s[s * cap:(s + 1) * cap, :] = _dot(_indicator(hit, BF16), h_ref[s])
        w_s[s * cap:(s + 1) * cap, :] = _affinity_of(hit, aff_ref[s, 0])


def _gather_blocks(starts_ref, base, h_ref, pos_ref, aff_ref, xe_s, w_s, cap, window):
    seq = h_ref.shape[1]
    n_blocks = seq // TOKEN_BLOCK
    if n_blocks == 1 or window >= cap:
        _gather_full(h_ref, pos_ref, aff_ref, xe_s, w_s, cap)
        return
    first, fits = [], None
    for j in range(n_blocks):
        lo, hi = starts_ref[j * LANES + base], starts_ref[(j + 1) * LANES + base]
        start = jnp.minimum(lo // F32_ROWS * F32_ROWS, cap - window)
        first.append(start)
        ok = hi - start <= window
        fits = ok if fits is None else fits & ok

    @pl.when(fits)
    def _():
        xe_s[...] = jnp.zeros(xe_s.shape, F32)
        w_s[...] = jnp.zeros(w_s.shape, F32)
        rel = lax.broadcasted_iota(jnp.int32, (window, TOKEN_BLOCK), 0).astype(F32)
        for j in range(n_blocks):
            tokens = slice(j * TOKEN_BLOCK, (j + 1) * TOKEN_BLOCK)
            hit = rel + first[j].astype(F32) == pos_ref[0, 0][:, tokens]
            rows = pl.ds(pl.multiple_of(first[j], F32_ROWS), window)
            xe_s[rows, :] += _dot(_indicator(hit, BF16), h_ref[0, tokens, :])
            w_s[rows, :] += _affinity_of(hit, aff_ref[0, 0][:, tokens])

    @pl.when(jnp.logical_not(fits))
    def _():
        _gather_full(h_ref, pos_ref, aff_ref, xe_s, w_s, cap)


def _expert_kernel(starts_ref, hx_ref, px_ref, ax_ref, *refs, cap_x, cap_z, window):
    if cap_z:
        hz_ref, pz_ref, az_ref, wg_ref, wu_ref, wd_ref, yx_ref, yz_ref, wg_s, wu_s, wd_s, xe_s, w_s = refs
    else:
        wg_ref, wu_ref, wd_ref, yx_ref, wg_s, wu_s, wd_s, xe_s, w_s = refs
    expert, step = pl.program_id(0), pl.program_id(1)
    n_latent = pl.num_programs(1) - (1 if cap_z else 0)

    @pl.when(step == 0)
    def _():
        wg_s[...] = wg_ref[0, 0].astype(BF16)
        wu_s[...] = wu_ref[0, 0].astype(BF16)
        wd_s[...] = wd_ref[0, 0].astype(BF16)

    @pl.when(step < n_latent)
    def _():
        _gather_blocks(starts_ref, step * N_EXPERTS + expert, hx_ref, px_ref, ax_ref, xe_s, w_s, cap_x, window)

    if cap_z:
        @pl.when(step == n_latent)
        def _():
            _gather_full(hz_ref, pz_ref, az_ref, xe_s, w_s, cap_z)

    xe = xe_s[...].astype(BF16)
    hid = (_silu(_dot(xe, wg_s[...])) * _dot(xe, wu_s[...])).astype(BF16)
    ye = (_dot(hid, wd_s[...]) * w_s[:, 0:1]).astype(BF16)

    @pl.when(step < n_latent)
    def _():
        yx_ref[0, 0] = ye

    if cap_z:
        @pl.when(step == n_latent)
        def _():
            for s in range(yz_ref.shape[0]):
                yz_ref[s, 0] = ye[s * cap_z:(s + 1) * cap_z]


def _experts(w_gate, w_up, w_down, layer, x, z=None):
    b, seq_x, d = x["h2"].shape
    _, n_exp, _, ff = w_gate.shape
    cap_x = EC_CAPACITY * seq_x // n_exp
    lanes = b * n_exp
    rows = lambda a: a[:lanes].reshape(b, n_exp, 1, a.shape[1])
    sample = lambda i: jnp.minimum(i, b - 1)
    wspec = lambda shape: pl.BlockSpec((1, 1) + shape, lambda e, i: (layer, e, 0, 0))
    wbuf = lambda shape: pltpu.VMEM(shape, BF16)
    row_x = pl.BlockSpec((1, 1, 1, seq_x), lambda e, i: (sample(i), e, 0, 0))
    in_specs = [pl.BlockSpec(memory_space=pltpu.SMEM),
                pl.BlockSpec((1, seq_x, d), lambda e, i: (sample(i), 0, 0)), row_x, row_x]
    args = [x["starts"], x["h2"], rows(x["slot_rows"]), rows(x["aff_rows"])]
    out_specs = [pl.BlockSpec((1, 1, cap_x, d), lambda e, i: (sample(i), e, 0, 0))]
    out_shape = [jax.ShapeDtypeStruct((b, n_exp, cap_x, d), BF16)]
    cap_z = 0
    if z is not None:
        seq_z = z["h2"].shape[1]
        cap_z = EC_CAPACITY * seq_z // n_exp
        assert b * cap_z == cap_x
        row_z = pl.BlockSpec((b, 1, 1, seq_z), lambda e, i: (0, e, 0, 0))
        in_specs += [pl.BlockSpec((b, seq_z, d), lambda e, i: (0, 0, 0)), row_z, row_z]
        args += [z["h2"], rows(z["slot_rows"]), rows(z["aff_rows"])]
        out_specs.append(pl.BlockSpec((b, 1, cap_z, d), lambda e, i: (0, e, 0, 0)))
        out_shape.append(jax.ShapeDtypeStruct((b, n_exp, cap_z, d), BF16))
    return pl.pallas_call(
        functools.partial(_expert_kernel, cap_x=cap_x, cap_z=cap_z, window=min(64, cap_x)),
        grid=(n_exp, b + (1 if z is not None else 0)),
        in_specs=in_specs + [wspec((d, ff)), wspec((d, ff)), wspec((ff, d))],
        out_specs=out_specs,
        out_shape=out_shape,
        scratch_shapes=[wbuf((d, ff)), wbuf((d, ff)), wbuf((ff, d)),
                        pltpu.VMEM((cap_x, d), F32), pltpu.VMEM((cap_x, LANES), F32)],
        compiler_params=_params("arbitrary", "arbitrary"),
        name="experts",
    )(*args, w_gate, w_up, w_down)


def _combine_kernel(starts_ref, slot_ref, ye_ref, x_ref, g2_ref, nf_ref, o_ref, *, cap, window, final):
    sample, tile = pl.program_id(0), pl.program_id(1)
    tm = x_ref.shape[1]
    per_pass = MXU_DEPTH // window
    base = tile * LANES + sample * N_EXPERTS
    first, fits = [], None
    for e in range(N_EXPERTS):
        lo, hi = starts_ref[base + e], starts_ref[base + LANES + e]
        start = jnp.minimum(lo // BF16_ROWS * BF16_ROWS, cap - window)
        first.append(start)
        ok = hi - start <= window
        fits = ok if fits is None else fits & ok
    pos = pltpu.roll(slot_ref[...], jnp.where(sample == 0, 0, LANES - sample * N_EXPERTS), 1)

    def finish(acc):
        x2 = x_ref[0] + g2_ref[0] * acc
        if final:
            x2 = x2 * lax.rsqrt(jnp.mean(x2 * x2, axis=-1, keepdims=True) + EPS) * nf_ref[...]
        o_ref[0] = x2

    @pl.when(fits)
    def _():
        lane = lax.broadcasted_iota(jnp.int32, (1, MXU_DEPTH), 1)
        rel = (lane % window).astype(F32)
        acc = jnp.zeros(x_ref.shape[1:], F32)
        for g in range(N_EXPERTS // per_pass):
            experts = range(g * per_pass, (g + 1) * per_pass)
            val = jnp.broadcast_to(pos[:, experts[0]:experts[0] + 1] - first[experts[0]].astype(F32), (tm, MXU_DEPTH))
            for k, e in enumerate(experts[1:], start=1):
                val = jnp.where(lane >= k * window, pos[:, e:e + 1] - first[e].astype(F32), val)
            rows = jnp.concatenate([ye_ref[0, e, pl.ds(pl.multiple_of(first[e], BF16_ROWS), window), :]
                                    for e in experts], axis=0)
            acc = acc + _dot(_indicator(val == rel, BF16), rows)
        finish(acc)

    @pl.when(jnp.logical_not(fits))
    def _():
        slots = lax.broadcasted_iota(jnp.int32, (tm, cap), 1).astype(F32)
        acc = jnp.zeros(x_ref.shape[1:], F32)
        for e in range(N_EXPERTS):
            acc = acc + _dot(_indicator(pos[:, e:e + 1] == slots, BF16), ye_ref[0, e])
        finish(acc)


def _combine(starts, slot_cols, ye, x1, g2, norm_f, cap, final):
    b, l, d = x1.shape
    tm = TOKEN_BLOCK
    window = min(64, cap)
    tok = lambda i, j: (i, j, 0)
    return pl.pallas_call(
        functools.partial(_combine_kernel, cap=cap, window=window, final=final),
        grid=(b, l // tm),
        in_specs=[pl.BlockSpec(memory_space=pltpu.SMEM),
                  pl.BlockSpec((tm, LANES), lambda i, j: (j, 0)),
                  pl.BlockSpec((1, N_EXPERTS, cap, d), lambda i, j: (i, 0, 0, 0)),
                  pl.BlockSpec((1, tm, d), tok), pl.BlockSpec((1, 1, d), lambda i, j: (i, 0, 0)),
                  pl.BlockSpec((1, d), lambda i, j: (0, 0))],
        out_specs=pl.BlockSpec((1, tm, d), tok),
        out_shape=jax.ShapeDtypeStruct((b, l, d), F32),
        compiler_params=_params("arbitrary", "arbitrary"),
        name="combine",
    )(starts, slot_cols, ye, x1, g2, norm_f)


def _token_mixer(x, mod, lw, states, rows, tm, heads):
    sh1, sc1 = mod[0], mod[1]
    qkv, ab, gate, pool = _in_proj(x, sh1, sc1, lw["norm1"], lw["wqkv"], lw["wab"], lw["wgate"], lw["wpool"], tm)
    o, s_f, s_b = _delta_net(qkv, ab, gate, lw["conv_w"], lw["a_log"], lw["dt_bias"], lw["dn_norm"], *states, heads)
    pooled = _pool_mixer(pool, lw["pool_w"], lw["pool_scale"], rows)
    return o, pooled, (s_f, s_b)


def _route_tokens(o, pooled, x, mod, lw, tm):
    seq = x.shape[1]
    x1, h2, aff = _out_proj(o, pooled, x, lw["w_out"], mod[2], lw["norm2"], mod[3], mod[4], lw["w_router"], tm)
    slot_cols, slot_rows, aff_rows, starts = _route(aff, EC_CAPACITY * seq // N_EXPERTS)
    return dict(x1=x1, h2=h2, slot_cols=slot_cols, slot_rows=slot_rows, aff_rows=aff_rows,
                starts=starts.astype(jnp.int32).reshape(-1), g2=mod[5])


def _scatter(r, ye, lw, final):
    cap = EC_CAPACITY * r["x1"].shape[1] // N_EXPERTS
    return _combine(r["starts"], r["slot_cols"], ye, r["x1"], r["g2"], lw["norm_f"], cap, final)


def kernel(x, c, ctx, c_ctx, w_mod, b_mod, norm1, norm2, w_in, conv_w, a_log, dt_bias, dn_norm, pool_w, pool_scale,
           w_out, w_router, w_gate, w_up, w_down, norm_f):
    batch, seq, d = x.shape
    depth = w_mod.shape[0]
    ctx_len = ctx.shape[1]
    dn_width = DN_HEADS * LANES
    qkv_cols = 3 * dn_width
    gate_cols = 2 * N_DIR * DN_HEADS
    state_cols = qkv_cols + gate_cols
    rows = seq // GRID_W
    tm_x, tm_z = 512, ctx_len

    cond_rows = 16
    cond = jnp.zeros((cond_rows, d), F32).at[:batch].set(c).at[batch].set(c_ctx)
    mod_all = _adaln_all(cond, w_mod, b_mod)

    zero_state = jnp.zeros((batch, DN_HEADS, LANES, LANES), F32)
    z = ctx
    for l in range(depth):
        wl = w_in[l]
        lw = dict(
            norm1=norm1[l][None], norm2=norm2[l][None],
            wqkv=wl[:, :qkv_cols].astype(BF16),
            wab=jnp.pad(wl[:, qkv_cols:state_cols], ((0, 0), (0, LANES - gate_cols))).astype(BF16),
            wgate=wl[:, state_cols:state_cols + dn_width].astype(BF16),
            wpool=wl[:, state_cols + dn_width:].astype(BF16),
            conv_w=conv_w[l], a_log=a_log[l], dt_bias=dt_bias[l], dn_norm=dn_norm[l][None],
            pool_w=pool_w[l].astype(BF16), pool_scale=pool_scale[l][None],
            w_out=w_out[l].astype(BF16),
            w_router=jnp.pad(w_router[l], ((0, 0), (0, LANES - N_EXPERTS))).astype(BF16),
            norm_f=norm_f[None],
        )
        mods = mod_all[l].reshape(cond_rows, 6, d)
        mod_x = [mods[:batch, i][:, None, :] for i in range(6)]
        mod_z = [jnp.broadcast_to(mods[batch, i][None, None, :], (batch, 1, d)) for i in range(6)]

        o_z, pooled_z, ctx_states = _token_mixer(z, mod_z, lw, (zero_state, zero_state), None, tm_z, DN_HEADS)
        o_x, pooled_x, _ = _token_mixer(x, mod_x, lw, ctx_states, rows, tm_x, 1)
        rx = _route_tokens(o_x, pooled_x, x, mod_x, lw, tm_x)
        if l < depth - 1:
            rz = _route_tokens(o_z, pooled_z, z, mod_z, lw, tm_z)
            ye_x, ye_z = _experts(w_gate, w_up, w_down, l, rx, rz)
            z = _scatter(rz, ye_z, lw, False)
        else:
            ye_x, = _experts(w_gate, w_up, w_down, l, rx)
        x = _scatter(rx, ye_x, lw, l == depth - 1)
    return x
```

```python
import functools

import jax
import jax.numpy as jnp
from jax import lax
from jax.experimental import pallas as pl
from jax.experimental.pallas import tpu as pltpu

F32 = jnp.float32
BF16 = jnp.bfloat16

LANES = 128
F32_ROWS = 8
BF16_ROWS = 16
GRID_W = 64
DN_HEADS = 4
N_DIR = 2
CHUNK = 64
DN_GROUP = 16
POOL_WINDOWS = (2, 4, 8, 16)
POOL_PAD = GRID_W * max(POOL_WINDOWS) // 4
N_EXPERTS = 16
EC_CAPACITY = 2
EPS = 1e-6
VMEM_LIMIT = 56 * 1024 * 1024

def _params(*semantics):
    return pltpu.CompilerParams(dimension_semantics=semantics, vmem_limit_bytes=VMEM_LIMIT)


def _silu(x):
    return x * jax.nn.sigmoid(x)


def _dot(a, b):
    return jnp.dot(a, b, preferred_element_type=F32)


def _indicator(mask, dtype=F32):
    return jnp.where(mask, 1.0, 0.0).astype(dtype)


def _mod_kernel(cond_ref, w_ref, b_ref, o_ref):
    s = _silu(cond_ref[...])
    s_hi = s.astype(BF16)
    s_lo = (s - s_hi.astype(F32)).astype(BF16)
    w = w_ref[0]
    w_hi = w.astype(BF16)
    w_lo = (w - w_hi.astype(F32)).astype(BF16)
    rows = s.shape[0]
    both = _dot(jnp.concatenate([s_hi, s_lo], axis=0), w_hi)
    o_ref[0] = both[:rows] + both[rows:] + _dot(s_hi, w_lo) + b_ref[0]


def _adaln_all(cond, w_mod, b_mod):
    depth, d, n = w_mod.shape
    rows = cond.shape[0]
    tn = 1536
    return pl.pallas_call(
        _mod_kernel,
        grid=(depth, n // tn),
        in_specs=[pl.BlockSpec((rows, d), lambda l, j: (0, 0)),
                  pl.BlockSpec((1, d, tn), lambda l, j: (l, 0, j)),
                  pl.BlockSpec((1, 1, tn), lambda l, j: (l, 0, j))],
        out_specs=pl.BlockSpec((1, rows, tn), lambda l, j: (l, 0, j)),
        out_shape=jax.ShapeDtypeStruct((depth, rows, n), F32),
        compiler_params=_params("arbitrary", "arbitrary"),
        name="adaln",
    )(cond, w_mod, b_mod.reshape(depth, 1, n))


def _norm_mod(x, nw, shift, scale):
    ms = jnp.mean(x * x, axis=-1, keepdims=True)
    return (x * lax.rsqrt(ms + EPS) * nw) * (1.0 + scale) + shift


def _in_kernel(x_ref, sh_ref, sc_ref, nw_ref, wqkv_ref, wab_ref, wgate_ref, wpool_ref,
               qkv_ref, ab_ref, gate_ref, pool_ref):
    h = _norm_mod(x_ref[0], nw_ref[...], sh_ref[0], sc_ref[0]).astype(BF16)
    qkv = _dot(h, wqkv_ref[...])
    for c in range(qkv_ref.shape[1]):
        qkv_ref[0, c] = qkv[:, c * LANES:(c + 1) * LANES]
    gate = _dot(h, wgate_ref[...])
    for c in range(gate_ref.shape[1]):
        gate_ref[0, c] = gate[:, c * LANES:(c + 1) * LANES]
    ab_ref[0] = _dot(h, wab_ref[...])
    pool_ref[0] = _dot(h, wpool_ref[...])


def _in_proj(x, shift, scale, nw, wqkv, wab, wgate, wpool, tm):
    b, l, d = x.shape
    row = lambda i, j: (i, 0, 0)
    tok = lambda i, j: (i, j, 0)
    fixed = lambda i, j: (0, 0)
    heads = lambda w: w.shape[1] // LANES
    per_head = lambda w: pl.BlockSpec((1, heads(w), tm, LANES), lambda i, j: (i, 0, j, 0))
    return pl.pallas_call(
        _in_kernel,
        grid=(b, l // tm),
        in_specs=[pl.BlockSpec((1, tm, d), tok),
                  pl.BlockSpec((1, 1, d), row), pl.BlockSpec((1, 1, d), row),
                  pl.BlockSpec((1, d), fixed)]
                 + [pl.BlockSpec(w.shape, fixed) for w in (wqkv, wab, wgate, wpool)],
        out_specs=[per_head(wqkv), pl.BlockSpec((1, tm, wab.shape[1]), tok), per_head(wgate),
                   pl.BlockSpec((1, tm, wpool.shape[1]), tok)],
        out_shape=[jax.ShapeDtypeStruct((b, heads(wqkv), l, LANES), F32),
                   jax.ShapeDtypeStruct((b, l, wab.shape[1]), F32),
                   jax.ShapeDtypeStruct((b, heads(wgate), l, LANES), F32),
                   jax.ShapeDtypeStruct((b, l, wpool.shape[1]), F32)],
        compiler_params=_params("arbitrary", "arbitrary"),
        name="in_proj",
    )(x, shift, scale, nw, wqkv, wab, wgate, wpool)


def _bdot(a, b):
    return jnp.einsum('gij,gjk->gik', a, b, preferred_element_type=F32)


def _dn_kernel(alog_ref, dtb_ref, q_ref, k_ref, v_ref, ab_ref, gate_ref, cwq_ref, cwk_ref, cwv_ref,
               dnw_ref, s0f_ref, s0b_ref, o_ref, sf_ref, sb_ref,
               qs, ks, vs, bet_all, la_all, bet, gl, la, kn_f, kn_b, bn_f, bn_b, qn_f, qn_b, on_f, on_b, pad,
               *, group):
    step = pl.program_id(1)
    _, heads, seq, _ = q_ref.shape
    n_chunks = seq // CHUNK
    lane = lax.broadcasted_iota(jnp.int32, (1, LANES), 1)

    edge = jnp.zeros((F32_ROWS, LANES), F32)
    pad[0:F32_ROWS, :] = edge
    pad[pl.ds(F32_ROWS + seq, F32_ROWS), :] = edge

    def conv_silu(x, w):
        pad[pl.ds(F32_ROWS, seq), :] = x
        y = (pad[pl.ds(F32_ROWS - 2, seq), :] * w[0:1] + pad[pl.ds(F32_ROWS - 1, seq), :] * w[1:2]
             + x * w[2:3] + pad[pl.ds(F32_ROWS + 1, seq), :] * w[3:4])
        return _silu(y)

    def l2norm(x):
        return x * lax.rsqrt(jnp.sum(x * x, axis=-1, keepdims=True) + EPS)

    @pl.when(step == 0)
    def _():
        ab = ab_ref[0]
        bet_all[...] = jax.nn.sigmoid(ab)
        la_all[...] = -jnp.exp(alog_ref[...]) * jax.nn.softplus(ab + dtb_ref[...])

    for h in range(heads):
        own = pl.ds(h * seq, seq)
        lanes = slice(h * LANES, (h + 1) * LANES)
        qs[own, :] = l2norm(conv_silu(q_ref[0, h], cwq_ref[:, lanes])) * (LANES ** -0.5)
        ks[own, :] = l2norm(conv_silu(k_ref[0, h], cwk_ref[:, lanes]))
        vs[own, :] = conv_silu(v_ref[0, h], cwv_ref[:, lanes])
        head = step * heads + h
        shift = jnp.where(head == 0, 0, LANES - head)
        bet[own, :] = pltpu.roll(bet_all[...], shift, 1)
        log_a = pltpu.roll(la_all[...], shift, 1)
        hi = log_a.astype(BF16).astype(F32)
        mid = (log_a - hi).astype(BF16).astype(F32)
        low = (log_a - hi - mid).astype(BF16).astype(F32)
        part = lane & (DN_HEADS - 1)
        la[own, :] = jnp.where(part == 0, hi, jnp.where(part == 1, pltpu.roll(mid, 1, 1),
                                                        pltpu.roll(low, 2, 1))).astype(BF16)

    ii = lax.broadcasted_iota(jnp.int32, (CHUNK, CHUNK), 0)
    jj = lax.broadcasted_iota(jnp.int32, (CHUNK, CHUNK), 1)
    sums = jnp.concatenate([_indicator(ii >= jj), _indicator(ii <= jj), jnp.ones((CHUNK, CHUNK), F32)],
                           axis=0).astype(BF16)
    eye = _indicator(ii == jj)
    span = group * CHUNK

    def wy_group(gi, carry):
        rows = pl.ds(pl.multiple_of(gi * span, span), span)
        state_rows = pl.ds(pl.multiple_of(gi * group * LANES, group * LANES), group * LANES)
        chunks = lambda a: a.reshape(group, CHUNK, a.shape[-1])
        q, k, v = chunks(qs[rows, :]), chunks(ks[rows, :]), chunks(vs[rows, :])
        log_a = chunks(la[rows, :])
        parts = jnp.concatenate([_dot(sums, log_a[c]) for c in range(group)], axis=0)
        cums = parts + pltpu.roll(parts, LANES - 1, 1) + pltpu.roll(parts, LANES - 2, 1)
        cums = cums.reshape(group, 3 * CHUNK, LANES)
        gl[rows, :] = cums[:, 2 * CHUNK:].reshape(span, LANES)
        beta_all = chunks(bet[rows, :])
        grams = jnp.einsum('gik,gjk->gij', jnp.concatenate([k, q], axis=1).astype(BF16), k.astype(BF16),
                           preferred_element_type=F32)
        for direction, (kn, bn, qn, on) in enumerate(((kn_f, bn_f, qn_f, on_f), (kn_b, bn_b, qn_b, on_b))):
            col = DN_HEADS * direction
            g = cums[:, direction * CHUNK:(direction + 1) * CHUNK, col:col + 1]
            g_last = cums[:, 2 * CHUNK:, col:col + 1]
            beta = beta_all[:, :, 2 * DN_HEADS + col:2 * DN_HEADS + col + 1]
            if direction == 0:
                incl, strict = ii >= jj, ii > jj
            else:
                incl, strict = ii <= jj, ii < jj
            g_cols = jnp.swapaxes(jnp.broadcast_to(g, (group, CHUNK, LANES)), 1, 2)[:, :CHUNK, :]
            decay = jnp.exp(jnp.minimum(g - g_cols, 0.0))
            e_g = jnp.exp(g)
            k_beta = k * beta
            m = jnp.where(strict, grams[:, :CHUNK] * beta * decay, 0.0)
            qk = jnp.where(incl, grams[:, CHUNK:] * decay, 0.0).astype(BF16)
            p = -m
            t_inv = eye + p
            pb = p.astype(BF16)
            p = _bdot(pb, pb)
            for _ in range(CHUNK.bit_length() - 3):
                pb = p.astype(BF16)
                r = _bdot(jnp.concatenate([pb, t_inv.astype(BF16)], axis=1), pb)
                p, t_inv = r[:, :CHUNK], t_inv + r[:, CHUNK:]
            t_inv = t_inv + _bdot(t_inv.astype(BF16), p.astype(BF16))
            wu = _bdot(t_inv.astype(BF16),
                       jnp.concatenate([k_beta * e_g, v * beta], axis=2).astype(BF16)).astype(BF16)
            k_tail = (k * jnp.exp(g_last - g)).astype(BF16)
            kb = jnp.einsum('gik,gin->gkn', k_tail, wu, preferred_element_type=F32)
            qb = _bdot(qk, wu)
            kn[state_rows, :] = kb[:, :, :LANES].reshape(group * LANES, LANES).astype(BF16)
            bn[state_rows, :] = kb[:, :, LANES:].reshape(group * LANES, LANES)
            qn[rows, :] = (q * e_g - qb[:, :, :LANES]).reshape(span, LANES).astype(BF16)
            on[rows, :] = qb[:, :, LANES:].reshape(span, LANES)
        return carry

    lax.fori_loop(0, heads * n_chunks // group, wy_group, 0)

    def scan_chunk(n, state, direction, kn, bn, qn, on):
        col = DN_HEADS * direction
        rows = pl.ds(pl.multiple_of(n * CHUNK, CHUNK), CHUNK)
        state_rows = pl.ds(pl.multiple_of(n * LANES, LANES), LANES)
        s16 = state.astype(BF16)
        on[rows, :] = on[rows, :] + _dot(qn[rows, :], s16)
        s_decay = jnp.exp(gl[pl.ds(pl.multiple_of(n * CHUNK, CHUNK), 8), :][0:1, col:col + 1])
        return state * s_decay - _dot(kn[state_rows, :], s16) + bn[state_rows, :]

    def scan_step(i, states):
        out = []
        for h in range(heads):
            out.append(scan_chunk(h * n_chunks + i, states[2 * h], 0, kn_f, bn_f, qn_f, on_f))
            out.append(scan_chunk(h * n_chunks + n_chunks - 1 - i, states[2 * h + 1], 1, kn_b, bn_b, qn_b, on_b))
        return tuple(out)

    init = tuple(ref[0, h] for h in range(heads) for ref in (s0f_ref, s0b_ref))
    final = lax.fori_loop(0, n_chunks, scan_step, init)
    for h in range(heads):
        sf_ref[0, h] = final[2 * h]
        sb_ref[0, h] = final[2 * h + 1]
        own = pl.ds(h * seq, seq)
        o = on_f[own, :] + on_b[own, :]
        y = o * lax.rsqrt(jnp.mean(o * o, axis=-1, keepdims=True) + EPS) * dnw_ref[...]
        o_ref[0, h] = (y * _silu(gate_ref[0, h])).astype(BF16)


def _delta_net(qkv, ab, gate, conv_w, a_log, dt_bias, dn_norm, s0f, s0b, heads):
    b, _, seq, hd = qkv.shape
    lane_vec = lambda a: jnp.pad(a.reshape(1, -1), ((0, 0), (0, LANES - a.size)))
    vec = pl.BlockSpec((1, LANES), lambda i, s: (0, 0))
    part = DN_HEADS // heads
    per_head = lambda off: pl.BlockSpec((1, heads, seq, hd), lambda i, s: (i, off * part + s, 0, 0))
    cw = lambda off: pl.BlockSpec((conv_w.shape[0], heads * hd), lambda i, s: (0, off * part + s))
    st = pl.BlockSpec((1, heads, hd, hd), lambda i, s: (i, s, 0, 0))
    rows = heads * seq
    group = min(DN_GROUP, rows // CHUNK)
    shared_f32 = pltpu.VMEM((seq, hd), F32)
    rows_f32 = pltpu.VMEM((rows, hd), F32)
    rows_bf16 = pltpu.VMEM((rows, hd), BF16)
    mats_f32 = pltpu.VMEM((rows // CHUNK * hd, hd), F32)
    mats_bf16 = pltpu.VMEM((rows // CHUNK * hd, hd), BF16)
    return pl.pallas_call(
        functools.partial(_dn_kernel, group=group),
        grid=(b, part),
        in_specs=[vec, vec, per_head(0), per_head(1), per_head(2),
                  pl.BlockSpec((1, seq, LANES), lambda i, s: (i, 0, 0)),
                  per_head(0), cw(0), cw(1), cw(2),
                  pl.BlockSpec((1, hd), lambda i, s: (0, 0)), st, st],
        out_specs=[per_head(0), st, st],
        out_shape=[jax.ShapeDtypeStruct((b, DN_HEADS, seq, hd), BF16),
                   jax.ShapeDtypeStruct((b, DN_HEADS, hd, hd), F32),
                   jax.ShapeDtypeStruct((b, DN_HEADS, hd, hd), F32)],
        scratch_shapes=[rows_f32] * 3 + [shared_f32] * 2 + [rows_f32] * 2 + [rows_bf16]
                       + [mats_bf16] * 2 + [mats_f32] * 2 + [rows_bf16] * 2 + [rows_f32] * 2
                       + [pltpu.VMEM((seq + 2 * F32_ROWS, hd), F32)],
        compiler_params=_params("arbitrary", "arbitrary"),
        name="delta_net",
    )(lane_vec(a_log), lane_vec(dt_bias), qkv, qkv, qkv, ab, gate, conv_w, conv_w, conv_w, dn_norm, s0f, s0b)


def _window_sum(x, pos, limit, half, stride, bufs):
    seq = x.shape[0]
    body = pl.ds(POOL_PAD, seq)
    shifted = lambda ref, off: ref[pl.ds(POOL_PAD + off, seq), :]
    keep = (lambda ok, v: jnp.where(ok, v, 0.0)) if stride == 1 else (lambda ok, v: v)
    a, l, r = bufs
    a[body, :] = x
    left = keep(pos >= 1, shifted(a, -stride))
    right = x
    k = 1
    while k < half:
        l[body, :] = left
        r[body, :] = right
        left = left + keep(pos >= k, shifted(l, -k * stride))
        right = right + keep(pos + k < limit, shifted(r, k * stride))
        k *= 2
    return left + right


def _window_count(pos, limit, half):
    return (jnp.minimum(pos + half, limit) - jnp.maximum(pos - half, 0)).astype(F32)


def _pool_kernel(u_ref, pw_ref, ps_ref, o_ref, *bufs, rows):
    seq = u_ref.shape[1]
    t = lax.broadcasted_iota(jnp.int32, (seq, LANES), 0)
    border = jnp.zeros((POOL_PAD, LANES), F32)
    for buf in bufs:
        buf[0:POOL_PAD, :] = border
        buf[pl.ds(POOL_PAD + seq, POOL_PAD), :] = border
    for g, w in enumerate(POOL_WINDOWS):
        lanes = slice(g * LANES, (g + 1) * LANES)
        x = u_ref[0, :, lanes]
        half = w // 2
        if rows is None:
            total = _window_sum(x, t, seq, half, 1, bufs)
            count = _window_count(t, seq, half)
        else:
            c, r = t & (GRID_W - 1), t >> (GRID_W.bit_length() - 1)
            total = _window_sum(_window_sum(x, c, GRID_W, half, 1, bufs), r, rows, half, GRID_W, bufs)
            count = _window_count(r, rows, half) * _window_count(c, GRID_W, half)
        m = (total / count - x).astype(BF16)
        o_ref[0, :, lanes] = (_dot(m, pw_ref[g]) * ps_ref[:, lanes]).astype(BF16)


def _pool_mixer(u, pool_w, pool_scale, rows):
    b, seq, width = u.shape
    blk = pl.BlockSpec((1, seq, width), lambda i: (i, 0, 0))
    return pl.pallas_call(
        functools.partial(_pool_kernel, rows=rows),
        grid=(b,),
        in_specs=[blk, pl.BlockSpec(pool_w.shape, lambda i: (0, 0, 0)), pl.BlockSpec((1, width), lambda i: (0, 0))],
        out_specs=blk,
        out_shape=jax.ShapeDtypeStruct((b, seq, width), BF16),
        scratch_shapes=[pltpu.VMEM((seq + 2 * POOL_PAD, LANES), F32)] * 3,
        compiler_params=_params("arbitrary"),
        name="pool_mixer",
    )(u, pool_w, pool_scale)


def _out_kernel(o_ref, p_ref, x_ref, wo_ref, g1_ref, nw_ref, sh_ref, sc_ref, wr_ref, x1_ref, h2_ref, aff_ref):
    sample = pl.program_id(1)
    o = jnp.concatenate([o_ref[0, h] for h in range(o_ref.shape[1])], axis=1)
    half = o.shape[1]
    y = _dot(o, wo_ref[:half, :]) + _dot(p_ref[0], wo_ref[half:, :])
    x1 = x_ref[0] + g1_ref[0] * y
    x1_ref[0] = x1
    h2 = _norm_mod(x1, nw_ref[...], sh_ref[0], sc_ref[0]).astype(BF16)
    h2_ref[0] = h2
    lane = lax.broadcasted_iota(jnp.int32, (1, LANES), 1)
    logits = jnp.where(lane < N_EXPERTS, _dot(h2, wr_ref[...]), -jnp.inf)
    ex = jnp.exp(logits - jnp.max(logits, axis=-1, keepdims=True))
    aff = pltpu.roll(ex / jnp.sum(ex, axis=-1, keepdims=True), sample * N_EXPERTS, 1)

    @pl.when(sample == 0)
    def _():
        aff_ref[...] = aff

    @pl.when(sample > 0)
    def _():
        aff_ref[...] += aff


def _out_proj(o, pooled, x, w_out, g1, nw, shift, scale, w_router, tm):
    b, l, d = x.shape
    assert b * N_EXPERTS <= LANES
    half = pooled.shape[2]
    row = lambda j, i: (i, 0, 0)
    tok = lambda j, i: (i, j, 0)
    fixed = lambda j, i: (0, 0)
    vec = pl.BlockSpec((1, 1, d), row)
    return pl.pallas_call(
        _out_kernel,
        grid=(l // tm, b),
        in_specs=[pl.BlockSpec((1, o.shape[1], tm, LANES), lambda j, i: (i, 0, j, 0)),
                  pl.BlockSpec((1, tm, half), tok), pl.BlockSpec((1, tm, d), tok),
                  pl.BlockSpec(w_out.shape, fixed), vec, pl.BlockSpec((1, d), fixed), vec, vec,
                  pl.BlockSpec(w_router.shape, fixed)],
        out_specs=[pl.BlockSpec((1, tm, d), tok), pl.BlockSpec((1, tm, d), tok),
                   pl.BlockSpec((tm, LANES), lambda j, i: (j, 0))],
        out_shape=[jax.ShapeDtypeStruct((b, l, d), F32), jax.ShapeDtypeStruct((b, l, d), BF16),
                   jax.ShapeDtypeStruct((l, LANES), F32)],
        compiler_params=_params("arbitrary", "arbitrary"),
        name="out_proj",
    )(o, pooled, x, w_out, g1, nw, shift, scale, w_router)


TOKEN_BLOCK = 256
MXU_DEPTH = 256


def _route_kernel(aff_ref, slotc_ref, slotr_ref, affr_ref, starts_ref, *, cap):
    seq = aff_ref.shape[0]
    aff = aff_ref[...]

    def bit_step(it, lo_bits):
        cand_bits = lo_bits | jnp.left_shift(jnp.int32(1), 30 - it)
        cand = lax.bitcast_convert_type(cand_bits, F32)
        count = jnp.sum(_indicator(aff >= cand), axis=0, keepdims=True)
        return jnp.where(count >= cap, cand_bits, lo_bits)

    lo_bits = lax.fori_loop(0, 31, bit_step, jnp.zeros((1, LANES), jnp.int32))
    lo = lax.bitcast_convert_type(lo_bits, F32)
    hi = lax.bitcast_convert_type(lo_bits + 1, F32)
    above = aff >= hi
    tied = (aff >= lo) & (aff < hi)
    need = cap - jnp.sum(_indicator(above), axis=0, keepdims=True)
    flags = jnp.concatenate([_indicator(above), _indicator(tied)], axis=1)

    r = lax.broadcasted_iota(jnp.int32, (TOKEN_BLOCK, TOKEN_BLOCK), 0)
    c = lax.broadcasted_iota(jnp.int32, (TOKEN_BLOCK, TOKEN_BLOCK), 1)
    tri = _indicator(r > c, BF16)
    offset = jnp.zeros((1, 2 * LANES), F32)
    pieces = []
    for j in range(seq // TOKEN_BLOCK):
        blk = flags[j * TOKEN_BLOCK:(j + 1) * TOKEN_BLOCK]
        pieces.append(_dot(tri, blk.astype(BF16)) + offset)
        offset = offset + jnp.sum(blk, axis=0, keepdims=True)
    before = jnp.concatenate(pieces, axis=0)
    tied_before = before[:, LANES:]
    rank = before[:, :LANES] + jnp.minimum(tied_before, need)
    chosen = above | (tied & (tied_before < need))
    slot = jnp.where(chosen, rank, -1.0)
    slotc_ref[...] = slot
    slotr_ref[...] = slot.T
    affr_ref[...] = aff.T
    n_blocks = seq // TOKEN_BLOCK
    for j in range(n_blocks):
        starts_ref[j:j + 1, :] = rank[j * TOKEN_BLOCK:j * TOKEN_BLOCK + 1, :]
    starts_ref[n_blocks:, :] = jnp.full((starts_ref.shape[0] - n_blocks, LANES), cap, F32)


def _route(aff, cap):
    seq = aff.shape[0]
    start_rows = -(-(seq // TOKEN_BLOCK + 1) // 8) * 8
    return pl.pallas_call(
        functools.partial(_route_kernel, cap=cap),
        out_shape=[jax.ShapeDtypeStruct((seq, LANES), F32), jax.ShapeDtypeStruct((LANES, seq), F32),
                   jax.ShapeDtypeStruct((LANES, seq), F32), jax.ShapeDtypeStruct((start_rows, LANES), F32)],
        compiler_params=pltpu.CompilerParams(vmem_limit_bytes=VMEM_LIMIT),
        name="route",
    )(aff)


def _affinity_of(hit, aff):
    return jnp.broadcast_to(jnp.sum(jnp.where(hit, aff, 0.0), axis=1, keepdims=True), (hit.shape[0], LANES))


def _gather_full(h_ref, pos_ref, aff_ref, xe_s, w_s, cap):
    samples, seq, _ = h_ref.shape
    slots = lax.broadcasted_iota(jnp.int32, (cap, seq), 0).astype(F32)
    for s in range(samples):
        hit = slots == pos_ref[s, 0]
        xe_s[s * cap:(s + 1) * cap, :] = _dot(_indicator(hit, BF16), h_ref[s])
        w_s[s * cap:(s + 1) * cap, :] = _affinity_of(hit, aff_ref[s, 0])


def _gather_blocks(starts_ref, base, h_ref, pos_ref, aff_ref, xe_s, w_s, cap, window):
    seq = h_ref.shape[1]
    n_blocks = seq // TOKEN_BLOCK
    if n_blocks == 1 or window >= cap:
        _gather_full(h_ref, pos_ref, aff_ref, xe_s, w_s, cap)
        return
    first, fits = [], None
    for j in range(n_blocks):
        lo, hi = starts_ref[j * LANES + base], starts_ref[(j + 1) * LANES + base]
        start = jnp.minimum(lo // F32_ROWS * F32_ROWS, cap - window)
        first.append(start)
        ok = hi - start <= window
        fits = ok if fits is None else fits & ok

    @pl.when(fits)
    def _():
        xe_s[...] = jnp.zeros(xe_s.shape, F32)
        w_s[...] = jnp.zeros(w_s.shape, F32)
        rel = lax.broadcasted_iota(jnp.int32, (window, TOKEN_BLOCK), 0).astype(F32)
        for j in range(n_blocks):
            tokens = slice(j * TOKEN_BLOCK, (j + 1) * TOKEN_BLOCK)
            hit = rel + first[j].astype(F32) == pos_ref[0, 0][:, tokens]
            rows = pl.ds(pl.multiple_of(first[j], F32_ROWS), window)
            xe_s[rows, :] += _dot(_indicator(hit, BF16), h_ref[0, tokens, :])
            w_s[rows, :] += _affinity_of(hit, aff_ref[0, 0][:, tokens])

    @pl.when(jnp.logical_not(fits))
    def _():
        _gather_full(h_ref, pos_ref, aff_ref, xe_s, w_s, cap)


def _expert_kernel(starts_ref, hx_ref, px_ref, ax_ref, *refs, cap_x, cap_z, window):
    if cap_z:
        hz_ref, pz_ref, az_ref, wg_ref, wu_ref, wd_ref, yx_ref, yz_ref, wg_s, wu_s, wd_s, xe_s, w_s = refs
    else:
        wg_ref, wu_ref, wd_ref, yx_ref, wg_s, wu_s, wd_s, xe_s, w_s = refs
    expert, step = pl.program_id(0), pl.program_id(1)
    n_latent = pl.num_programs(1) - (1 if cap_z else 0)

    @pl.when(step == 0)
    def _():
        wg_s[...] = wg_ref[0, 0].astype(BF16)
        wu_s[...] = wu_ref[0, 0].astype(BF16)
        wd_s[...] = wd_ref[0, 0].astype(BF16)

    @pl.when(step < n_latent)
    def _():
        _gather_blocks(starts_ref, step * N_EXPERTS + expert, hx_ref, px_ref, ax_ref, xe_s, w_s, cap_x, window)

    if cap_z:
        @pl.when(step == n_latent)
        def _():
            _gather_full(hz_ref, pz_ref, az_ref, xe_s, w_s, cap_z)

    xe = xe_s[...].astype(BF16)
    hid = (_silu(_dot(xe, wg_s[...])) * _dot(xe, wu_s[...])).astype(BF16)
    ye = (_dot(hid, wd_s[...]) * w_s[:, 0:1]).astype(BF16)

    @pl.when(step < n_latent)
    def _():
        yx_ref[0, 0] = ye

    if cap_z:
        @pl.when(step == n_latent)
        def _():
            for s in range(yz_ref.shape[0]):
                yz_ref[s, 0] = ye[s * cap_z:(s + 1) * cap_z]


def _experts(w_gate, w_up, w_down, layer, x, z=None):
    b, seq_x, d = x["h2"].shape
    _, n_exp, _, ff = w_gate.shape
    cap_x = EC_CAPACITY * seq_x // n_exp
    lanes = b * n_exp
    rows = lambda a: a[:lanes].reshape(b, n_exp, 1, a.shape[1])
    sample = lambda i: jnp.minimum(i, b - 1)
    wspec = lambda shape: pl.BlockSpec((1, 1) + shape, lambda e, i: (layer, e, 0, 0))
    wbuf = lambda shape: pltpu.VMEM(shape, BF16)
    row_x = pl.BlockSpec((1, 1, 1, seq_x), lambda e, i: (sample(i), e, 0, 0))
    in_specs = [pl.BlockSpec(memory_space=pltpu.SMEM),
                pl.BlockSpec((1, seq_x, d), lambda e, i: (sample(i), 0, 0)), row_x, row_x]
    args = [x["starts"], x["h2"], rows(x["slot_rows"]), rows(x["aff_rows"])]
    out_specs = [pl.BlockSpec((1, 1, cap_x, d), lambda e, i: (sample(i), e, 0, 0))]
    out_shape = [jax.ShapeDtypeStruct((b, n_exp, cap_x, d), BF16)]
    cap_z = 0
    if z is not None:
        seq_z = z["h2"].shape[1]
        cap_z = EC_CAPACITY * seq_z // n_exp
        assert b * cap_z == cap_x
        row_z = pl.BlockSpec((b, 1, 1, seq_z), lambda e, i: (0, e, 0, 0))
        in_specs += [pl.BlockSpec((b, seq_z, d), lambda e, i: (0, 0, 0)), row_z, row_z]
        args += [z["h2"], rows(z["slot_rows"]), rows(z["aff_rows"])]
        out_specs.append(pl.BlockSpec((b, 1, cap_z, d), lambda e, i: (0, e, 0, 0)))
        out_shape.append(jax.ShapeDtypeStruct((b, n_exp, cap_z, d), BF16))
    return pl.pallas_call(
        functools.partial(_expert_kernel, cap_x=cap_x, cap_z=cap_z, window=min(64, cap_x)),
        grid=(n_exp, b + (1 if z is not None else 0)),
        in_specs=in_specs + [wspec((d, ff)), wspec((d, ff)), wspec((ff, d))],
        out_specs=out_specs,
        out_shape=out_shape,
        scratch_shapes=[wbuf((d, ff)), wbuf((d, ff)), wbuf((ff, d)),
                        pltpu.VMEM((cap_x, d), F32), pltpu.VMEM((cap_x, LANES), F32)],
        compiler_params=_params("arbitrary", "arbitrary"),
        name="experts",
    )(*args, w_gate, w_up, w_down)


def _combine_kernel(starts_ref, slot_ref, ye_ref, x_ref, g2_ref, nf_ref, o_ref, *, cap, window, final):
    sample, tile = pl.program_id(0), pl.program_id(1)
    tm = x_ref.shape[1]
    per_pass = MXU_DEPTH // window
    base = tile * LANES + sample * N_EXPERTS
    first, fits = [], None
    for e in range(N_EXPERTS):
        lo, hi = starts_ref[base + e], starts_ref[base + LANES + e]
        start = jnp.minimum(lo // BF16_ROWS * BF16_ROWS, cap - window)
        first.append(start)
        ok = hi - start <= window
        fits = ok if fits is None else fits & ok
    pos = pltpu.roll(slot_ref[...], jnp.where(sample == 0, 0, LANES - sample * N_EXPERTS), 1)

    def finish(acc):
        x2 = x_ref[0] + g2_ref[0] * acc
        if final:
            x2 = x2 * lax.rsqrt(jnp.mean(x2 * x2, axis=-1, keepdims=True) + EPS) * nf_ref[...]
        o_ref[0] = x2

    @pl.when(fits)
    def _():
        lane = lax.broadcasted_iota(jnp.int32, (1, MXU_DEPTH), 1)
        rel = (lane % window).astype(F32)
        acc = jnp.zeros(x_ref.shape[1:], F32)
        for g in range(N_EXPERTS // per_pass):
            experts = range(g * per_pass, (g + 1) * per_pass)
            val = jnp.broadcast_to(pos[:, experts[0]:experts[0] + 1] - first[experts[0]].astype(F32), (tm, MXU_DEPTH))
            for k, e in enumerate(experts[1:], start=1):
                val = jnp.where(lane >= k * window, pos[:, e:e + 1] - first[e].astype(F32), val)
            rows = jnp.concatenate([ye_ref[0, e, pl.ds(pl.multiple_of(first[e], BF16_ROWS), window), :]
                                    for e in experts], axis=0)
            acc = acc + _dot(_indicator(val == rel, BF16), rows)
        finish(acc)

    @pl.when(jnp.logical_not(fits))
    def _():
        slots = lax.broadcasted_iota(jnp.int32, (tm, cap), 1).astype(F32)
        acc = jnp.zeros(x_ref.shape[1:], F32)
        for e in range(N_EXPERTS):
            acc = acc + _dot(_indicator(pos[:, e:e + 1] == slots, BF16), ye_ref[0, e])
        finish(acc)


def _combine(starts, slot_cols, ye, x1, g2, norm_f, cap, final):
    b, l, d = x1.shape
    tm = TOKEN_BLOCK
    window = min(64, cap)
    tok = lambda i, j: (i, j, 0)
    return pl.pallas_call(
        functools.partial(_combine_kernel, cap=cap, window=window, final=final),
        grid=(b, l // tm),
        in_specs=[pl.BlockSpec(memory_space=pltpu.SMEM),
                  pl.BlockSpec((tm, LANES), lambda i, j: (j, 0)),
                  pl.BlockSpec((1, N_EXPERTS, cap, d), lambda i, j: (i, 0, 0, 0)),
                  pl.BlockSpec((1, tm, d), tok), pl.BlockSpec((1, 1, d), lambda i, j: (i, 0, 0)),
                  pl.BlockSpec((1, d), lambda i, j: (0, 0))],
        out_specs=pl.BlockSpec((1, tm, d), tok),
        out_shape=jax.ShapeDtypeStruct((b, l, d), F32),
        compiler_params=_params("arbitrary", "arbitrary"),
        name="combine",
    )(starts, slot_cols, ye, x1, g2, norm_f)


def _token_mixer(x, mod, lw, states, rows, tm, heads):
    sh1, sc1 = mod[0], mod[1]
    qkv, ab, gate, pool = _in_proj(x, sh1, sc1, lw["norm1"], lw["wqkv"], lw["wab"], lw["wgate"], lw["wpool"], tm)
    o, s_f, s_b = _delta_net(qkv, ab, gate, lw["conv_w"], lw["a_log"], lw["dt_bias"], lw["dn_norm"], *states, heads)
    pooled = _pool_mixer(pool, lw["pool_w"], lw["pool_scale"], rows)
    return o, pooled, (s_f, s_b)


def _route_tokens(o, pooled, x, mod, lw, tm):
    seq = x.shape[1]
    x1, h2, aff = _out_proj(o, pooled, x, lw["w_out"], mod[2], lw["norm2"], mod[3], mod[4], lw["w_router"], tm)
    slot_cols, slot_rows, aff_rows, starts = _route(aff, EC_CAPACITY * seq // N_EXPERTS)
    return dict(x1=x1, h2=h2, slot_cols=slot_cols, slot_rows=slot_rows, aff_rows=aff_rows,
                starts=starts.astype(jnp.int32).reshape(-1), g2=mod[5])


def _scatter(r, ye, lw, final):
    cap = EC_CAPACITY * r["x1"].shape[1] // N_EXPERTS
    return _combine(r["starts"], r["slot_cols"], ye, r["x1"], r["g2"], lw["norm_f"], cap, final)


def kernel(x, c, ctx, c_ctx, w_mod, b_mod, norm1, norm2, w_in, conv_w, a_log, dt_bias, dn_norm, pool_w, pool_scale,
           w_out, w_router, w_gate, w_up, w_down, norm_f):
    batch, seq, d = x.shape
    depth = w_mod.shape[0]
    ctx_len = ctx.shape[1]
    dn_width = DN_HEADS * LANES
    qkv_cols = 3 * dn_width
    gate_cols = 2 * N_DIR * DN_HEADS
    state_cols = qkv_cols + gate_cols
    rows = seq // GRID_W
    tm_x, tm_z = 512, ctx_len

    cond_rows = 16
    cond = jnp.zeros((cond_rows, d), F32).at[:batch].set(c).at[batch].set(c_ctx)
    mod_all = _adaln_all(cond, w_mod, b_mod)

    zero_state = jnp.zeros((batch, DN_HEADS, LANES, LANES), F32)
    z = ctx
    for l in range(depth):
        wl = w_in[l]
        lw = dict(
            norm1=norm1[l][None], norm2=norm2[l][None],
            wqkv=wl[:, :qkv_cols].astype(BF16),
            wab=jnp.pad(wl[:, qkv_cols:state_cols], ((0, 0), (0, LANES - gate_cols))).astype(BF16),
            wgate=wl[:, state_cols:state_cols + dn_width].astype(BF16),
            wpool=wl[:, state_cols + dn_width:].astype(BF16),
            conv_w=conv_w[l], a_log=a_log[l], dt_bias=dt_bias[l], dn_norm=dn_norm[l][None],
            pool_w=pool_w[l].astype(BF16), pool_scale=pool_scale[l][None],
            w_out=w_out[l].astype(BF16),
            w_router=jnp.pad(w_router[l], ((0, 0), (0, LANES - N_EXPERTS))).astype(BF16),
            norm_f=norm_f[None],
        )
        mods = mod_all[l].reshape(cond_rows, 6, d)
        mod_x = [mods[:batch, i][:, None, :] for i in range(6)]
        mod_z = [jnp.broadcast_to(mods[batch, i][None, None, :], (batch, 1, d)) for i in range(6)]

        o_z, pooled_z, ctx_states = _token_mixer(z, mod_z, lw, (zero_state, zero_state), None, tm_z, DN_HEADS)
        o_x, pooled_x, _ = _token_mixer(x, mod_x, lw, ctx_states, rows, tm_x, 1)
        rx = _route_tokens(o_x, pooled_x, x, mod_x, lw, tm_x)
        if l < depth - 1:
            rz = _route_tokens(o_z, pooled_z, z, mod_z, lw, tm_z)
            ye_x, ye_z = _experts(w_gate, w_up, w_down, l, rx, rz)
            z = _scatter(rz, ye_z, lw, False)
        else:
            ye_x, = _experts(w_gate, w_up, w_down, l, rx)
        x = _scatter(rx, ye_x, lw, l == depth - 1)
    return x
```

```python
import functools

import jax
import jax.numpy as jnp
from jax import lax
from jax.experimental import pallas as pl
from jax.experimental.pallas import tpu as pltpu

F32 = jnp.float32
BF16 = jnp.bfloat16

LANES = 128
F32_ROWS = 8
BF16_ROWS = 16
GRID_W = 64
DN_HEADS = 4
N_DIR = 2
CHUNK = 64
DN_GROUP = 16
POOL_WINDOWS = (2, 4, 8, 16)
POOL_PAD = GRID_W * max(POOL_WINDOWS) // 4
N_EXPERTS = 16
EC_CAPACITY = 2
EPS = 1e-6
VMEM_LIMIT = 56 * 1024 * 1024

def _params(*semantics):
    return pltpu.CompilerParams(dimension_semantics=semantics, vmem_limit_bytes=VMEM_LIMIT)


def _silu(x):
    return x * jax.nn.sigmoid(x)


def _dot(a, b):
    return jnp.dot(a, b, preferred_element_type=F32)


def _indicator(mask, dtype=F32):
    return jnp.where(mask, 1.0, 0.0).astype(dtype)


def _mod_kernel(cond_ref, w_ref, b_ref, o_ref):
    s = _silu(cond_ref[...])
    s_hi = s.astype(BF16)
    s_lo = (s - s_hi.astype(F32)).astype(BF16)
    w = w_ref[0]
    w_hi = w.astype(BF16)
    w_lo = (w - w_hi.astype(F32)).astype(BF16)
    rows = s.shape[0]
    both = _dot(jnp.concatenate([s_hi, s_lo], axis=0), w_hi)
    o_ref[0] = both[:rows] + both[rows:] + _dot(s_hi, w_lo) + b_ref[0]


def _adaln_all(cond, w_mod, b_mod):
    depth, d, n = w_mod.shape
    rows = cond.shape[0]
    tn = 1536
    return pl.pallas_call(
        _mod_kernel,
        grid=(depth, n // tn),
        in_specs=[pl.BlockSpec((rows, d), lambda l, j: (0, 0)),
                  pl.BlockSpec((1, d, tn), lambda l, j: (l, 0, j)),
                  pl.BlockSpec((1, 1, tn), lambda l, j: (l, 0, j))],
        out_specs=pl.BlockSpec((1, rows, tn), lambda l, j: (l, 0, j)),
        out_shape=jax.ShapeDtypeStruct((depth, rows, n), F32),
        compiler_params=_params("arbitrary", "arbitrary"),
        name="adaln",
    )(cond, w_mod, b_mod.reshape(depth, 1, n))


def _norm_mod(x, nw, shift, scale):
    ms = jnp.mean(x * x, axis=-1, keepdims=True)
    return (x * lax.rsqrt(ms + EPS) * nw) * (1.0 + scale) + shift


def _in_kernel(x_ref, sh_ref, sc_ref, nw_ref, wqkv_ref, wab_ref, wgate_ref, wpool_ref,
               qkv_ref, ab_ref, gate_ref, pool_ref):
    h = _norm_mod(x_ref[0], nw_ref[...], sh_ref[0], sc_ref[0]).astype(BF16)
    qkv = _dot(h, wqkv_ref[...])
    for c in range(qkv_ref.shape[1]):
        qkv_ref[0, c] = qkv[:, c * LANES:(c + 1) * LANES]
    gate = _dot(h, wgate_ref[...])
    for c in range(gate_ref.shape[1]):
        gate_ref[0, c] = gate[:, c * LANES:(c + 1) * LANES]
    ab_ref[0] = _dot(h, wab_ref[...])
    pool_ref[0] = _dot(h, wpool_ref[...])


def _in_proj(x, shift, scale, nw, wqkv, wab, wgate, wpool, tm):
    b, l, d = x.shape
    row = lambda i, j: (i, 0, 0)
    tok = lambda i, j: (i, j, 0)
    fixed = lambda i, j: (0, 0)
    heads = lambda w: w.shape[1] // LANES
    per_head = lambda w: pl.BlockSpec((1, heads(w), tm, LANES), lambda i, j: (i, 0, j, 0))
    return pl.pallas_call(
        _in_kernel,
        grid=(b, l // tm),
        in_specs=[pl.BlockSpec((1, tm, d), tok),
                  pl.BlockSpec((1, 1, d), row), pl.BlockSpec((1, 1, d), row),
                  pl.BlockSpec((1, d), fixed)]
                 + [pl.BlockSpec(w.shape, fixed) for w in (wqkv, wab, wgate, wpool)],
        out_specs=[per_head(wqkv), pl.BlockSpec((1, tm, wab.shape[1]), tok), per_head(wgate),
                   pl.BlockSpec((1, tm, wpool.shape[1]), tok)],
        out_shape=[jax.ShapeDtypeStruct((b, heads(wqkv), l, LANES), F32),
                   jax.ShapeDtypeStruct((b, l, wab.shape[1]), F32),
                   jax.ShapeDtypeStruct((b, heads(wgate), l, LANES), F32),
                   jax.ShapeDtypeStruct((b, l, wpool.shape[1]), F32)],
        compiler_params=_params("arbitrary", "arbitrary"),
        name="in_proj",
    )(x, shift, scale, nw, wqkv, wab, wgate, wpool)


def _bdot(a, b):
    return jnp.einsum('gij,gjk->gik', a, b, preferred_element_type=F32)


def _dn_kernel(alog_ref, dtb_ref, q_ref, k_ref, v_ref, ab_ref, gate_ref, cwq_ref, cwk_ref, cwv_ref,
               dnw_ref, s0f_ref, s0b_ref, o_ref, sf_ref, sb_ref,
               qs, ks, vs, bet_all, la_all, bet, gl, la, kn_f, kn_b, bn_f, bn_b, qn_f, qn_b, on_f, on_b, pad,
               *, group):
    step = pl.program_id(1)
    _, heads, seq, _ = q_ref.shape
    n_chunks = seq // CHUNK
    lane = lax.broadcasted_iota(jnp.int32, (1, LANES), 1)

    edge = jnp.zeros((F32_ROWS, LANES), F32)
    pad[0:F32_ROWS, :] = edge
    pad[pl.ds(F32_ROWS + seq, F32_ROWS), :] = edge

    def conv_silu(x, w):
        pad[pl.ds(F32_ROWS, seq), :] = x
        y = (pad[pl.ds(F32_ROWS - 2, seq), :] * w[0:1] + pad[pl.ds(F32_ROWS - 1, seq), :] * w[1:2]
             + x * w[2:3] + pad[pl.ds(F32_ROWS + 1, seq), :] * w[3:4])
        return _silu(y)

    def l2norm(x):
        return x * lax.rsqrt(jnp.sum(x * x, axis=-1, keepdims=True) + EPS)

    @pl.when(step == 0)
    def _():
        ab = ab_ref[0]
        bet_all[...] = jax.nn.sigmoid(ab)
        la_all[...] = -jnp.exp(alog_ref[...]) * jax.nn.softplus(ab + dtb_ref[...])

    for h in range(heads):
        own = pl.ds(h * seq, seq)
        lanes = slice(h * LANES, (h + 1) * LANES)
        qs[own, :] = l2norm(conv_silu(q_ref[0, h], cwq_ref[:, lanes])) * (LANES ** -0.5)
        ks[own, :] = l2norm(conv_silu(k_ref[0, h], cwk_ref[:, lanes]))
        vs[own, :] = conv_silu(v_ref[0, h], cwv_ref[:, lanes])
        head = step * heads + h
        shift = jnp.where(head == 0, 0, LANES - head)
        bet[own, :] = pltpu.roll(bet_all[...], shift, 1)
        log_a = pltpu.roll(la_all[...], shift, 1)
        hi = log_a.astype(BF16).astype(F32)
        mid = (log_a - hi).astype(BF16).astype(F32)
        low = (log_a - hi - mid).astype(BF16).astype(F32)
        part = lane & (DN_HEADS - 1)
        la[own, :] = jnp.where(part == 0, hi, jnp.where(part == 1, pltpu.roll(mid, 1, 1),
                                                        pltpu.roll(low, 2, 1))).astype(BF16)

    ii = lax.broadcasted_iota(jnp.int32, (CHUNK, CHUNK), 0)
    jj = lax.broadcasted_iota(jnp.int32, (CHUNK, CHUNK), 1)
    sums = jnp.concatenate([_indicator(ii >= jj), _indicator(ii <= jj), jnp.ones((CHUNK, CHUNK), F32)],
                           axis=0).astype(BF16)
    eye = _indicator(ii == jj)
    span = group * CHUNK

    def wy_group(gi, carry):
        rows = pl.ds(pl.multiple_of(gi * span, span), span)
        state_rows = pl.ds(pl.multiple_of(gi * group * LANES, group * LANES), group * LANES)
        chunks = lambda a: a.reshape(group, CHUNK, a.shape[-1])
        q, k, v = chunks(qs[rows, :]), chunks(ks[rows, :]), chunks(vs[rows, :])
        log_a = chunks(la[rows, :])
        parts = jnp.concatenate([_dot(sums, log_a[c]) for c in range(group)], axis=0)
        cums = parts + pltpu.roll(parts, LANES - 1, 1) + pltpu.roll(parts, LANES - 2, 1)
        cums = cums.reshape(group, 3 * CHUNK, LANES)
        gl[rows, :] = cums[:, 2 * CHUNK:].reshape(span, LANES)
        beta_all = chunks(bet[rows, :])
        k16 = k.astype(BF16)
        for direction, (kn, bn, qn, on) in enumerate(((kn_f, bn_f, qn_f, on_f), (kn_b, bn_b, qn_b, on_b))):
            col = DN_HEADS * direction
            g = cums[:, direction * CHUNK:(direction + 1) * CHUNK, col:col + 1]
            g_last = cums[:, 2 * CHUNK:, col:col + 1]
            beta = beta_all[:, :, 2 * DN_HEADS + col:2 * DN_HEADS + col + 1]
            if direction == 0:
                incl, strict = ii >= jj, ii > jj
            else:
                incl, strict = ii <= jj, ii < jj
            g_cols = jnp.swapaxes(jnp.broadcast_to(g, (group, CHUNK, LANES)), 1, 2)[:, :CHUNK, :]
            decay = jnp.exp(jnp.minimum(g - g_cols, 0.0))
            e_g = jnp.exp(g)
            k_beta = k * beta
            both = jnp.einsum('gik,gjk->gij', jnp.concatenate([k_beta, q], axis=1).astype(BF16), k16,
                              preferred_element_type=F32)
            m = jnp.where(strict, both[:, :CHUNK] * decay, 0.0)
            qk = jnp.where(incl, both[:, CHUNK:] * decay, 0.0).astype(BF16)
            p = -m
            t_inv = eye + p
            pb = p.astype(BF16)
            p = _bdot(pb, pb)
            for _ in range(CHUNK.bit_length() - 3):
                pb = p.astype(BF16)
                r = _bdot(jnp.concatenate([pb, t_inv.astype(BF16)], axis=1), pb)
                p, t_inv = r[:, :CHUNK], t_inv + r[:, CHUNK:]
            t_inv = t_inv + _bdot(t_inv.astype(BF16), p.astype(BF16))
            wu = _bdot(t_inv.astype(BF16),
                       jnp.concatenate([k_beta * e_g, v * beta], axis=2).astype(BF16)).astype(BF16)
            k_tail = (k * jnp.exp(g_last - g)).astype(BF16)
            kb = jnp.einsum('gik,gin->gkn', k_tail, wu, preferred_element_type=F32)
            qb = _bdot(qk, wu)
            kn[state_rows, :] = kb[:, :, :LANES].reshape(group * LANES, LANES).astype(BF16)
            bn[state_rows, :] = kb[:, :, LANES:].reshape(group * LANES, LANES)
            qn[rows, :] = (q * e_g - qb[:, :, :LANES]).reshape(span, LANES).astype(BF16)
            on[rows, :] = qb[:, :, LANES:].reshape(span, LANES)
        return carry

    lax.fori_loop(0, heads * n_chunks // group, wy_group, 0)

    def scan_chunk(n, state, direction, kn, bn, qn, on):
        col = DN_HEADS * direction
        rows = pl.ds(pl.multiple_of(n * CHUNK, CHUNK), CHUNK)
        state_rows = pl.ds(pl.multiple_of(n * LANES, LANES), LANES)
        s16 = state.astype(BF16)
        on[rows, :] = on[rows, :] + _dot(qn[rows, :], s16)
        s_decay = jnp.exp(gl[pl.ds(pl.multiple_of(n * CHUNK, CHUNK), 8), :][0:1, col:col + 1])
        return state * s_decay - _dot(kn[state_rows, :], s16) + bn[state_rows, :]

    def scan_step(i, states):
        out = []
        for h in range(heads):
            out.append(scan_chunk(h * n_chunks + i, states[2 * h], 0, kn_f, bn_f, qn_f, on_f))
            out.append(scan_chunk(h * n_chunks + n_chunks - 1 - i, states[2 * h + 1], 1, kn_b, bn_b, qn_b, on_b))
        return tuple(out)

    init = tuple(ref[0, h] for h in range(heads) for ref in (s0f_ref, s0b_ref))
    final = lax.fori_loop(0, n_chunks, scan_step, init)
    for h in range(heads):
        sf_ref[0, h] = final[2 * h]
        sb_ref[0, h] = final[2 * h + 1]
        own = pl.ds(h * seq, seq)
        o = on_f[own, :] + on_b[own, :]
        y = o * lax.rsqrt(jnp.mean(o * o, axis=-1, keepdims=True) + EPS) * dnw_ref[...]
        o_ref[0, h] = (y * _silu(gate_ref[0, h])).astype(BF16)


def _delta_net(qkv, ab, gate, conv_w, a_log, dt_bias, dn_norm, s0f, s0b, heads):
    b, _, seq, hd = qkv.shape
    lane_vec = lambda a: jnp.pad(a.reshape(1, -1), ((0, 0), (0, LANES - a.size)))
    vec = pl.BlockSpec((1, LANES), lambda i, s: (0, 0))
    part = DN_HEADS // heads
    per_head = lambda off: pl.BlockSpec((1, heads, seq, hd), lambda i, s: (i, off * part + s, 0, 0))
    cw = lambda off: pl.BlockSpec((conv_w.shape[0], heads * hd), lambda i, s: (0, off * part + s))
    st = pl.BlockSpec((1, heads, hd, hd), lambda i, s: (i, s, 0, 0))
    rows = heads * seq
    group = min(DN_GROUP, rows // CHUNK)
    shared_f32 = pltpu.VMEM((seq, hd), F32)
    rows_f32 = pltpu.VMEM((rows, hd), F32)
    rows_bf16 = pltpu.VMEM((rows, hd), BF16)
    mats_f32 = pltpu.VMEM((rows // CHUNK * hd, hd), F32)
    mats_bf16 = pltpu.VMEM((rows // CHUNK * hd, hd), BF16)
    return pl.pallas_call(
        functools.partial(_dn_kernel, group=group),
        grid=(b, part),
        in_specs=[vec, vec, per_head(0), per_head(1), per_head(2),
                  pl.BlockSpec((1, seq, LANES), lambda i, s: (i, 0, 0)),
                  per_head(0), cw(0), cw(1), cw(2),
                  pl.BlockSpec((1, hd), lambda i, s: (0, 0)), st, st],
        out_specs=[per_head(0), st, st],
        out_shape=[jax.ShapeDtypeStruct((b, DN_HEADS, seq, hd), BF16),
                   jax.ShapeDtypeStruct((b, DN_HEADS, hd, hd), F32),
                   jax.ShapeDtypeStruct((b, DN_HEADS, hd, hd), F32)],
        scratch_shapes=[rows_f32] * 3 + [shared_f32] * 2 + [rows_f32] * 2 + [rows_bf16]
                       + [mats_bf16] * 2 + [mats_f32] * 2 + [rows_bf16] * 2 + [rows_f32] * 2
                       + [pltpu.VMEM((seq + 2 * F32_ROWS, hd), F32)],
        compiler_params=_params("arbitrary", "arbitrary"),
        name="delta_net",
    )(lane_vec(a_log), lane_vec(dt_bias), qkv, qkv, qkv, ab, gate, conv_w, conv_w, conv_w, dn_norm, s0f, s0b)


def _window_sum(x, pos, limit, half, stride, bufs):
    seq = x.shape[0]
    body = pl.ds(POOL_PAD, seq)
    shifted = lambda ref, off: ref[pl.ds(POOL_PAD + off, seq), :]
    keep = (lambda ok, v: jnp.where(ok, v, 0.0)) if stride == 1 else (lambda ok, v: v)
    a, l, r = bufs
    a[body, :] = x
    left = keep(pos >= 1, shifted(a, -stride))
    right = x
    k = 1
    while k < half:
        l[body, :] = left
        r[body, :] = right
        left = left + keep(pos >= k, shifted(l, -k * stride))
        right = right + keep(pos + k < limit, shifted(r, k * stride))
        k *= 2
    return left + right


def _window_count(pos, limit, half):
    return (jnp.minimum(pos + half, limit) - jnp.maximum(pos - half, 0)).astype(F32)


def _pool_kernel(u_ref, pw_ref, ps_ref, o_ref, *bufs, rows):
    seq = u_ref.shape[1]
    t = lax.broadcasted_iota(jnp.int32, (seq, LANES), 0)
    border = jnp.zeros((POOL_PAD, LANES), F32)
    for buf in bufs:
        buf[0:POOL_PAD, :] = border
        buf[pl.ds(POOL_PAD + seq, POOL_PAD), :] = border
    for g, w in enumerate(POOL_WINDOWS):
        lanes = slice(g * LANES, (g + 1) * LANES)
        x = u_ref[0, :, lanes]
        half = w // 2
        if rows is None:
            total = _window_sum(x, t, seq, half, 1, bufs)
            count = _window_count(t, seq, half)
        else:
            c, r = t & (GRID_W - 1), t >> (GRID_W.bit_length() - 1)
            total = _window_sum(_window_sum(x, c, GRID_W, half, 1, bufs), r, rows, half, GRID_W, bufs)
            count = _window_count(r, rows, half) * _window_count(c, GRID_W, half)
        m = (total / count - x).astype(BF16)
        o_ref[0, :, lanes] = (_dot(m, pw_ref[g]) * ps_ref[:, lanes]).astype(BF16)


def _pool_mixer(u, pool_w, pool_scale, rows):
    b, seq, width = u.shape
    blk = pl.BlockSpec((1, seq, width), lambda i: (i, 0, 0))
    return pl.pallas_call(
        functools.partial(_pool_kernel, rows=rows),
        grid=(b,),
        in_specs=[blk, pl.BlockSpec(pool_w.shape, lambda i: (0, 0, 0)), pl.BlockSpec((1, width), lambda i: (0, 0))],
        out_specs=blk,
        out_shape=jax.ShapeDtypeStruct((b, seq, width), BF16),
        scratch_shapes=[pltpu.VMEM((seq + 2 * POOL_PAD, LANES), F32)] * 3,
        compiler_params=_params("arbitrary"),
        name="pool_mixer",
    )(u, pool_w, pool_scale)


def _out_kernel(o_ref, p_ref, x_ref, wo_ref, g1_ref, nw_ref, sh_ref, sc_ref, wr_ref, x1_ref, h2_ref, aff_ref):
    sample = pl.program_id(1)
    o = jnp.concatenate([o_ref[0, h] for h in range(o_ref.shape[1])], axis=1)
    half = o.shape[1]
    y = _dot(o, wo_ref[:half, :]) + _dot(p_ref[0], wo_ref[half:, :])
    x1 = x_ref[0] + g1_ref[0] * y
    x1_ref[0] = x1
    h2 = _norm_mod(x1, nw_ref[...], sh_ref[0], sc_ref[0]).astype(BF16)
    h2_ref[0] = h2
    lane = lax.broadcasted_iota(jnp.int32, (1, LANES), 1)
    logits = jnp.where(lane < N_EXPERTS, _dot(h2, wr_ref[...]), -jnp.inf)
    ex = jnp.exp(logits - jnp.max(logits, axis=-1, keepdims=True))
    aff = pltpu.roll(ex / jnp.sum(ex, axis=-1, keepdims=True), sample * N_EXPERTS, 1)

    @pl.when(sample == 0)
    def _():
        aff_ref[...] = aff

    @pl.when(sample > 0)
    def _():
        aff_ref[...] += aff


def _out_proj(o, pooled, x, w_out, g1, nw, shift, scale, w_router, tm):
    b, l, d = x.shape
    assert b * N_EXPERTS <= LANES
    half = pooled.shape[2]
    row = lambda j, i: (i, 0, 0)
    tok = lambda j, i: (i, j, 0)
    fixed = lambda j, i: (0, 0)
    vec = pl.BlockSpec((1, 1, d), row)
    return pl.pallas_call(
        _out_kernel,
        grid=(l // tm, b),
        in_specs=[pl.BlockSpec((1, o.shape[1], tm, LANES), lambda j, i: (i, 0, j, 0)),
                  pl.BlockSpec((1, tm, half), tok), pl.BlockSpec((1, tm, d), tok),
                  pl.BlockSpec(w_out.shape, fixed), vec, pl.BlockSpec((1, d), fixed), vec, vec,
                  pl.BlockSpec(w_router.shape, fixed)],
        out_specs=[pl.BlockSpec((1, tm, d), tok), pl.BlockSpec((1, tm, d), tok),
                   pl.BlockSpec((tm, LANES), lambda j, i: (j, 0))],
        out_shape=[jax.ShapeDtypeStruct((b, l, d), F32), jax.ShapeDtypeStruct((b, l, d), BF16),
                   jax.ShapeDtypeStruct((l, LANES), F32)],
        compiler_params=_params("arbitrary", "arbitrary"),
        name="out_proj",
    )(o, pooled, x, w_out, g1, nw, shift, scale, w_router)


TOKEN_BLOCK = 256
MXU_DEPTH = 256


def _route_kernel(aff_ref, slotc_ref, slotr_ref, affr_ref, starts_ref, *, cap):
    seq = aff_ref.shape[0]
    aff = aff_ref[...]

    def bit_step(it, lo_bits):
        cand_bits = lo_bits | jnp.left_shift(jnp.int32(1), 30 - it)
        cand = lax.bitcast_convert_type(cand_bits, F32)
        count = jnp.sum(_indicator(aff >= cand), axis=0, keepdims=True)
        return jnp.where(count >= cap, cand_bits, lo_bits)

    lo_bits = lax.fori_loop(0, 31, bit_step, jnp.zeros((1, LANES), jnp.int32))
    lo = lax.bitcast_convert_type(lo_bits, F32)
    hi = lax.bitcast_convert_type(lo_bits + 1, F32)
    above = aff >= hi
    tied = (aff >= lo) & (aff < hi)
    need = cap - jnp.sum(_indicator(above), axis=0, keepdims=True)
    flags = jnp.concatenate([_indicator(above), _indicator(tied)], axis=1)

    r = lax.broadcasted_iota(jnp.int32, (TOKEN_BLOCK, TOKEN_BLOCK), 0)
    c = lax.broadcasted_iota(jnp.int32, (TOKEN_BLOCK, TOKEN_BLOCK), 1)
    tri = _indicator(r > c, BF16)
    offset = jnp.zeros((1, 2 * LANES), F32)
    pieces = []
    for j in range(seq // TOKEN_BLOCK):
        blk = flags[j * TOKEN_BLOCK:(j + 1) * TOKEN_BLOCK]
        pieces.append(_dot(tri, blk.astype(BF16)) + offset)
        offset = offset + jnp.sum(blk, axis=0, keepdims=True)
    before = jnp.concatenate(pieces, axis=0)
    tied_before = before[:, LANES:]
    rank = before[:, :LANES] + jnp.minimum(tied_before, need)
    chosen = above | (tied & (tied_before < need))
    slot = jnp.where(chosen, rank, -1.0)
    slotc_ref[...] = slot
    slotr_ref[...] = slot.T
    affr_ref[...] = aff.T
    n_blocks = seq // TOKEN_BLOCK
    for j in range(n_blocks):
        starts_ref[j:j + 1, :] = rank[j * TOKEN_BLOCK:j * TOKEN_BLOCK + 1, :]
    starts_ref[n_blocks:, :] = jnp.full((starts_ref.shape[0] - n_blocks, LANES), cap, F32)


def _route(aff, cap):
    seq = aff.shape[0]
    start_rows = -(-(seq // TOKEN_BLOCK + 1) // 8) * 8
    return pl.pallas_call(
        functools.partial(_route_kernel, cap=cap),
        out_shape=[jax.ShapeDtypeStruct((seq, LANES), F32), jax.ShapeDtypeStruct((LANES, seq), F32),
                   jax.ShapeDtypeStruct((LANES, seq), F32), jax.ShapeDtypeStruct((start_rows, LANES), F32)],
        compiler_params=pltpu.CompilerParams(vmem_limit_bytes=VMEM_LIMIT),
        name="route",
    )(aff)


def _affinity_of(hit, aff):
    return jnp.broadcast_to(jnp.sum(jnp.where(hit, aff, 0.0), axis=1, keepdims=True), (hit.shape[0], LANES))


def _gather_full(h_ref, pos_ref, aff_ref, xe_s, w_s, cap):
    samples, seq, _ = h_ref.shape
    slots = lax.broadcasted_iota(jnp.int32, (cap, seq), 0).astype(F32)
    for s in range(samples):
        hit = slots == pos_ref[s, 0]
        xe_s[s * cap:(s + 1) * cap, :] = _dot(_indicator(hit, BF16), h_ref[s])
        w_s[s * cap:(s + 1) * cap, :] = _affinity_of(hit, aff_ref[s, 0])


def _gather_blocks(starts_ref, base, h_ref, pos_ref, aff_ref, xe_s, w_s, cap, window):
    seq = h_ref.shape[1]
    n_blocks = seq // TOKEN_BLOCK
    if n_blocks == 1 or window >= cap:
        _gather_full(h_ref, pos_ref, aff_ref, xe_s, w_s, cap)
        return
    first, fits = [], None
    for j in range(n_blocks):
        lo, hi = starts_ref[j * LANES + base], starts_ref[(j + 1) * LANES + base]
        start = jnp.minimum(lo // F32_ROWS * F32_ROWS, cap - window)
        first.append(start)
        ok = hi - start <= window
        fits = ok if fits is None else fits & ok

    @pl.when(fits)
    def _():
        xe_s[...] = jnp.zeros(xe_s.shape, F32)
        w_s[...] = jnp.zeros(w_s.shape, F32)
        rel = lax.broadcasted_iota(jnp.int32, (window, TOKEN_BLOCK), 0).astype(F32)
        for j in range(n_blocks):
            tokens = slice(j * TOKEN_BLOCK, (j + 1) * TOKEN_BLOCK)
            hit = rel + first[j].astype(F32) == pos_ref[0, 0][:, tokens]
            rows = pl.ds(pl.multiple_of(first[j], F32_ROWS), window)
            xe_s[rows, :] += _dot(_indicator(hit, BF16), h_ref[0, tokens, :])
            w_s[rows, :] += _affinity_of(hit, aff_ref[0, 0][:, tokens])

    @pl.when(jnp.logical_not(fits))
    def _():
        _gather_full(h_ref, pos_ref, aff_ref, xe_s, w_s, cap)


def _expert_kernel(starts_ref, hx_ref, px_ref, ax_ref, *refs, cap_x, cap_z, window):
    if cap_z:
        hz_ref, pz_ref, az_ref, wg_ref, wu_ref, wd_ref, yx_ref, yz_ref, wg_s, wu_s, wd_s, xe_s, w_s = refs
    else:
        wg_ref, wu_ref, wd_ref, yx_ref, wg_s, wu_s, wd_s, xe_s, w_s = refs
    expert, step = pl.program_id(0), pl.program_id(1)
    n_latent = pl.num_programs(1) - (1 if cap_z else 0)

    @pl.when(step == 0)
    def _():
        wg_s[...] = wg_ref[0, 0].astype(BF16)
        wu_s[...] = wu_ref[0, 0].astype(BF16)
        wd_s[...] = wd_ref[0, 0].astype(BF16)

    @pl.when(step < n_latent)
    def _():
        _gather_blocks(starts_ref, step * N_EXPERTS + expert, hx_ref, px_ref, ax_ref, xe_s, w_s, cap_x, window)

    if cap_z:
        @pl.when(step == n_latent)
        def _():
            _gather_full(hz_ref, pz_ref, az_ref, xe_s, w_s, cap_z)

    xe = xe_s[...].astype(BF16)
    hid = (_silu(_dot(xe, wg_s[...])) * _dot(xe, wu_s[...])).astype(BF16)
    ye = (_dot(hid, wd_s[...]) * w_s[:, 0:1]).astype(BF16)

    @pl.when(step < n_latent)
    def _():
        yx_ref[0, 0] = ye

    if cap_z:
        @pl.when(step == n_latent)
        def _():
            for s in range(yz_ref.shape[0]):
                yz_ref[s, 0] = ye[s * cap_z:(s + 1) * cap_z]


def _experts(w_gate, w_up, w_down, layer, x, z=None):
    b, seq_x, d = x["h2"].shape
    _, n_exp, _, ff = w_gate.shape
    cap_x = EC_CAPACITY * seq_x // n_exp
    lanes = b * n_exp
    rows = lambda a: a[:lanes].reshape(b, n_exp, 1, a.shape[1])
    sample = lambda i: jnp.minimum(i, b - 1)
    wspec = lambda shape: pl.BlockSpec((1, 1) + shape, lambda e, i: (layer, e, 0, 0))
    wbuf = lambda shape: pltpu.VMEM(shape, BF16)
    row_x = pl.BlockSpec((1, 1, 1, seq_x), lambda e, i: (sample(i), e, 0, 0))
    in_specs = [pl.BlockSpec(memory_space=pltpu.SMEM),
                pl.BlockSpec((1, seq_x, d), lambda e, i: (sample(i), 0, 0)), row_x, row_x]
    args = [x["starts"], x["h2"], rows(x["slot_rows"]), rows(x["aff_rows"])]
    out_specs = [pl.BlockSpec((1, 1, cap_x, d), lambda e, i: (sample(i), e, 0, 0))]
    out_shape = [jax.ShapeDtypeStruct((b, n_exp, cap_x, d), BF16)]
    cap_z = 0
    if z is not None:
        seq_z = z["h2"].shape[1]
        cap_z = EC_CAPACITY * seq_z // n_exp
        assert b * cap_z == cap_x
        row_z = pl.BlockSpec((b, 1, 1, seq_z), lambda e, i: (0, e, 0, 0))
        in_specs += [pl.BlockSpec((b, seq_z, d), lambda e, i: (0, 0, 0)), row_z, row_z]
        args += [z["h2"], rows(z["slot_rows"]), rows(z["aff_rows"])]
        out_specs.append(pl.BlockSpec((b, 1, cap_z, d), lambda e, i: (0, e, 0, 0)))
        out_shape.append(jax.ShapeDtypeStruct((b, n_exp, cap_z, d), BF16))
    return pl.pallas_call(
        functools.partial(_expert_kernel, cap_x=cap_x, cap_z=cap_z, window=min(64, cap_x)),
        grid=(n_exp, b + (1 if z is not None else 0)),
        in_specs=in_specs + [wspec((d, ff)), wspec((d, ff)), wspec((ff, d))],
        out_specs=out_specs,
        out_shape=out_shape,
        scratch_shapes=[wbuf((d, ff)), wbuf((d, ff)), wbuf((ff, d)),
                        pltpu.VMEM((cap_x, d), F32), pltpu.VMEM((cap_x, LANES), F32)],
        compiler_params=_params("arbitrary", "arbitrary"),
        name="experts",
    )(*args, w_gate, w_up, w_down)


def _combine_kernel(starts_ref, slot_ref, ye_ref, x_ref, g2_ref, nf_ref, o_ref, *, cap, window, final):
    sample, tile = pl.program_id(0), pl.program_id(1)
    tm = x_ref.shape[1]
    per_pass = MXU_DEPTH // window
    base = tile * LANES + sample * N_EXPERTS
    first, fits = [], None
    for e in range(N_EXPERTS):
        lo, hi = starts_ref[base + e], starts_ref[base + LANES + e]
        start = jnp.minimum(lo // BF16_ROWS * BF16_ROWS, cap - window)
        first.append(start)
        ok = hi - start <= window
        fits = ok if fits is None else fits & ok
    pos = pltpu.roll(slot_ref[...], jnp.where(sample == 0, 0, LANES - sample * N_EXPERTS), 1)

    def finish(acc):
        x2 = x_ref[0] + g2_ref[0] * acc
        if final:
            x2 = x2 * lax.rsqrt(jnp.mean(x2 * x2, axis=-1, keepdims=True) + EPS) * nf_ref[...]
        o_ref[0] = x2

    @pl.when(fits)
    def _():
        lane = lax.broadcasted_iota(jnp.int32, (1, MXU_DEPTH), 1)
        rel = (lane % window).astype(F32)
        acc = jnp.zeros(x_ref.shape[1:], F32)
        for g in range(N_EXPERTS // per_pass):
            experts = range(g * per_pass, (g + 1) * per_pass)
            val = jnp.broadcast_to(pos[:, experts[0]:experts[0] + 1] - first[experts[0]].astype(F32), (tm, MXU_DEPTH))
            for k, e in enumerate(experts[1:], start=1):
                val = jnp.where(lane >= k * window, pos[:, e:e + 1] - first[e].astype(F32), val)
            rows = jnp.concatenate([ye_ref[0, e, pl.ds(pl.multiple_of(first[e], BF16_ROWS), window), :]
                                    for e in experts], axis=0)
            acc = acc + _dot(_indicator(val == rel, BF16), rows)
        finish(acc)

    @pl.when(jnp.logical_not(fits))
    def _():
        slots = lax.broadcasted_iota(jnp.int32, (tm, cap), 1).astype(F32)
        acc = jnp.zeros(x_ref.shape[1:], F32)
        for e in range(N_EXPERTS):
            acc = acc + _dot(_indicator(pos[:, e:e + 1] == slots, BF16), ye_ref[0, e])
        finish(acc)


def _combine(starts, slot_cols, ye, x1, g2, norm_f, cap, final):
    b, l, d = x1.shape
    tm = TOKEN_BLOCK
    window = min(64, cap)
    tok = lambda i, j: (i, j, 0)
    return pl.pallas_call(
        functools.partial(_combine_kernel, cap=cap, window=window, final=final),
        grid=(b, l // tm),
        in_specs=[pl.BlockSpec(memory_space=pltpu.SMEM),
                  pl.BlockSpec((tm, LANES), lambda i, j: (j, 0)),
                  pl.BlockSpec((1, N_EXPERTS, cap, d), lambda i, j: (i, 0, 0, 0)),
                  pl.BlockSpec((1, tm, d), tok), pl.BlockSpec((1, 1, d), lambda i, j: (i, 0, 0)),
                  pl.BlockSpec((1, d), lambda i, j: (0, 0))],
        out_specs=pl.BlockSpec((1, tm, d), tok),
        out_shape=jax.ShapeDtypeStruct((b, l, d), F32),
        compiler_params=_params("arbitrary", "arbitrary"),
        name="combine",
    )(starts, slot_cols, ye, x1, g2, norm_f)


def _token_mixer(x, mod, lw, states, rows, tm, heads):
    sh1, sc1 = mod[0], mod[1]
    qkv, ab, gate, pool = _in_proj(x, sh1, sc1, lw["norm1"], lw["wqkv"], lw["wab"], lw["wgate"], lw["wpool"], tm)
    o, s_f, s_b = _delta_net(qkv, ab, gate, lw["conv_w"], lw["a_log"], lw["dt_bias"], lw["dn_norm"], *states, heads)
    pooled = _pool_mixer(pool, lw["pool_w"], lw["pool_scale"], rows)
    return o, pooled, (s_f, s_b)


def _route_tokens(o, pooled, x, mod, lw, tm):
    seq = x.shape[1]
    x1, h2, aff = _out_proj(o, pooled, x, lw["w_out"], mod[2], lw["norm2"], mod[3], mod[4], lw["w_router"], tm)
    slot_cols, slot_rows, aff_rows, starts = _route(aff, EC_CAPACITY * seq // N_EXPERTS)
    return dict(x1=x1, h2=h2, slot_cols=slot_cols, slot_rows=slot_rows, aff_rows=aff_rows,
                starts=starts.astype(jnp.int32).reshape(-1), g2=mod[5])


def _scatter(r, ye, lw, final):
    cap = EC_CAPACITY * r["x1"].shape[1] // N_EXPERTS
    return _combine(r["starts"], r["slot_cols"], ye, r["x1"], r["g2"], lw["norm_f"], cap, final)


def kernel(x, c, ctx, c_ctx, w_mod, b_mod, norm1, norm2, w_in, conv_w, a_log, dt_bias, dn_norm, pool_w, pool_scale,
           w_out, w_router, w_gate, w_up, w_down, norm_f):
    batch, seq, d = x.shape
    depth = w_mod.shape[0]
    ctx_len = ctx.shape[1]
    dn_width = DN_HEADS * LANES
    qkv_cols = 3 * dn_width
    gate_cols = 2 * N_DIR * DN_HEADS
    state_cols = qkv_cols + gate_cols
    rows = seq // GRID_W
    tm_x, tm_z = 512, ctx_len

    cond_rows = 16
    cond = jnp.zeros((cond_rows, d), F32).at[:batch].set(c).at[batch].set(c_ctx)
    mod_all = _adaln_all(cond, w_mod, b_mod)

    zero_state = jnp.zeros((batch, DN_HEADS, LANES, LANES), F32)
    z = ctx
    for l in range(depth):
        wl = w_in[l]
        lw = dict(
            norm1=norm1[l][None], norm2=norm2[l][None],
            wqkv=wl[:, :qkv_cols].astype(BF16),
            wab=jnp.pad(wl[:, qkv_cols:state_cols], ((0, 0), (0, LANES - gate_cols))).astype(BF16),
            wgate=wl[:, state_cols:state_cols + dn_width].astype(BF16),
            wpool=wl[:, state_cols + dn_width:].astype(BF16),
            conv_w=conv_w[l], a_log=a_log[l], dt_bias=dt_bias[l], dn_norm=dn_norm[l][None],
            pool_w=pool_w[l].astype(BF16), pool_scale=pool_scale[l][None],
            w_out=w_out[l].astype(BF16),
            w_router=jnp.pad(w_router[l], ((0, 0), (0, LANES - N_EXPERTS))).astype(BF16),
            norm_f=norm_f[None],
        )
        mods = mod_all[l].reshape(cond_rows, 6, d)
        mod_x = [mods[:batch, i][:, None, :] for i in range(6)]
        mod_z = [jnp.broadcast_to(mods[batch, i][None, None, :], (batch, 1, d)) for i in range(6)]

        o_z, pooled_z, ctx_states = _token_mixer(z, mod_z, lw, (zero_state, zero_state), None, tm_z, DN_HEADS)
        o_x, pooled_x, _ = _token_mixer(x, mod_x, lw, ctx_states, rows, tm_x, 1)
        rx = _route_tokens(o_x, pooled_x, x, mod_x, lw, tm_x)
        if l < depth - 1:
            rz = _route_tokens(o_z, pooled_z, z, mod_z, lw, tm_z)
            ye_x, ye_z = _experts(w_gate, w_up, w_down, l, rx, rz)
            z = _scatter(rz, ye_z, lw, False)
        else:
            ye_x, = _experts(w_gate, w_up, w_down, l, rx)
        x = _scatter(rx, ye_x, lw, l == depth - 1)
    return x
```

```python
import functools

import jax
import jax.numpy as jnp
from jax import lax
from jax.experimental import pallas as pl
from jax.experimental.pallas import tpu as pltpu

F32 = jnp.float32
BF16 = jnp.bfloat16

LANES = 128
F32_ROWS = 8
BF16_ROWS = 16
GRID_W = 64
DN_HEADS = 4
N_DIR = 2
CHUNK = 64
DN_GROUP = 16
POOL_WINDOWS = (2, 4, 8, 16)
POOL_PAD = GRID_W * max(POOL_WINDOWS) // 4
N_EXPERTS = 16
EC_CAPACITY = 2
EPS = 1e-6
VMEM_LIMIT = 56 * 1024 * 1024


def _params(*semantics):
    return pltpu.CompilerParams(dimension_semantics=semantics, vmem_limit_bytes=VMEM_LIMIT)


def _silu(x):
    return x * jax.nn.sigmoid(x)


def _dot(a, b):
    return jnp.dot(a, b, preferred_element_type=F32)


def _indicator(mask, dtype=F32):
    return jnp.where(mask, 1.0, 0.0).astype(dtype)


def _mod_kernel(cond_ref, w_ref, b_ref, o_ref):
    s = _silu(cond_ref[...])
    s_hi = s.astype(BF16)
    s_lo = (s - s_hi.astype(F32)).astype(BF16)
    w = w_ref[0]
    w_hi = w.astype(BF16)
    w_lo = (w - w_hi.astype(F32)).astype(BF16)
    rows = s.shape[0]
    both = _dot(jnp.concatenate([s_hi, s_lo], axis=0), w_hi)
    o_ref[0] = both[:rows] + both[rows:] + _dot(s_hi, w_lo) + b_ref[0]


def _adaln_all(cond, w_mod, b_mod):
    depth, d, n = w_mod.shape
    rows = cond.shape[0]
    tn = 1536
    return pl.pallas_call(
        _mod_kernel,
        grid=(depth, n // tn),
        in_specs=[pl.BlockSpec((rows, d), lambda l, j: (0, 0)),
                  pl.BlockSpec((1, d, tn), lambda l, j: (l, 0, j)),
                  pl.BlockSpec((1, 1, tn), lambda l, j: (l, 0, j))],
        out_specs=pl.BlockSpec((1, rows, tn), lambda l, j: (l, 0, j)),
        out_shape=jax.ShapeDtypeStruct((depth, rows, n), F32),
        compiler_params=_params("arbitrary", "arbitrary"),
        name="adaln",
    )(cond, w_mod, b_mod.reshape(depth, 1, n))


def _norm_mod(x, nw, shift, scale):
    ms = jnp.mean(x * x, axis=-1, keepdims=True)
    return (x * lax.rsqrt(ms + EPS) * nw) * (1.0 + scale) + shift


def _in_kernel(x_ref, sh_ref, sc_ref, nw_ref, wqkv_ref, wab_ref, wgate_ref, wpool_ref,
               qkv_ref, ab_ref, gate_ref, pool_ref):
    h = _norm_mod(x_ref[0], nw_ref[...], sh_ref[0], sc_ref[0]).astype(BF16)
    qkv = _dot(h, wqkv_ref[...])
    for c in range(qkv_ref.shape[1]):
        qkv_ref[0, c] = qkv[:, c * LANES:(c + 1) * LANES]
    gate = _dot(h, wgate_ref[...])
    for c in range(gate_ref.shape[1]):
        gate_ref[0, c] = gate[:, c * LANES:(c + 1) * LANES]
    ab_ref[0] = _dot(h, wab_ref[...])
    pool_ref[0] = _dot(h, wpool_ref[...])


def _in_proj(x, shift, scale, nw, wqkv, wab, wgate, wpool, tm):
    b, l, d = x.shape
    row = lambda i, j: (i, 0, 0)
    tok = lambda i, j: (i, j, 0)
    fixed = lambda i, j: (0, 0)
    heads = lambda w: w.shape[1] // LANES
    per_head = lambda w: pl.BlockSpec((1, heads(w), tm, LANES), lambda i, j: (i, 0, j, 0))
    return pl.pallas_call(
        _in_kernel,
        grid=(b, l // tm),
        in_specs=[pl.BlockSpec((1, tm, d), tok),
                  pl.BlockSpec((1, 1, d), row), pl.BlockSpec((1, 1, d), row),
                  pl.BlockSpec((1, d), fixed)]
                 + [pl.BlockSpec(w.shape, fixed) for w in (wqkv, wab, wgate, wpool)],
        out_specs=[per_head(wqkv), pl.BlockSpec((1, tm, wab.shape[1]), tok), per_head(wgate),
                   pl.BlockSpec((1, tm, wpool.shape[1]), tok)],
        out_shape=[jax.ShapeDtypeStruct((b, heads(wqkv), l, LANES), F32),
                   jax.ShapeDtypeStruct((b, l, wab.shape[1]), F32),
                   jax.ShapeDtypeStruct((b, heads(wgate), l, LANES), F32),
                   jax.ShapeDtypeStruct((b, l, wpool.shape[1]), F32)],
        compiler_params=_params("arbitrary", "arbitrary"),
        name="in_proj",
    )(x, shift, scale, nw, wqkv, wab, wgate, wpool)


def _bdot(a, b):
    return jnp.einsum('gij,gjk->gik', a, b, preferred_element_type=F32)


def _dn_kernel(alog_ref, dtb_ref, q_ref, k_ref, v_ref, ab_ref, gate_ref, cwq_ref, cwk_ref, cwv_ref,
               dnw_ref, s0f_ref, s0b_ref, o_ref, sf_ref, sb_ref,
               qs, ks, vs, bet_all, la_all, bet, gl_f, gl_b, la, kn_f, kn_b, bn_f, bn_b, qn_f, qn_b, on_f, on_b, pad,
               *, group):
    step = pl.program_id(1)
    _, heads, seq, _ = q_ref.shape
    n_chunks = seq // CHUNK
    lane = lax.broadcasted_iota(jnp.int32, (1, LANES), 1)

    edge = jnp.zeros((F32_ROWS, LANES), F32)
    pad[0:F32_ROWS, :] = edge
    pad[pl.ds(F32_ROWS + seq, F32_ROWS), :] = edge

    def conv_silu(x, w):
        pad[pl.ds(F32_ROWS, seq), :] = x
        y = (pad[pl.ds(F32_ROWS - 2, seq), :] * w[0:1] + pad[pl.ds(F32_ROWS - 1, seq), :] * w[1:2]
             + x * w[2:3] + pad[pl.ds(F32_ROWS + 1, seq), :] * w[3:4])
        return _silu(y)

    def l2norm(x):
        return x * lax.rsqrt(jnp.sum(x * x, axis=-1, keepdims=True) + EPS)

    @pl.when(step == 0)
    def _():
        ab = ab_ref[0]
        bet_all[...] = jax.nn.sigmoid(ab)
        la_all[...] = -jnp.exp(alog_ref[...]) * jax.nn.softplus(ab + dtb_ref[...])

    for h in range(heads):
        own = pl.ds(h * seq, seq)
        lanes = slice(h * LANES, (h + 1) * LANES)
        qs[own, :] = l2norm(conv_silu(q_ref[0, h], cwq_ref[:, lanes])) * (LANES ** -0.5)
        ks[own, :] = l2norm(conv_silu(k_ref[0, h], cwk_ref[:, lanes]))
        vs[own, :] = conv_silu(v_ref[0, h], cwv_ref[:, lanes])
        head = step * heads + h
        shift = jnp.where(head == 0, 0, LANES - head)
        bet[own, :] = pltpu.roll(bet_all[...], shift, 1)
        log_a = pltpu.roll(la_all[...], shift, 1)
        hi = log_a.astype(BF16).astype(F32)
        mid = (log_a - hi).astype(BF16).astype(F32)
        low = (log_a - hi - mid).astype(BF16).astype(F32)
        part = lane & (DN_HEADS - 1)
        la[own, :] = jnp.where(part == 0, hi, jnp.where(part == 1, pltpu.roll(mid, 1, 1),
                                                        pltpu.roll(low, 2, 1))).astype(BF16)

    ii = lax.broadcasted_iota(jnp.int32, (CHUNK, CHUNK), 0)
    jj = lax.broadcasted_iota(jnp.int32, (CHUNK, CHUNK), 1)
    eye = _indicator(ii == jj)
    span = group * CHUNK
    n_groups = heads * n_chunks // group

    def aligned(start, size):
        return pl.ds(start if isinstance(start, int) else pl.multiple_of(start, size), size)

    def wy_direction(first, direction, kn, bn, qn, on, gl):
        rows = aligned(first * CHUNK, span)
        state_rows = aligned(first * LANES, group * LANES)
        chunks = lambda a: a.reshape(group, CHUNK, a.shape[-1])
        q, k, v = chunks(qs[rows, :]), chunks(ks[rows, :]), chunks(vs[rows, :])
        log_a = chunks(la[rows, :])
        col = DN_HEADS * direction
        if direction == 0:
            incl, strict = ii >= jj, ii > jj
        else:
            incl, strict = ii <= jj, ii < jj
        sums = jnp.concatenate([_indicator(incl), jnp.ones((CHUNK, CHUNK), F32)], axis=0).astype(BF16)
        parts = jnp.concatenate([_dot(sums, log_a[c]) for c in range(group)], axis=0)
        cums = parts + pltpu.roll(parts, LANES - 1, 1) + pltpu.roll(parts, LANES - 2, 1)
        cums = cums.reshape(group, 2 * CHUNK, LANES)
        gl[rows, :] = cums[:, CHUNK:].reshape(span, LANES)
        g = cums[:, :CHUNK, col:col + 1]
        g_last = cums[:, CHUNK:, col:col + 1]
        beta = chunks(bet[rows, :])[:, :, 2 * DN_HEADS + col:2 * DN_HEADS + col + 1]
        g_cols = jnp.swapaxes(jnp.broadcast_to(g, (group, CHUNK, LANES)), 1, 2)[:, :CHUNK, :]
        decay = jnp.exp(jnp.minimum(g - g_cols, 0.0))
        e_g = jnp.exp(g)
        k_beta = k * beta
        both = jnp.einsum('gik,gjk->gij', jnp.concatenate([k_beta, q], axis=1).astype(BF16), k.astype(BF16),
                          preferred_element_type=F32)
        m = jnp.where(strict, both[:, :CHUNK] * decay, 0.0)
        qk = jnp.where(incl, both[:, CHUNK:] * decay, 0.0).astype(BF16)
        p = -m
        t_inv = eye + p
        pb = p.astype(BF16)
        p = _bdot(pb, pb)
        for _ in range(CHUNK.bit_length() - 3):
            pb = p.astype(BF16)
            r = _bdot(jnp.concatenate([pb, t_inv.astype(BF16)], axis=1), pb)
            p, t_inv = r[:, :CHUNK], t_inv + r[:, CHUNK:]
        t_inv = t_inv + _bdot(t_inv.astype(BF16), p.astype(BF16))
        wu = _bdot(t_inv.astype(BF16),
                   jnp.concatenate([k_beta * e_g, v * beta], axis=2).astype(BF16)).astype(BF16)
        k_tail = (k * jnp.exp(g_last - g)).astype(BF16)
        kb = jnp.einsum('gik,gin->gkn', k_tail, wu, preferred_element_type=F32)
        qb = _bdot(qk, wu)
        kn[state_rows, :] = kb[:, :, :LANES].reshape(group * LANES, LANES).astype(BF16)
        bn[state_rows, :] = kb[:, :, LANES:].reshape(group * LANES, LANES)
        qn[rows, :] = (q * e_g - qb[:, :, :LANES]).reshape(span, LANES).astype(BF16)
        on[rows, :] = qb[:, :, LANES:].reshape(span, LANES)

    fwd = (kn_f, bn_f, qn_f, on_f, gl_f)
    bwd = (kn_b, bn_b, qn_b, on_b, gl_b)

    def scan_chunk(n, state, direction, kn, bn, qn, on, gl):
        col = DN_HEADS * direction
        rows = aligned(n * CHUNK, CHUNK)
        state_rows = aligned(n * LANES, LANES)
        s16 = state.astype(BF16)
        on[rows, :] = on[rows, :] + _dot(qn[rows, :], s16)
        s_decay = jnp.exp(gl[aligned(n * CHUNK, F32_ROWS), :][0:1, col:col + 1])
        return state * s_decay - _dot(kn[state_rows, :], s16) + bn[state_rows, :]

    def scan_step(i, states):
        out = []
        for h in range(heads):
            out.append(scan_chunk(h * n_chunks + i, states[2 * h], 0, *fwd))
            out.append(scan_chunk(h * n_chunks + n_chunks - 1 - i, states[2 * h + 1], 1, *bwd))
        return tuple(out)

    states = tuple(ref[0, h] for h in range(heads) for ref in (s0f_ref, s0b_ref))
    if heads == 1 and n_groups == 2:
        wy_direction(0, 0, *fwd)
        wy_direction(group, 1, *bwd)
        wy_direction(group, 0, *fwd)
        wy_direction(0, 1, *bwd)
        for i in range(group):
            states = scan_step(i, states)
        final = lax.fori_loop(group, n_chunks, scan_step, states)
    else:
        def wy_group(gi, carry):
            wy_direction(gi * group, 0, *fwd)
            wy_direction(gi * group, 1, *bwd)
            return carry

        lax.fori_loop(0, n_groups, wy_group, 0)
        final = lax.fori_loop(0, n_chunks, scan_step, states)
    for h in range(heads):
        sf_ref[0, h] = final[2 * h]
        sb_ref[0, h] = final[2 * h + 1]
        own = pl.ds(h * seq, seq)
        o = on_f[own, :] + on_b[own, :]
        y = o * lax.rsqrt(jnp.mean(o * o, axis=-1, keepdims=True) + EPS) * dnw_ref[...]
        o_ref[0, h] = (y * _silu(gate_ref[0, h])).astype(BF16)


def _delta_net(qkv, ab, gate, conv_w, a_log, dt_bias, dn_norm, s0f, s0b, heads):
    b, _, seq, hd = qkv.shape
    lane_vec = lambda a: jnp.pad(a.reshape(1, -1), ((0, 0), (0, LANES - a.size)))
    vec = pl.BlockSpec((1, LANES), lambda i, s: (0, 0))
    part = DN_HEADS // heads
    per_head = lambda off: pl.BlockSpec((1, heads, seq, hd), lambda i, s: (i, off * part + s, 0, 0))
    cw = lambda off: pl.BlockSpec((conv_w.shape[0], heads * hd), lambda i, s: (0, off * part + s))
    st = pl.BlockSpec((1, heads, hd, hd), lambda i, s: (i, s, 0, 0))
    rows = heads * seq
    group = min(DN_GROUP, rows // CHUNK)
    shared_f32 = pltpu.VMEM((seq, hd), F32)
    rows_f32 = pltpu.VMEM((rows, hd), F32)
    rows_bf16 = pltpu.VMEM((rows, hd), BF16)
    mats_f32 = pltpu.VMEM((rows // CHUNK * hd, hd), F32)
    mats_bf16 = pltpu.VMEM((rows // CHUNK * hd, hd), BF16)
    return pl.pallas_call(
        functools.partial(_dn_kernel, group=group),
        grid=(b, part),
        in_specs=[vec, vec, per_head(0), per_head(1), per_head(2),
                  pl.BlockSpec((1, seq, LANES), lambda i, s: (i, 0, 0)),
                  per_head(0), cw(0), cw(1), cw(2),
                  pl.BlockSpec((1, hd), lambda i, s: (0, 0)), st, st],
        out_specs=[per_head(0), st, st],
        out_shape=[jax.ShapeDtypeStruct((b, DN_HEADS, seq, hd), BF16),
                   jax.ShapeDtypeStruct((b, DN_HEADS, hd, hd), F32),
                   jax.ShapeDtypeStruct((b, DN_HEADS, hd, hd), F32)],
        scratch_shapes=[rows_f32] * 3 + [shared_f32] * 2 + [rows_f32] * 3 + [rows_bf16]
                       + [mats_bf16] * 2 + [mats_f32] * 2 + [rows_bf16] * 2 + [rows_f32] * 2
                       + [pltpu.VMEM((seq + 2 * F32_ROWS, hd), F32)],
        compiler_params=_params("arbitrary", "arbitrary"),
        name="delta_net",
    )(lane_vec(a_log), lane_vec(dt_bias), qkv, qkv, qkv, ab, gate, conv_w, conv_w, conv_w, dn_norm, s0f, s0b)


def _window_sum(x, pos, limit, half, stride, bufs):
    seq = x.shape[0]
    body = pl.ds(POOL_PAD, seq)
    shifted = lambda ref, off: ref[pl.ds(POOL_PAD + off, seq), :]
    keep = (lambda ok, v: jnp.where(ok, v, 0.0)) if stride == 1 else (lambda ok, v: v)
    a, l, r = bufs
    a[body, :] = x
    left = keep(pos >= 1, shifted(a, -stride))
    right = x
    k = 1
    while k < half:
        l[body, :] = left
        r[body, :] = right
        left = left + keep(pos >= k, shifted(l, -k * stride))
        right = right + keep(pos + k < limit, shifted(r, k * stride))
        k *= 2
    return left + right


def _window_count(pos, limit, half):
    return (jnp.minimum(pos + half, limit) - jnp.maximum(pos - half, 0)).astype(F32)


def _pool_kernel(u_ref, pw_ref, ps_ref, o_ref, *bufs, rows):
    seq = u_ref.shape[1]
    t = lax.broadcasted_iota(jnp.int32, (seq, LANES), 0)
    border = jnp.zeros((POOL_PAD, LANES), F32)
    for buf in bufs:
        buf[0:POOL_PAD, :] = border
        buf[pl.ds(POOL_PAD + seq, POOL_PAD), :] = border
    for g, w in enumerate(POOL_WINDOWS):
        lanes = slice(g * LANES, (g + 1) * LANES)
        x = u_ref[0, :, lanes]
        half = w // 2
        if rows is None:
            total = _window_sum(x, t, seq, half, 1, bufs)
            count = _window_count(t, seq, half)
        else:
            c, r = t & (GRID_W - 1), t >> (GRID_W.bit_length() - 1)
            total = _window_sum(_window_sum(x, c, GRID_W, half, 1, bufs), r, rows, half, GRID_W, bufs)
            count = _window_count(r, rows, half) * _window_count(c, GRID_W, half)
        m = (total / count - x).astype(BF16)
        o_ref[0, :, lanes] = (_dot(m, pw_ref[g]) * ps_ref[:, lanes]).astype(BF16)


def _pool_mixer(u, pool_w, pool_scale, rows):
    b, seq, width = u.shape
    blk = pl.BlockSpec((1, seq, width), lambda i: (i, 0, 0))
    return pl.pallas_call(
        functools.partial(_pool_kernel, rows=rows),
        grid=(b,),
        in_specs=[blk, pl.BlockSpec(pool_w.shape, lambda i: (0, 0, 0)), pl.BlockSpec((1, width), lambda i: (0, 0))],
        out_specs=blk,
        out_shape=jax.ShapeDtypeStruct((b, seq, width), BF16),
        scratch_shapes=[pltpu.VMEM((seq + 2 * POOL_PAD, LANES), F32)] * 3,
        compiler_params=_params("arbitrary"),
        name="pool_mixer",
    )(u, pool_w, pool_scale)


def _out_kernel(o_ref, p_ref, x_ref, wo_ref, g1_ref, nw_ref, sh_ref, sc_ref, wr_ref, x1_ref, h2_ref, aff_ref):
    sample = pl.program_id(1)
    o = jnp.concatenate([o_ref[0, h] for h in range(o_ref.shape[1])], axis=1)
    half = o.shape[1]
    y = _dot(o, wo_ref[:half, :]) + _dot(p_ref[0], wo_ref[half:, :])
    x1 = x_ref[0] + g1_ref[0] * y
    x1_ref[0] = x1
    h2 = _norm_mod(x1, nw_ref[...], sh_ref[0], sc_ref[0]).astype(BF16)
    h2_ref[0] = h2
    lane = lax.broadcasted_iota(jnp.int32, (1, LANES), 1)
    logits = jnp.where(lane < N_EXPERTS, _dot(h2, wr_ref[...]), -jnp.inf)
    ex = jnp.exp(logits - jnp.max(logits, axis=-1, keepdims=True))
    aff = pltpu.roll(ex / jnp.sum(ex, axis=-1, keepdims=True), sample * N_EXPERTS, 1)

    @pl.when(sample == 0)
    def _():
        aff_ref[...] = aff

    @pl.when(sample > 0)
    def _():
        aff_ref[...] += aff


def _out_proj(o, pooled, x, w_out, g1, nw, shift, scale, w_router, tm):
    b, l, d = x.shape
    assert b * N_EXPERTS <= LANES
    half = pooled.shape[2]
    row = lambda j, i: (i, 0, 0)
    tok = lambda j, i: (i, j, 0)
    fixed = lambda j, i: (0, 0)
    vec = pl.BlockSpec((1, 1, d), row)
    return pl.pallas_call(
        _out_kernel,
        grid=(l // tm, b),
        in_specs=[pl.BlockSpec((1, o.shape[1], tm, LANES), lambda j, i: (i, 0, j, 0)),
                  pl.BlockSpec((1, tm, half), tok), pl.BlockSpec((1, tm, d), tok),
                  pl.BlockSpec(w_out.shape, fixed), vec, pl.BlockSpec((1, d), fixed), vec, vec,
                  pl.BlockSpec(w_router.shape, fixed)],
        out_specs=[pl.BlockSpec((1, tm, d), tok), pl.BlockSpec((1, tm, d), tok),
                   pl.BlockSpec((tm, LANES), lambda j, i: (j, 0))],
        out_shape=[jax.ShapeDtypeStruct((b, l, d), F32), jax.ShapeDtypeStruct((b, l, d), BF16),
                   jax.ShapeDtypeStruct((l, LANES), F32)],
        compiler_params=_params("arbitrary", "arbitrary"),
        name="out_proj",
    )(o, pooled, x, w_out, g1, nw, shift, scale, w_router)


TOKEN_BLOCK = 256
MXU_DEPTH = 256


def _route_kernel(aff_ref, slotc_ref, slotr_ref, affr_ref, starts_ref, *, cap):
    seq = aff_ref.shape[0]
    aff = aff_ref[...]

    def bit_step(it, lo_bits):
        cand_bits = lo_bits | jnp.left_shift(jnp.int32(1), 30 - it)
        cand = lax.bitcast_convert_type(cand_bits, F32)
        count = jnp.sum(_indicator(aff >= cand), axis=0, keepdims=True)
        return jnp.where(count >= cap, cand_bits, lo_bits)

    lo_bits = lax.fori_loop(0, 31, bit_step, jnp.zeros((1, LANES), jnp.int32))
    lo = lax.bitcast_convert_type(lo_bits, F32)
    hi = lax.bitcast_convert_type(lo_bits + 1, F32)
    above = aff >= hi
    tied = (aff >= lo) & (aff < hi)
    need = cap - jnp.sum(_indicator(above), axis=0, keepdims=True)
    flags = jnp.concatenate([_indicator(above), _indicator(tied)], axis=1)

    r = lax.broadcasted_iota(jnp.int32, (TOKEN_BLOCK, TOKEN_BLOCK), 0)
    c = lax.broadcasted_iota(jnp.int32, (TOKEN_BLOCK, TOKEN_BLOCK), 1)
    tri = _indicator(r > c, BF16)
    offset = jnp.zeros((1, 2 * LANES), F32)
    pieces = []
    for j in range(seq // TOKEN_BLOCK):
        blk = flags[j * TOKEN_BLOCK:(j + 1) * TOKEN_BLOCK]
        pieces.append(_dot(tri, blk.astype(BF16)) + offset)
        offset = offset + jnp.sum(blk, axis=0, keepdims=True)
    before = jnp.concatenate(pieces, axis=0)
    tied_before = before[:, LANES:]
    rank = before[:, :LANES] + jnp.minimum(tied_before, need)
    chosen = above | (tied & (tied_before < need))
    slot = jnp.where(chosen, rank, -1.0)
    slotc_ref[...] = slot
    slotr_ref[...] = slot.T
    affr_ref[...] = aff.T
    n_blocks = seq // TOKEN_BLOCK
    for j in range(n_blocks):
        starts_ref[j:j + 1, :] = rank[j * TOKEN_BLOCK:j * TOKEN_BLOCK + 1, :]
    starts_ref[n_blocks:, :] = jnp.full((starts_ref.shape[0] - n_blocks, LANES), cap, F32)


def _route(aff, cap):
    seq = aff.shape[0]
    start_rows = -(-(seq // TOKEN_BLOCK + 1) // 8) * 8
    return pl.pallas_call(
        functools.partial(_route_kernel, cap=cap),
        out_shape=[jax.ShapeDtypeStruct((seq, LANES), F32), jax.ShapeDtypeStruct((LANES, seq), F32),
                   jax.ShapeDtypeStruct((LANES, seq), F32), jax.ShapeDtypeStruct((start_rows, LANES), F32)],
        compiler_params=pltpu.CompilerParams(vmem_limit_bytes=VMEM_LIMIT),
        name="route",
    )(aff)


def _affinity_of(hit, aff):
    return jnp.broadcast_to(jnp.sum(jnp.where(hit, aff, 0.0), axis=1, keepdims=True), (hit.shape[0], LANES))


def _gather_full(h_ref, pos_ref, aff_ref, xe_s, w_s, cap):
    samples, seq, _ = h_ref.shape
    slots = lax.broadcasted_iota(jnp.int32, (cap, seq), 0).astype(F32)
    for s in range(samples):
        hit = slots == pos_ref[s, 0]
        xe_s[s * cap:(s + 1) * cap, :] = _dot(_indicator(hit, BF16), h_ref[s])
        w_s[s * cap:(s + 1) * cap, :] = _affinity_of(hit, aff_ref[s, 0])


def _gather_blocks(starts_ref, base, h_ref, pos_ref, aff_ref, xe_s, w_s, cap, window):
    seq = h_ref.shape[1]
    n_blocks = seq // TOKEN_BLOCK
    if n_blocks == 1 or window >= cap:
        _gather_full(h_ref, pos_ref, aff_ref, xe_s, w_s, cap)
        return
    first, fits = [], None
    for j in range(n_blocks):
        lo, hi = starts_ref[j * LANES + base], starts_ref[(j + 1) * LANES + base]
        start = jnp.minimum(lo // F32_ROWS * F32_ROWS, cap - window)
        first.append(start)
        ok = hi - start <= window
        fits = ok if fits is None else fits & ok

    @pl.when(fits)
    def _():
        xe_s[...] = jnp.zeros(xe_s.shape, F32)
        w_s[...] = jnp.zeros(w_s.shape, F32)
        rel = lax.broadcasted_iota(jnp.int32, (window, TOKEN_BLOCK), 0).astype(F32)
        for j in range(n_blocks):
            tokens = slice(j * TOKEN_BLOCK, (j + 1) * TOKEN_BLOCK)
            hit = rel + first[j].astype(F32) == pos_ref[0, 0][:, tokens]
            rows = pl.ds(pl.multiple_of(first[j], F32_ROWS), window)
            xe_s[rows, :] += _dot(_indicator(hit, BF16), h_ref[0, tokens, :])
            w_s[rows, :] += _affinity_of(hit, aff_ref[0, 0][:, tokens])

    @pl.when(jnp.logical_not(fits))
    def _():
        _gather_full(h_ref, pos_ref, aff_ref, xe_s, w_s, cap)


def _expert_kernel(starts_ref, hx_ref, px_ref, ax_ref, *refs, cap_x, cap_z, window):
    if cap_z:
        hz_ref, pz_ref, az_ref, wg_ref, wu_ref, wd_ref, yx_ref, yz_ref, wg_s, wu_s, wd_s, xe_s, w_s = refs
    else:
        wg_ref, wu_ref, wd_ref, yx_ref, wg_s, wu_s, wd_s, xe_s, w_s = refs
    expert, step = pl.program_id(0), pl.program_id(1)
    n_latent = pl.num_programs(1) - (1 if cap_z else 0)

    @pl.when(step == 0)
    def _():
        wg_s[...] = wg_ref[0, 0].astype(BF16)
        wu_s[...] = wu_ref[0, 0].astype(BF16)
        wd_s[...] = wd_ref[0, 0].astype(BF16)

    @pl.when(step < n_latent)
    def _():
        _gather_blocks(starts_ref, step * N_EXPERTS + expert, hx_ref, px_ref, ax_ref, xe_s, w_s, cap_x, window)

    if cap_z:
        @pl.when(step == n_latent)
        def _():
            _gather_full(hz_ref, pz_ref, az_ref, xe_s, w_s, cap_z)

    xe = xe_s[...].astype(BF16)
    hid = (_silu(_dot(xe, wg_s[...])) * _dot(xe, wu_s[...])).astype(BF16)
    ye = (_dot(hid, wd_s[...]) * w_s[:, 0:1]).astype(BF16)

    @pl.when(step < n_latent)
    def _():
        yx_ref[0, 0] = ye

    if cap_z:
        @pl.when(step == n_latent)
        def _():
            for s in range(yz_ref.shape[0]):
                yz_ref[s, 0] = ye[s * cap_z:(s + 1) * cap_z]


def _experts(w_gate, w_up, w_down, layer, x, z=None):
    b, seq_x, d = x["h2"].shape
    _, n_exp, _, ff = w_gate.shape
    cap_x = EC_CAPACITY * seq_x // n_exp
    lanes = b * n_exp
    rows = lambda a: a[:lanes].reshape(b, n_exp, 1, a.shape[1])
    sample = lambda i: jnp.minimum(i, b - 1)
    wspec = lambda shape: pl.BlockSpec((1, 1) + shape, lambda e, i: (layer, e, 0, 0))
    wbuf = lambda shape: pltpu.VMEM(shape, BF16)
    row_x = pl.BlockSpec((1, 1, 1, seq_x), lambda e, i: (sample(i), e, 0, 0))
    in_specs = [pl.BlockSpec(memory_space=pltpu.SMEM),
                pl.BlockSpec((1, seq_x, d), lambda e, i: (sample(i), 0, 0)), row_x, row_x]
    args = [x["starts"], x["h2"], rows(x["slot_rows"]), rows(x["aff_rows"])]
    out_specs = [pl.BlockSpec((1, 1, cap_x, d), lambda e, i: (sample(i), e, 0, 0))]
    out_shape = [jax.ShapeDtypeStruct((b, n_exp, cap_x, d), BF16)]
    cap_z = 0
    if z is not None:
        seq_z = z["h2"].shape[1]
        cap_z = EC_CAPACITY * seq_z // n_exp
        assert b * cap_z == cap_x
        row_z = pl.BlockSpec((b, 1, 1, seq_z), lambda e, i: (0, e, 0, 0))
        in_specs += [pl.BlockSpec((b, seq_z, d), lambda e, i: (0, 0, 0)), row_z, row_z]
        args += [z["h2"], rows(z["slot_rows"]), rows(z["aff_rows"])]
        out_specs.append(pl.BlockSpec((b, 1, cap_z, d), lambda e, i: (0, e, 0, 0)))
        out_shape.append(jax.ShapeDtypeStruct((b, n_exp, cap_z, d), BF16))
    return pl.pallas_call(
        functools.partial(_expert_kernel, cap_x=cap_x, cap_z=cap_z, window=min(64, cap_x)),
        grid=(n_exp, b + (1 if z is not None else 0)),
        in_specs=in_specs + [wspec((d, ff)), wspec((d, ff)), wspec((ff, d))],
        out_specs=out_specs,
        out_shape=out_shape,
        scratch_shapes=[wbuf((d, ff)), wbuf((d, ff)), wbuf((ff, d)),
                        pltpu.VMEM((cap_x, d), F32), pltpu.VMEM((cap_x, LANES), F32)],
        compiler_params=_params("arbitrary", "arbitrary"),
        name="experts",
    )(*args, w_gate, w_up, w_down)


def _combine_kernel(starts_ref, slot_ref, ye_ref, x_ref, g2_ref, nf_ref, o_ref, *, cap, window, final):
    sample, tile = pl.program_id(0), pl.program_id(1)
    tm = x_ref.shape[1]
    per_pass = MXU_DEPTH // window
    base = tile * LANES + sample * N_EXPERTS
    first, fits = [], None
    for e in range(N_EXPERTS):
        lo, hi = starts_ref[base + e], starts_ref[base + LANES + e]
        start = jnp.minimum(lo // BF16_ROWS * BF16_ROWS, cap - window)
        first.append(start)
        ok = hi - start <= window
        fits = ok if fits is None else fits & ok
    pos = pltpu.roll(slot_ref[...], jnp.where(sample == 0, 0, LANES - sample * N_EXPERTS), 1)

    def finish(acc):
        x2 = x_ref[0] + g2_ref[0] * acc
        if final:
            x2 = x2 * lax.rsqrt(jnp.mean(x2 * x2, axis=-1, keepdims=True) + EPS) * nf_ref[...]
        o_ref[0] = x2

    @pl.when(fits)
    def _():
        lane = lax.broadcasted_iota(jnp.int32, (1, MXU_DEPTH), 1)
        rel = (lane % window).astype(F32)
        acc = jnp.zeros(x_ref.shape[1:], F32)
        for g in range(N_EXPERTS // per_pass):
            experts = range(g * per_pass, (g + 1) * per_pass)
            val = jnp.broadcast_to(pos[:, experts[0]:experts[0] + 1] - first[experts[0]].astype(F32), (tm, MXU_DEPTH))
            for k, e in enumerate(experts[1:], start=1):
                val = jnp.where(lane >= k * window, pos[:, e:e + 1] - first[e].astype(F32), val)
            rows = jnp.concatenate([ye_ref[0, e, pl.ds(pl.multiple_of(first[e], BF16_ROWS), window), :]
                                    for e in experts], axis=0)
            acc = acc + _dot(_indicator(val == rel, BF16), rows)
        finish(acc)

    @pl.when(jnp.logical_not(fits))
    def _():
        slots = lax.broadcasted_iota(jnp.int32, (tm, cap), 1).astype(F32)
        acc = jnp.zeros(x_ref.shape[1:], F32)
        for e in range(N_EXPERTS):
            acc = acc + _dot(_indicator(pos[:, e:e + 1] == slots, BF16), ye_ref[0, e])
        finish(acc)


def _combine(starts, slot_cols, ye, x1, g2, norm_f, cap, final):
    b, l, d = x1.shape
    tm = TOKEN_BLOCK
    window = min(64, cap)
    tok = lambda i, j: (i, j, 0)
    return pl.pallas_call(
        functools.partial(_combine_kernel, cap=cap, window=window, final=final),
        grid=(b, l // tm),
        in_specs=[pl.BlockSpec(memory_space=pltpu.SMEM),
                  pl.BlockSpec((tm, LANES), lambda i, j: (j, 0)),
                  pl.BlockSpec((1, N_EXPERTS, cap, d), lambda i, j: (i, 0, 0, 0)),
                  pl.BlockSpec((1, tm, d), tok), pl.BlockSpec((1, 1, d), lambda i, j: (i, 0, 0)),
                  pl.BlockSpec((1, d), lambda i, j: (0, 0))],
        out_specs=pl.BlockSpec((1, tm, d), tok),
        out_shape=jax.ShapeDtypeStruct((b, l, d), F32),
        compiler_params=_params("arbitrary", "arbitrary"),
        name="combine",
    )(starts, slot_cols, ye, x1, g2, norm_f)


def _token_mixer(x, mod, lw, states, rows, tm, heads):
    sh1, sc1 = mod[0], mod[1]
    qkv, ab, gate, pool = _in_proj(x, sh1, sc1, lw["norm1"], lw["wqkv"], lw["wab"], lw["wgate"], lw["wpool"], tm)
    o, s_f, s_b = _delta_net(qkv, ab, gate, lw["conv_w"], lw["a_log"], lw["dt_bias"], lw["dn_norm"], *states, heads)
    pooled = _pool_mixer(pool, lw["pool_w"], lw["pool_scale"], rows)
    return o, pooled, (s_f, s_b)


def _route_tokens(o, pooled, x, mod, lw, tm):
    seq = x.shape[1]
    x1, h2, aff = _out_proj(o, pooled, x, lw["w_out"], mod[2], lw["norm2"], mod[3], mod[4], lw["w_router"], tm)
    slot_cols, slot_rows, aff_rows, starts = _route(aff, EC_CAPACITY * seq // N_EXPERTS)
    return dict(x1=x1, h2=h2, slot_cols=slot_cols, slot_rows=slot_rows, aff_rows=aff_rows,
                starts=starts.astype(jnp.int32).reshape(-1), g2=mod[5])


def _scatter(r, ye, lw, final):
    cap = EC_CAPACITY * r["x1"].shape[1] // N_EXPERTS
    return _combine(r["starts"], r["slot_cols"], ye, r["x1"], r["g2"], lw["norm_f"], cap, final)


def kernel(x, c, ctx, c_ctx, w_mod, b_mod, norm1, norm2, w_in, conv_w, a_log, dt_bias, dn_norm, pool_w, pool_scale,
           w_out, w_router, w_gate, w_up, w_down, norm_f):
    batch, seq, d = x.shape
    depth = w_mod.shape[0]
    ctx_len = ctx.shape[1]
    dn_width = DN_HEADS * LANES
    qkv_cols = 3 * dn_width
    gate_cols = 2 * N_DIR * DN_HEADS
    state_cols = qkv_cols + gate_cols
    rows = seq // GRID_W
    tm_x, tm_z = 1024, ctx_len

    cond_rows = 16
    cond = jnp.zeros((cond_rows, d), F32).at[:batch].set(c).at[batch].set(c_ctx)
    mod_all = _adaln_all(cond, w_mod, b_mod)

    zero_state = jnp.zeros((batch, DN_HEADS, LANES, LANES), F32)
    z = ctx
    for l in range(depth):
        wl = w_in[l]
        lw = dict(
            norm1=norm1[l][None], norm2=norm2[l][None],
            wqkv=wl[:, :qkv_cols].astype(BF16),
            wab=jnp.pad(wl[:, qkv_cols:state_cols], ((0, 0), (0, LANES - gate_cols))).astype(BF16),
            wgate=wl[:, state_cols:state_cols + dn_width].astype(BF16),
            wpool=wl[:, state_cols + dn_width:].astype(BF16),
            conv_w=conv_w[l], a_log=a_log[l], dt_bias=dt_bias[l], dn_norm=dn_norm[l][None],
            pool_w=pool_w[l].astype(BF16), pool_scale=pool_scale[l][None],
            w_out=w_out[l].astype(BF16),
            w_router=jnp.pad(w_router[l], ((0, 0), (0, LANES - N_EXPERTS))).astype(BF16),
            norm_f=norm_f[None],
        )
        mods = mod_all[l].reshape(cond_rows, 6, d)
        mod_x = [mods[:batch, i][:, None, :] for i in range(6)]
        mod_z = [jnp.broadcast_to(mods[batch, i][None, None, :], (batch, 1, d)) for i in range(6)]

        o_z, pooled_z, ctx_states = _token_mixer(z, mod_z, lw, (zero_state, zero_state), None, tm_z, DN_HEADS)
        o_x, pooled_x, _ = _token_mixer(x, mod_x, lw, ctx_states, rows, tm_x, 1)
        rx = _route_tokens(o_x, pooled_x, x, mod_x, lw, tm_x)
        if l < depth - 1:
            rz = _route_tokens(o_z, pooled_z, z, mod_z, lw, tm_z)
            ye_x, ye_z = _experts(w_gate, w_up, w_down, l, rx, rz)
            z = _scatter(rz, ye_z, lw, False)
        else:
            ye_x, = _experts(w_gate, w_up, w_down, l, rx)
        x = _scatter(rx, ye_x, lw, l == depth - 1)
    return x
```

```python
import functools

import jax
import jax.numpy as jnp
from jax import lax
from jax.experimental import pallas as pl
from jax.experimental.pallas import tpu as pltpu

F32 = jnp.float32
BF16 = jnp.bfloat16

LANES = 128
F32_ROWS = 8
BF16_ROWS = 16
GRID_W = 64
DN_HEADS = 4
N_DIR = 2
CHUNK = 64
DN_GROUP = 16
POOL_WINDOWS = (2, 4, 8, 16)
POOL_PAD = GRID_W * max(POOL_WINDOWS) // 4
N_EXPERTS = 16
EC_CAPACITY = 2
EPS = 1e-6
VMEM_LIMIT = 56 * 1024 * 1024


def _params(*semantics):
    return pltpu.CompilerParams(dimension_semantics=semantics, vmem_limit_bytes=VMEM_LIMIT)


def _silu(x):
    return x * jax.nn.sigmoid(x)


def _dot(a, b):
    return jnp.dot(a, b, preferred_element_type=F32)


def _indicator(mask, dtype=F32):
    return jnp.where(mask, 1.0, 0.0).astype(dtype)


def _mod_kernel(cond_ref, w_ref, b_ref, o_ref):
    s = _silu(cond_ref[...])
    s_hi = s.astype(BF16)
    s_lo = (s - s_hi.astype(F32)).astype(BF16)
    w = w_ref[0]
    w_hi = w.astype(BF16)
    w_lo = (w - w_hi.astype(F32)).astype(BF16)
    rows = s.shape[0]
    both = _dot(jnp.concatenate([s_hi, s_lo], axis=0), w_hi)
    o_ref[0] = both[:rows] + both[rows:] + _dot(s_hi, w_lo) + b_ref[0]


def _adaln_all(cond, w_mod, b_mod):
    depth, d, n = w_mod.shape
    rows = cond.shape[0]
    tn = 1536
    return pl.pallas_call(
        _mod_kernel,
        grid=(depth, n // tn),
        in_specs=[pl.BlockSpec((rows, d), lambda l, j: (0, 0)),
                  pl.BlockSpec((1, d, tn), lambda l, j: (l, 0, j)),
                  pl.BlockSpec((1, 1, tn), lambda l, j: (l, 0, j))],
        out_specs=pl.BlockSpec((1, rows, tn), lambda l, j: (l, 0, j)),
        out_shape=jax.ShapeDtypeStruct((depth, rows, n), F32),
        compiler_params=_params("arbitrary", "arbitrary"),
        name="adaln",
    )(cond, w_mod, b_mod.reshape(depth, 1, n))


def _norm_mod(x, nw, shift, scale):
    ms = jnp.mean(x * x, axis=-1, keepdims=True)
    return (x * lax.rsqrt(ms + EPS) * nw) * (1.0 + scale) + shift


def _in_kernel(x_ref, sh_ref, sc_ref, nw_ref, wqkv_ref, wab_ref, wgate_ref, wpool_ref,
               qkv_ref, ab_ref, gate_ref, pool_ref):
    h = _norm_mod(x_ref[0], nw_ref[...], sh_ref[0], sc_ref[0]).astype(BF16)
    qkv = _dot(h, wqkv_ref[...])
    for c in range(qkv_ref.shape[1]):
        qkv_ref[0, c] = qkv[:, c * LANES:(c + 1) * LANES]
    gate = _dot(h, wgate_ref[...])
    for c in range(gate_ref.shape[1]):
        gate_ref[0, c] = gate[:, c * LANES:(c + 1) * LANES]
    ab_ref[0] = _dot(h, wab_ref[...])
    pool_ref[0] = _dot(h, wpool_ref[...])


def _in_proj(x, shift, scale, nw, wqkv, wab, wgate, wpool, tm):
    b, l, d = x.shape
    row = lambda i, j: (i, 0, 0)
    tok = lambda i, j: (i, j, 0)
    fixed = lambda i, j: (0, 0)
    heads = lambda w: w.shape[1] // LANES
    per_head = lambda w: pl.BlockSpec((1, heads(w), tm, LANES), lambda i, j: (i, 0, j, 0))
    return pl.pallas_call(
        _in_kernel,
        grid=(b, l // tm),
        in_specs=[pl.BlockSpec((1, tm, d), tok),
                  pl.BlockSpec((1, 1, d), row), pl.BlockSpec((1, 1, d), row),
                  pl.BlockSpec((1, d), fixed)]
                 + [pl.BlockSpec(w.shape, fixed) for w in (wqkv, wab, wgate, wpool)],
        out_specs=[per_head(wqkv), pl.BlockSpec((1, tm, wab.shape[1]), tok), per_head(wgate),
                   pl.BlockSpec((1, tm, wpool.shape[1]), tok)],
        out_shape=[jax.ShapeDtypeStruct((b, heads(wqkv), l, LANES), F32),
                   jax.ShapeDtypeStruct((b, l, wab.shape[1]), F32),
                   jax.ShapeDtypeStruct((b, heads(wgate), l, LANES), F32),
                   jax.ShapeDtypeStruct((b, l, wpool.shape[1]), F32)],
        compiler_params=_params("arbitrary", "arbitrary"),
        name="in_proj",
    )(x, shift, scale, nw, wqkv, wab, wgate, wpool)


def _bdot(a, b):
    return jnp.einsum('gij,gjk->gik', a, b, preferred_element_type=F32)


def _dn_kernel(alog_ref, dtb_ref, q_ref, k_ref, v_ref, ab_ref, gate_ref, cwq_ref, cwk_ref, cwv_ref,
               dnw_ref, s0f_ref, s0b_ref, o_ref, sf_ref, sb_ref,
               qs, ks, vs, bet_all, la_all, bet, gl_f, gl_b, la, kn_f, kn_b, bn_f, bn_b, qn_f, qn_b, on_f, on_b, pad,
               *, group):
    step = pl.program_id(1)
    _, heads, seq, _ = q_ref.shape
    n_chunks = seq // CHUNK
    lane = lax.broadcasted_iota(jnp.int32, (1, LANES), 1)

    edge = jnp.zeros((F32_ROWS, LANES), F32)
    pad[0:F32_ROWS, :] = edge
    pad[pl.ds(F32_ROWS + seq, F32_ROWS), :] = edge

    def conv_silu(x, w):
        pad[pl.ds(F32_ROWS, seq), :] = x
        y = (pad[pl.ds(F32_ROWS - 2, seq), :] * w[0:1] + pad[pl.ds(F32_ROWS - 1, seq), :] * w[1:2]
             + x * w[2:3] + pad[pl.ds(F32_ROWS + 1, seq), :] * w[3:4])
        return _silu(y)

    def l2norm(x):
        return x * lax.rsqrt(jnp.sum(x * x, axis=-1, keepdims=True) + EPS)

    @pl.when(step == 0)
    def _():
        ab = ab_ref[0]
        bet_all[...] = jax.nn.sigmoid(ab)
        la_all[...] = -jnp.exp(alog_ref[...]) * jax.nn.softplus(ab + dtb_ref[...])

    for h in range(heads):
        own = pl.ds(h * seq, seq)
        lanes = slice(h * LANES, (h + 1) * LANES)
        qs[own, :] = l2norm(conv_silu(q_ref[0, h], cwq_ref[:, lanes])) * (LANES ** -0.5)
        ks[own, :] = l2norm(conv_silu(k_ref[0, h], cwk_ref[:, lanes]))
        vs[own, :] = conv_silu(v_ref[0, h], cwv_ref[:, lanes])
        head = step * heads + h
        shift = jnp.where(head == 0, 0, LANES - head)
        bet[own, :] = pltpu.roll(bet_all[...], shift, 1)
        log_a = pltpu.roll(la_all[...], shift, 1)
        hi = log_a.astype(BF16).astype(F32)
        mid = (log_a - hi).astype(BF16).astype(F32)
        low = (log_a - hi - mid).astype(BF16).astype(F32)
        part = lane & (DN_HEADS - 1)
        la[own, :] = jnp.where(part == 0, hi, jnp.where(part == 1, pltpu.roll(mid, 1, 1),
                                                        pltpu.roll(low, 2, 1))).astype(BF16)

    ii = lax.broadcasted_iota(jnp.int32, (CHUNK, CHUNK), 0)
    jj = lax.broadcasted_iota(jnp.int32, (CHUNK, CHUNK), 1)
    eye = _indicator(ii == jj)
    span = group * CHUNK
    n_groups = heads * n_chunks // group

    def aligned(start, size):
        return pl.ds(start if isinstance(start, int) else pl.multiple_of(start, size), size)

    def wy_direction(first, direction, kn, bn, qn, on, gl):
        rows = aligned(first * CHUNK, span)
        state_rows = aligned(first * LANES, group * LANES)
        chunks = lambda a: a.reshape(group, CHUNK, a.shape[-1])
        q, k, v = chunks(qs[rows, :]), chunks(ks[rows, :]), chunks(vs[rows, :])
        log_a = chunks(la[rows, :])
        col = DN_HEADS * direction
        if direction == 0:
            incl, strict = ii >= jj, ii > jj
        else:
            incl, strict = ii <= jj, ii < jj
        sums = jnp.concatenate([_indicator(incl), jnp.ones((CHUNK, CHUNK), F32)], axis=0).astype(BF16)
        parts = jnp.concatenate([_dot(sums, log_a[c]) for c in range(group)], axis=0)
        cums = parts + pltpu.roll(parts, LANES - 1, 1) + pltpu.roll(parts, LANES - 2, 1)
        cums = cums.reshape(group, 2 * CHUNK, LANES)
        gl[rows, :] = cums[:, CHUNK:].reshape(span, LANES)
        g = cums[:, :CHUNK, col:col + 1]
        g_last = cums[:, CHUNK:, col:col + 1]
        beta = chunks(bet[rows, :])[:, :, 2 * DN_HEADS + col:2 * DN_HEADS + col + 1]
        g_cols = jnp.swapaxes(jnp.broadcast_to(g, (group, CHUNK, LANES)), 1, 2)[:, :CHUNK, :]
        decay = jnp.exp(jnp.minimum(g - g_cols, 0.0))
        e_g = jnp.exp(g)
        k_beta = k * beta
        both = jnp.einsum('gik,gjk->gij', jnp.concatenate([k_beta, q], axis=1).astype(BF16), k.astype(BF16),
                          preferred_element_type=F32)
        m = jnp.where(strict, both[:, :CHUNK] * decay, 0.0)
        qk = jnp.where(incl, both[:, CHUNK:] * decay, 0.0).astype(BF16)
        p = -m
        t_inv = eye + p
        pb = p.astype(BF16)
        p = _bdot(pb, pb)
        for _ in range(CHUNK.bit_length() - 3):
            pb = p.astype(BF16)
            r = _bdot(jnp.concatenate([pb, t_inv.astype(BF16)], axis=1), pb)
            p, t_inv = r[:, :CHUNK], t_inv + r[:, CHUNK:]
        t_inv = t_inv + _bdot(t_inv.astype(BF16), p.astype(BF16))
        wu = _bdot(t_inv.astype(BF16),
                   jnp.concatenate([k_beta * e_g, v * beta], axis=2).astype(BF16)).astype(BF16)
        k_tail = (k * jnp.exp(g_last - g)).astype(BF16)
        kb = jnp.einsum('gik,gin->gkn', k_tail, wu, preferred_element_type=F32)
        qb = _bdot(qk, wu)
        kn[state_rows, :] = kb[:, :, :LANES].reshape(group * LANES, LANES).astype(BF16)
        bn[state_rows, :] = kb[:, :, LANES:].reshape(group * LANES, LANES)
        qn[rows, :] = (q * e_g - qb[:, :, :LANES]).reshape(span, LANES).astype(BF16)
        on[rows, :] = qb[:, :, LANES:].reshape(span, LANES)

    fwd = (kn_f, bn_f, qn_f, on_f, gl_f)
    bwd = (kn_b, bn_b, qn_b, on_b, gl_b)

    def scan_chunk(n, state, direction, kn, bn, qn, on, gl):
        col = DN_HEADS * direction
        rows = aligned(n * CHUNK, CHUNK)
        state_rows = aligned(n * LANES, LANES)
        s16 = state.astype(BF16)
        on[rows, :] = on[rows, :] + _dot(qn[rows, :], s16)
        s_decay = jnp.exp(gl[aligned(n * CHUNK, F32_ROWS), :][0:1, col:col + 1])
        return state * s_decay - _dot(kn[state_rows, :], s16) + bn[state_rows, :]

    def scan_step(i, states):
        out = []
        for h in range(heads):
            out.append(scan_chunk(h * n_chunks + i, states[2 * h], 0, *fwd))
            out.append(scan_chunk(h * n_chunks + n_chunks - 1 - i, states[2 * h + 1], 1, *bwd))
        return tuple(out)

    states = tuple(ref[0, h] for h in range(heads) for ref in (s0f_ref, s0b_ref))
    if heads == 1 and n_groups == 2:
        wy_direction(0, 0, *fwd)
        wy_direction(group, 1, *bwd)
        wy_direction(group, 0, *fwd)
        wy_direction(0, 1, *bwd)
        for i in range(group):
            states = scan_step(i, states)
        final = lax.fori_loop(group, n_chunks, scan_step, states)
    else:
        def wy_group(gi, carry):
            wy_direction(gi * group, 0, *fwd)
            wy_direction(gi * group, 1, *bwd)
            return carry

        lax.fori_loop(0, n_groups, wy_group, 0)
        final = lax.fori_loop(0, n_chunks, scan_step, states)
    for h in range(heads):
        sf_ref[0, h] = final[2 * h]
        sb_ref[0, h] = final[2 * h + 1]
        own = pl.ds(h * seq, seq)
        o = on_f[own, :] + on_b[own, :]
        y = o * lax.rsqrt(jnp.mean(o * o, axis=-1, keepdims=True) + EPS) * dnw_ref[...]
        o_ref[0, h] = (y * _silu(gate_ref[0, h])).astype(BF16)


def _delta_net(qkv, ab, gate, conv_w, a_log, dt_bias, dn_norm, s0f, s0b, heads):
    b, _, seq, hd = qkv.shape
    lane_vec = lambda a: jnp.pad(a.reshape(1, -1), ((0, 0), (0, LANES - a.size)))
    vec = pl.BlockSpec((1, LANES), lambda i, s: (0, 0))
    part = DN_HEADS // heads
    per_head = lambda off: pl.BlockSpec((1, heads, seq, hd), lambda i, s: (i, off * part + s, 0, 0))
    cw = lambda off: pl.BlockSpec((conv_w.shape[0], heads * hd), lambda i, s: (0, off * part + s))
    st = pl.BlockSpec((1, heads, hd, hd), lambda i, s: (i, s, 0, 0))
    rows = heads * seq
    group = min(DN_GROUP, rows // CHUNK)
    shared_f32 = pltpu.VMEM((seq, hd), F32)
    rows_f32 = pltpu.VMEM((rows, hd), F32)
    rows_bf16 = pltpu.VMEM((rows, hd), BF16)
    mats_f32 = pltpu.VMEM((rows // CHUNK * hd, hd), F32)
    mats_bf16 = pltpu.VMEM((rows // CHUNK * hd, hd), BF16)
    return pl.pallas_call(
        functools.partial(_dn_kernel, group=group),
        grid=(b, part),
        in_specs=[vec, vec, per_head(0), per_head(1), per_head(2),
                  pl.BlockSpec((1, seq, LANES), lambda i, s: (i, 0, 0)),
                  per_head(0), cw(0), cw(1), cw(2),
                  pl.BlockSpec((1, hd), lambda i, s: (0, 0)), st, st],
        out_specs=[per_head(0), st, st],
        out_shape=[jax.ShapeDtypeStruct((b, DN_HEADS, seq, hd), BF16),
                   jax.ShapeDtypeStruct((b, DN_HEADS, hd, hd), F32),
                   jax.ShapeDtypeStruct((b, DN_HEADS, hd, hd), F32)],
        scratch_shapes=[rows_f32] * 3 + [shared_f32] * 2 + [rows_f32] * 3 + [rows_bf16]
                       + [mats_bf16] * 2 + [mats_f32] * 2 + [rows_bf16] * 2 + [rows_f32] * 2
                       + [pltpu.VMEM((seq + 2 * F32_ROWS, hd), F32)],
        compiler_params=_params("arbitrary", "arbitrary"),
        name="delta_net",
    )(lane_vec(a_log), lane_vec(dt_bias), qkv, qkv, qkv, ab, gate, conv_w, conv_w, conv_w, dn_norm, s0f, s0b)


def _window_sum(x, pos, limit, half, stride, bufs):
    seq = x.shape[0]
    body = pl.ds(POOL_PAD, seq)
    shifted = lambda ref, off: ref[pl.ds(POOL_PAD + off, seq), :]
    keep = (lambda ok, v: jnp.where(ok, v, 0.0)) if stride == 1 else (lambda ok, v: v)
    a, l, r = bufs
    a[body, :] = x
    left = keep(pos >= 1, shifted(a, -stride))
    right = x
    k = 1
    while k < half:
        l[body, :] = left
        r[body, :] = right
        left = left + keep(pos >= k, shifted(l, -k * stride))
        right = right + keep(pos + k < limit, shifted(r, k * stride))
        k *= 2
    return left + right


def _window_count(pos, limit, half):
    return (jnp.minimum(pos + half, limit) - jnp.maximum(pos - half, 0)).astype(F32)


def _pool_kernel(u_ref, pw_ref, ps_ref, o_ref, *bufs, rows):
    seq = u_ref.shape[1]
    t = lax.broadcasted_iota(jnp.int32, (seq, LANES), 0)
    border = jnp.zeros((POOL_PAD, LANES), F32)
    for buf in bufs:
        buf[0:POOL_PAD, :] = border
        buf[pl.ds(POOL_PAD + seq, POOL_PAD), :] = border
    for g, w in enumerate(POOL_WINDOWS):
        lanes = slice(g * LANES, (g + 1) * LANES)
        x = u_ref[0, :, lanes]
        half = w // 2
        if rows is None:
            total = _window_sum(x, t, seq, half, 1, bufs)
            count = _window_count(t, seq, half)
        else:
            c, r = t & (GRID_W - 1), t >> (GRID_W.bit_length() - 1)
            total = _window_sum(_window_sum(x, c, GRID_W, half, 1, bufs), r, rows, half, GRID_W, bufs)
            in_cols = _window_count(lax.broadcasted_iota(jnp.int32, (1, GRID_W, LANES), 1), GRID_W, half)
            in_rows = _window_count(lax.broadcasted_iota(jnp.int32, (rows, 1, LANES), 0), rows, half)
            count = (in_rows * in_cols).reshape(seq, LANES)
        m = (total / count - x).astype(BF16)
        o_ref[0, :, lanes] = (_dot(m, pw_ref[g]) * ps_ref[:, lanes]).astype(BF16)


def _pool_mixer(u, pool_w, pool_scale, rows):
    b, seq, width = u.shape
    blk = pl.BlockSpec((1, seq, width), lambda i: (i, 0, 0))
    return pl.pallas_call(
        functools.partial(_pool_kernel, rows=rows),
        grid=(b,),
        in_specs=[blk, pl.BlockSpec(pool_w.shape, lambda i: (0, 0, 0)), pl.BlockSpec((1, width), lambda i: (0, 0))],
        out_specs=blk,
        out_shape=jax.ShapeDtypeStruct((b, seq, width), BF16),
        scratch_shapes=[pltpu.VMEM((seq + 2 * POOL_PAD, LANES), F32)] * 3,
        compiler_params=_params("arbitrary"),
        name="pool_mixer",
    )(u, pool_w, pool_scale)


def _out_kernel(o_ref, p_ref, x_ref, wo_ref, g1_ref, nw_ref, sh_ref, sc_ref, wr_ref, x1_ref, h2_ref, aff_ref):
    sample = pl.program_id(1)
    o = jnp.concatenate([o_ref[0, h] for h in range(o_ref.shape[1])], axis=1)
    half = o.shape[1]
    y = _dot(o, wo_ref[:half, :]) + _dot(p_ref[0], wo_ref[half:, :])
    x1 = x_ref[0] + g1_ref[0] * y
    x1_ref[0] = x1
    h2 = _norm_mod(x1, nw_ref[...], sh_ref[0], sc_ref[0]).astype(BF16)
    h2_ref[0] = h2
    lane = lax.broadcasted_iota(jnp.int32, (1, LANES), 1)
    logits = jnp.where(lane < N_EXPERTS, _dot(h2, wr_ref[...]), -jnp.inf)
    ex = jnp.exp(logits - jnp.max(logits, axis=-1, keepdims=True))
    aff = pltpu.roll(ex / jnp.sum(ex, axis=-1, keepdims=True), sample * N_EXPERTS, 1)

    @pl.when(sample == 0)
    def _():
        aff_ref[...] = aff

    @pl.when(sample > 0)
    def _():
        aff_ref[...] += aff


def _out_proj(o, pooled, x, w_out, g1, nw, shift, scale, w_router, tm):
    b, l, d = x.shape
    assert b * N_EXPERTS <= LANES
    half = pooled.shape[2]
    row = lambda j, i: (i, 0, 0)
    tok = lambda j, i: (i, j, 0)
    fixed = lambda j, i: (0, 0)
    vec = pl.BlockSpec((1, 1, d), row)
    return pl.pallas_call(
        _out_kernel,
        grid=(l // tm, b),
        in_specs=[pl.BlockSpec((1, o.shape[1], tm, LANES), lambda j, i: (i, 0, j, 0)),
                  pl.BlockSpec((1, tm, half), tok), pl.BlockSpec((1, tm, d), tok),
                  pl.BlockSpec(w_out.shape, fixed), vec, pl.BlockSpec((1, d), fixed), vec, vec,
                  pl.BlockSpec(w_router.shape, fixed)],
        out_specs=[pl.BlockSpec((1, tm, d), tok), pl.BlockSpec((1, tm, d), tok),
                   pl.BlockSpec((tm, LANES), lambda j, i: (j, 0))],
        out_shape=[jax.ShapeDtypeStruct((b, l, d), F32), jax.ShapeDtypeStruct((b, l, d), BF16),
                   jax.ShapeDtypeStruct((l, LANES), F32)],
        compiler_params=_params("arbitrary", "arbitrary"),
        name="out_proj",
    )(o, pooled, x, w_out, g1, nw, shift, scale, w_router)


TOKEN_BLOCK = 256
MXU_DEPTH = 256


def _route_kernel(aff_ref, slotc_ref, slotr_ref, affr_ref, starts_ref, *, cap):
    seq = aff_ref.shape[0]
    aff = aff_ref[...]

    def bit_step(it, lo_bits):
        cand_bits = lo_bits | jnp.left_shift(jnp.int32(1), 30 - it)
        cand = lax.bitcast_convert_type(cand_bits, F32)
        count = jnp.sum(_indicator(aff >= cand), axis=0, keepdims=True)
        return jnp.where(count >= cap, cand_bits, lo_bits)

    lo_bits = lax.fori_loop(0, 31, bit_step, jnp.zeros((1, LANES), jnp.int32))
    lo = lax.bitcast_convert_type(lo_bits, F32)
    hi = lax.bitcast_convert_type(lo_bits + 1, F32)
    above = aff >= hi
    tied = (aff >= lo) & (aff < hi)
    need = cap - jnp.sum(_indicator(above), axis=0, keepdims=True)
    flags = jnp.concatenate([_indicator(above), _indicator(tied)], axis=1)

    r = lax.broadcasted_iota(jnp.int32, (TOKEN_BLOCK, TOKEN_BLOCK), 0)
    c = lax.broadcasted_iota(jnp.int32, (TOKEN_BLOCK, TOKEN_BLOCK), 1)
    tri = _indicator(r > c, BF16)
    offset = jnp.zeros((1, 2 * LANES), F32)
    pieces = []
    for j in range(seq // TOKEN_BLOCK):
        blk = flags[j * TOKEN_BLOCK:(j + 1) * TOKEN_BLOCK]
        pieces.append(_dot(tri, blk.astype(BF16)) + offset)
        offset = offset + jnp.sum(blk, axis=0, keepdims=True)
    before = jnp.concatenate(pieces, axis=0)
    tied_before = before[:, LANES:]
    rank = before[:, :LANES] + jnp.minimum(tied_before, need)
    chosen = above | (tied & (tied_before < need))
    slot = jnp.where(chosen, rank, -1.0)
    slotc_ref[...] = slot
    slotr_ref[...] = slot.T
    affr_ref[...] = aff.T
    n_blocks = seq // TOKEN_BLOCK
    for j in range(n_blocks):
        starts_ref[j:j + 1, :] = rank[j * TOKEN_BLOCK:j * TOKEN_BLOCK + 1, :]
    starts_ref[n_blocks:, :] = jnp.full((starts_ref.shape[0] - n_blocks, LANES), cap, F32)


def _route(aff, cap):
    seq = aff.shape[0]
    start_rows = -(-(seq // TOKEN_BLOCK + 1) // 8) * 8
    return pl.pallas_call(
        functools.partial(_route_kernel, cap=cap),
        out_shape=[jax.ShapeDtypeStruct((seq, LANES), F32), jax.ShapeDtypeStruct((LANES, seq), F32),
                   jax.ShapeDtypeStruct((LANES, seq), F32), jax.ShapeDtypeStruct((start_rows, LANES), F32)],
        compiler_params=pltpu.CompilerParams(vmem_limit_bytes=VMEM_LIMIT),
        name="route",
    )(aff)


def _affinity_of(hit, aff):
    return jnp.broadcast_to(jnp.sum(jnp.where(hit, aff, 0.0), axis=1, keepdims=True), (hit.shape[0], LANES))


def _gather_full(h_ref, pos_ref, aff_ref, row0, xe_s, w_s, cap):
    samples, seq, _ = h_ref.shape
    slots = lax.broadcasted_iota(jnp.int32, (cap, seq), 0).astype(F32)
    for s in range(samples):
        row = pl.ds(row0 + s * N_EXPERTS, 1)
        hit = slots == pos_ref[row, :]
        xe_s[s * cap:(s + 1) * cap, :] = _dot(_indicator(hit, BF16), h_ref[s])
        w_s[s * cap:(s + 1) * cap, :] = _affinity_of(hit, aff_ref[row, :])


def _gather_blocks(starts_ref, base, h_ref, pos_ref, aff_ref, row0, xe_s, w_s, cap, window):
    seq = h_ref.shape[1]
    n_blocks = seq // TOKEN_BLOCK
    if n_blocks == 1 or window >= cap:
        _gather_full(h_ref, pos_ref, aff_ref, row0, xe_s, w_s, cap)
        return
    first, fits = [], None
    for j in range(n_blocks):
        lo, hi = starts_ref[j * LANES + base], starts_ref[(j + 1) * LANES + base]
        start = jnp.minimum(lo // F32_ROWS * F32_ROWS, cap - window)
        first.append(start)
        ok = hi - start <= window
        fits = ok if fits is None else fits & ok

    @pl.when(fits)
    def _():
        xe_s[...] = jnp.zeros(xe_s.shape, F32)
        w_s[...] = jnp.zeros(w_s.shape, F32)
        rel = lax.broadcasted_iota(jnp.int32, (window, TOKEN_BLOCK), 0).astype(F32)
        pos, aff = pos_ref[pl.ds(row0, 1), :], aff_ref[pl.ds(row0, 1), :]
        for j in range(n_blocks):
            tokens = slice(j * TOKEN_BLOCK, (j + 1) * TOKEN_BLOCK)
            hit = rel + first[j].astype(F32) == pos[:, tokens]
            rows = pl.ds(pl.multiple_of(first[j], F32_ROWS), window)
            xe_s[rows, :] += _dot(_indicator(hit, BF16), h_ref[0, tokens, :])
            w_s[rows, :] += _affinity_of(hit, aff[:, tokens])

    @pl.when(jnp.logical_not(fits))
    def _():
        _gather_full(h_ref, pos_ref, aff_ref, row0, xe_s, w_s, cap)


def _expert_kernel(starts_ref, hx_ref, px_ref, ax_ref, *refs, cap_x, cap_z, window):
    if cap_z:
        hz_ref, pz_ref, az_ref, wg_ref, wu_ref, wd_ref, yx_ref, yz_ref, wg_s, wu_s, wd_s, xe_s, w_s = refs
    else:
        wg_ref, wu_ref, wd_ref, yx_ref, wg_s, wu_s, wd_s, xe_s, w_s = refs
    expert, step = pl.program_id(0), pl.program_id(1)
    n_latent = pl.num_programs(1) - (1 if cap_z else 0)

    @pl.when(step == 0)
    def _():
        wg_s[...] = wg_ref[0, 0].astype(BF16)
        wu_s[...] = wu_ref[0, 0].astype(BF16)
        wd_s[...] = wd_ref[0, 0].astype(BF16)

    @pl.when(step < n_latent)
    def _():
        lane = step * N_EXPERTS + expert
        _gather_blocks(starts_ref, lane, hx_ref, px_ref, ax_ref, lane % F32_ROWS, xe_s, w_s, cap_x, window)

    if cap_z:
        @pl.when(step == n_latent)
        def _():
            _gather_full(hz_ref, pz_ref, az_ref, expert, xe_s, w_s, cap_z)

    xe = xe_s[...].astype(BF16)
    hid = (_silu(_dot(xe, wg_s[...])) * _dot(xe, wu_s[...])).astype(BF16)
    ye = (_dot(hid, wd_s[...]) * w_s[:, 0:1]).astype(BF16)

    @pl.when(step < n_latent)
    def _():
        yx_ref[0, 0] = ye

    if cap_z:
        @pl.when(step == n_latent)
        def _():
            for s in range(yz_ref.shape[0]):
                yz_ref[s, 0] = ye[s * cap_z:(s + 1) * cap_z]


def _experts(w_gate, w_up, w_down, layer, x, z=None):
    b, seq_x, d = x["h2"].shape
    _, n_exp, _, ff = w_gate.shape
    cap_x = EC_CAPACITY * seq_x // n_exp
    sample = lambda i: jnp.minimum(i, b - 1)
    wspec = lambda shape: pl.BlockSpec((1, 1) + shape, lambda e, i: (layer, e, 0, 0))
    wbuf = lambda shape: pltpu.VMEM(shape, BF16)
    row_x = pl.BlockSpec((F32_ROWS, seq_x), lambda e, i: ((sample(i) * n_exp + e) // F32_ROWS, 0))
    in_specs = [pl.BlockSpec(memory_space=pltpu.SMEM),
                pl.BlockSpec((1, seq_x, d), lambda e, i: (sample(i), 0, 0)), row_x, row_x]
    args = [x["starts"], x["h2"], x["slot_rows"], x["aff_rows"]]
    out_specs = [pl.BlockSpec((1, 1, cap_x, d), lambda e, i: (sample(i), e, 0, 0))]
    out_shape = [jax.ShapeDtypeStruct((b, n_exp, cap_x, d), BF16)]
    cap_z = 0
    if z is not None:
        seq_z = z["h2"].shape[1]
        cap_z = EC_CAPACITY * seq_z // n_exp
        assert b * cap_z == cap_x
        row_z = pl.BlockSpec((LANES, seq_z), lambda e, i: (0, 0))
        in_specs += [pl.BlockSpec((b, seq_z, d), lambda e, i: (0, 0, 0)), row_z, row_z]
        args += [z["h2"], z["slot_rows"], z["aff_rows"]]
        out_specs.append(pl.BlockSpec((b, 1, cap_z, d), lambda e, i: (0, e, 0, 0)))
        out_shape.append(jax.ShapeDtypeStruct((b, n_exp, cap_z, d), BF16))
    return pl.pallas_call(
        functools.partial(_expert_kernel, cap_x=cap_x, cap_z=cap_z, window=min(64, cap_x)),
        grid=(n_exp, b + (1 if z is not None else 0)),
        in_specs=in_specs + [wspec((d, ff)), wspec((d, ff)), wspec((ff, d))],
        out_specs=out_specs,
        out_shape=out_shape,
        scratch_shapes=[wbuf((d, ff)), wbuf((d, ff)), wbuf((ff, d)),
                        pltpu.VMEM((cap_x, d), F32), pltpu.VMEM((cap_x, LANES), F32)],
        compiler_params=_params("arbitrary", "arbitrary"),
        name="experts",
    )(*args, w_gate, w_up, w_down)


def _combine_kernel(starts_ref, slot_ref, ye_ref, x_ref, g2_ref, nf_ref, o_ref, *, cap, window, final):
    sample, tile = pl.program_id(0), pl.program_id(1)
    tm = x_ref.shape[1]
    per_pass = MXU_DEPTH // window
    base = tile * LANES + sample * N_EXPERTS
    first, fits = [], None
    for e in range(N_EXPERTS):
        lo, hi = starts_ref[base + e], starts_ref[base + LANES + e]
        start = jnp.minimum(lo // BF16_ROWS * BF16_ROWS, cap - window)
        first.append(start)
        ok = hi - start <= window
        fits = ok if fits is None else fits & ok
    pos = pltpu.roll(slot_ref[...], jnp.where(sample == 0, 0, LANES - sample * N_EXPERTS), 1)

    def finish(acc):
        x2 = x_ref[0] + g2_ref[0] * acc
        if final:
            x2 = x2 * lax.rsqrt(jnp.mean(x2 * x2, axis=-1, keepdims=True) + EPS) * nf_ref[...]
        o_ref[0] = x2

    @pl.when(fits)
    def _():
        lane = lax.broadcasted_iota(jnp.int32, (1, MXU_DEPTH), 1)
        rel = (lane % window).astype(F32)
        acc = jnp.zeros(x_ref.shape[1:], F32)
        for g in range(N_EXPERTS // per_pass):
            experts = range(g * per_pass, (g + 1) * per_pass)
            val = jnp.broadcast_to(pos[:, experts[0]:experts[0] + 1] - first[experts[0]].astype(F32), (tm, MXU_DEPTH))
            for k, e in enumerate(experts[1:], start=1):
                val = jnp.where(lane >= k * window, pos[:, e:e + 1] - first[e].astype(F32), val)
            rows = jnp.concatenate([ye_ref[0, e, pl.ds(pl.multiple_of(first[e], BF16_ROWS), window), :]
                                    for e in experts], axis=0)
            acc = acc + _dot(_indicator(val == rel, BF16), rows)
        finish(acc)

    @pl.when(jnp.logical_not(fits))
    def _():
        slots = lax.broadcasted_iota(jnp.int32, (tm, cap), 1).astype(F32)
        acc = jnp.zeros(x_ref.shape[1:], F32)
        for e in range(N_EXPERTS):
            acc = acc + _dot(_indicator(pos[:, e:e + 1] == slots, BF16), ye_ref[0, e])
        finish(acc)


def _combine(starts, slot_cols, ye, x1, g2, norm_f, cap, final):
    b, l, d = x1.shape
    tm = TOKEN_BLOCK
    window = min(64, cap)
    tok = lambda i, j: (i, j, 0)
    return pl.pallas_call(
        functools.partial(_combine_kernel, cap=cap, window=window, final=final),
        grid=(b, l // tm),
        in_specs=[pl.BlockSpec(memory_space=pltpu.SMEM),
                  pl.BlockSpec((tm, LANES), lambda i, j: (j, 0)),
                  pl.BlockSpec((1, N_EXPERTS, cap, d), lambda i, j: (i, 0, 0, 0)),
                  pl.BlockSpec((1, tm, d), tok), pl.BlockSpec((1, 1, d), lambda i, j: (i, 0, 0)),
                  pl.BlockSpec((1, d), lambda i, j: (0, 0))],
        out_specs=pl.BlockSpec((1, tm, d), tok),
        out_shape=jax.ShapeDtypeStruct((b, l, d), F32),
        compiler_params=_params("arbitrary", "arbitrary"),
        name="combine",
    )(starts, slot_cols, ye, x1, g2, norm_f)


def _token_mixer(x, mod, lw, states, rows, tm, heads):
    sh1, sc1 = mod[0], mod[1]
    qkv, ab, gate, pool = _in_proj(x, sh1, sc1, lw["norm1"], lw["wqkv"], lw["wab"], lw["wgate"], lw["wpool"], tm)
    o, s_f, s_b = _delta_net(qkv, ab, gate, lw["conv_w"], lw["a_log"], lw["dt_bias"], lw["dn_norm"], *states, heads)
    pooled = _pool_mixer(pool, lw["pool_w"], lw["pool_scale"], rows)
    return o, pooled, (s_f, s_b)


def _route_tokens(o, pooled, x, mod, lw, tm):
    seq = x.shape[1]
    x1, h2, aff = _out_proj(o, pooled, x, lw["w_out"], mod[2], lw["norm2"], mod[3], mod[4], lw["w_router"], tm)
    slot_cols, slot_rows, aff_rows, starts = _route(aff, EC_CAPACITY * seq // N_EXPERTS)
    return dict(x1=x1, h2=h2, slot_cols=slot_cols, slot_rows=slot_rows, aff_rows=aff_rows,
                starts=starts.astype(jnp.int32).reshape(-1), g2=mod[5])


def _scatter(r, ye, lw, final):
    cap = EC_CAPACITY * r["x1"].shape[1] // N_EXPERTS
    return _combine(r["starts"], r["slot_cols"], ye, r["x1"], r["g2"], lw["norm_f"], cap, final)


def kernel(x, c, ctx, c_ctx, w_mod, b_mod, norm1, norm2, w_in, conv_w, a_log, dt_bias, dn_norm, pool_w, pool_scale,
           w_out, w_router, w_gate, w_up, w_down, norm_f):
    batch, seq, d = x.shape
    depth = w_mod.shape[0]
    ctx_len = ctx.shape[1]
    dn_width = DN_HEADS * LANES
    qkv_cols = 3 * dn_width
    gate_cols = 2 * N_DIR * DN_HEADS
    state_cols = qkv_cols + gate_cols
    rows = seq // GRID_W
    tm_x, tm_z = 1024, ctx_len

    cond_rows = 16
    cond = jnp.zeros((cond_rows, d), F32).at[:batch].set(c).at[batch].set(c_ctx)
    mod_all = _adaln_all(cond, w_mod, b_mod)

    zero_state = jnp.zeros((batch, DN_HEADS, LANES, LANES), F32)
    z = ctx
    for l in range(depth):
        wl = w_in[l]
        lw = dict(
            norm1=norm1[l][None], norm2=norm2[l][None],
            wqkv=wl[:, :qkv_cols].astype(BF16),
            wab=jnp.pad(wl[:, qkv_cols:state_cols], ((0, 0), (0, LANES - gate_cols))).astype(BF16),
            wgate=wl[:, state_cols:state_cols + dn_width].astype(BF16),
            wpool=wl[:, state_cols + dn_width:].astype(BF16),
            conv_w=conv_w[l], a_log=a_log[l], dt_bias=dt_bias[l], dn_norm=dn_norm[l][None],
            pool_w=pool_w[l].astype(BF16), pool_scale=pool_scale[l][None],
            w_out=w_out[l].astype(BF16),
            w_router=jnp.pad(w_router[l], ((0, 0), (0, LANES - N_EXPERTS))).astype(BF16),
            norm_f=norm_f[None],
        )
        mods = mod_all[l].reshape(cond_rows, 6, d)
        mod_x = [mods[:batch, i][:, None, :] for i in range(6)]
        mod_z = [jnp.broadcast_to(mods[batch, i][None, None, :], (batch, 1, d)) for i in range(6)]

        o_z, pooled_z, ctx_states = _token_mixer(z, mod_z, lw, (zero_state, zero_state), None, tm_z, DN_HEADS)
        o_x, pooled_x, _ = _token_mixer(x, mod_x, lw, ctx_states, rows, tm_x, 1)
        rx = _route_tokens(o_x, pooled_x, x, mod_x, lw, tm_x)
        if l < depth - 1:
            rz = _route_tokens(o_z, pooled_z, z, mod_z, lw, tm_z)
            ye_x, ye_z = _experts(w_gate, w_up, w_down, l, rx, rz)
            z = _scatter(rz, ye_z, lw, False)
        else:
            ye_x, = _experts(w_gate, w_up, w_down, l, rx)
        x = _scatter(rx, ye_x, lw, l == depth - 1)
    return x
```

```python
import functools

import jax
import jax.numpy as jnp
from jax import lax
from jax.experimental import pallas as pl
from jax.experimental.pallas import tpu as pltpu

F32 = jnp.float32
BF16 = jnp.bfloat16

LANES = 128
F32_ROWS = 8
BF16_ROWS = 16
GRID_W = 64
DN_HEADS = 4
N_DIR = 2
CHUNK = 64
DN_GROUP = 16
POOL_WINDOWS = (2, 4, 8, 16)
POOL_PAD = GRID_W * max(POOL_WINDOWS) // 4
N_EXPERTS = 16
EC_CAPACITY = 2
EPS = 1e-6
VMEM_LIMIT = 56 * 1024 * 1024


def _params(*semantics):
    return pltpu.CompilerParams(dimension_semantics=semantics, vmem_limit_bytes=VMEM_LIMIT)


def _silu(x):
    return x * jax.nn.sigmoid(x)


def _dot(a, b):
    return jnp.dot(a, b, preferred_element_type=F32)


def _indicator(mask, dtype=F32):
    return jnp.where(mask, 1.0, 0.0).astype(dtype)


def _mod_kernel(cond_ref, w_ref, b_ref, o_ref):
    s = _silu(cond_ref[...])
    s_hi = s.astype(BF16)
    s_lo = (s - s_hi.astype(F32)).astype(BF16)
    w = w_ref[0]
    w_hi = w.astype(BF16)
    w_lo = (w - w_hi.astype(F32)).astype(BF16)
    rows = s.shape[0]
    both = _dot(jnp.concatenate([s_hi, s_lo], axis=0), w_hi)
    o_ref[0] = both[:rows] + both[rows:] + _dot(s_hi, w_lo) + b_ref[0]


def _adaln_all(cond, w_mod, b_mod):
    depth, d, n = w_mod.shape
    rows = cond.shape[0]
    tn = 1536
    return pl.pallas_call(
        _mod_kernel,
        grid=(depth, n // tn),
        in_specs=[pl.BlockSpec((rows, d), lambda l, j: (0, 0)),
                  pl.BlockSpec((1, d, tn), lambda l, j: (l, 0, j)),
                  pl.BlockSpec((1, 1, tn), lambda l, j: (l, 0, j))],
        out_specs=pl.BlockSpec((1, rows, tn), lambda l, j: (l, 0, j)),
        out_shape=jax.ShapeDtypeStruct((depth, rows, n), F32),
        compiler_params=_params("arbitrary", "arbitrary"),
        name="adaln",
    )(cond, w_mod, b_mod.reshape(depth, 1, n))


def _norm_mod(x, nw, shift, scale):
    ms = jnp.mean(x * x, axis=-1, keepdims=True)
    return (x * lax.rsqrt(ms + EPS) * nw) * (1.0 + scale) + shift


def _in_kernel(x_ref, sh_ref, sc_ref, nw_ref, wqkv_ref, wab_ref, wgate_ref, wpool_ref,
               qkv_ref, ab_ref, gate_ref, pool_ref):
    h = _norm_mod(x_ref[0], nw_ref[...], sh_ref[0], sc_ref[0]).astype(BF16)
    qkv = _dot(h, wqkv_ref[...])
    for c in range(qkv_ref.shape[1]):
        qkv_ref[0, c] = qkv[:, c * LANES:(c + 1) * LANES]
    gate = _dot(h, wgate_ref[...])
    for c in range(gate_ref.shape[1]):
        gate_ref[0, c] = gate[:, c * LANES:(c + 1) * LANES]
    ab_ref[0] = _dot(h, wab_ref[...])
    pool_ref[0] = _dot(h, wpool_ref[...])


def _in_proj(x, shift, scale, nw, wqkv, wab, wgate, wpool, tm):
    b, l, d = x.shape
    row = lambda i, j: (i, 0, 0)
    tok = lambda i, j: (i, j, 0)
    fixed = lambda i, j: (0, 0)
    heads = lambda w: w.shape[1] // LANES
    per_head = lambda w: pl.BlockSpec((1, heads(w), tm, LANES), lambda i, j: (i, 0, j, 0))
    return pl.pallas_call(
        _in_kernel,
        grid=(b, l // tm),
        in_specs=[pl.BlockSpec((1, tm, d), tok),
                  pl.BlockSpec((1, 1, d), row), pl.BlockSpec((1, 1, d), row),
                  pl.BlockSpec((1, d), fixed)]
                 + [pl.BlockSpec(w.shape, fixed) for w in (wqkv, wab, wgate, wpool)],
        out_specs=[per_head(wqkv), pl.BlockSpec((1, tm, wab.shape[1]), tok), per_head(wgate),
                   pl.BlockSpec((1, tm, wpool.shape[1]), tok)],
        out_shape=[jax.ShapeDtypeStruct((b, heads(wqkv), l, LANES), F32),
                   jax.ShapeDtypeStruct((b, l, wab.shape[1]), F32),
                   jax.ShapeDtypeStruct((b, heads(wgate), l, LANES), F32),
                   jax.ShapeDtypeStruct((b, l, wpool.shape[1]), F32)],
        compiler_params=_params("arbitrary", "arbitrary"),
        name="in_proj",
    )(x, shift, scale, nw, wqkv, wab, wgate, wpool)


def _bdot(a, b):
    return jnp.einsum('gij,gjk->gik', a, b, preferred_element_type=F32)


def _dn_kernel(alog_ref, dtb_ref, q_ref, k_ref, v_ref, ab_ref, gate_ref, cwq_ref, cwk_ref, cwv_ref,
               dnw_ref, s0f_ref, s0b_ref, o_ref, sf_ref, sb_ref,
               qs, ks, vs, bet_all, la_all, bet, gl_f, gl_b, la, kn_f, kn_b, bn_f, bn_b, qn_f, qn_b, on_f, on_b, pad,
               *, group):
    step = pl.program_id(1)
    _, heads, seq, _ = q_ref.shape
    n_chunks = seq // CHUNK
    lane = lax.broadcasted_iota(jnp.int32, (1, LANES), 1)

    edge = jnp.zeros((F32_ROWS, LANES), F32)
    pad[0:F32_ROWS, :] = edge
    pad[pl.ds(F32_ROWS + seq, F32_ROWS), :] = edge

    def conv_silu(x, w):
        pad[pl.ds(F32_ROWS, seq), :] = x
        y = (pad[pl.ds(F32_ROWS - 2, seq), :] * w[0:1] + pad[pl.ds(F32_ROWS - 1, seq), :] * w[1:2]
             + x * w[2:3] + pad[pl.ds(F32_ROWS + 1, seq), :] * w[3:4])
        return _silu(y)

    def l2norm(x):
        return x * lax.rsqrt(jnp.sum(x * x, axis=-1, keepdims=True) + EPS)

    @pl.when(step == 0)
    def _():
        ab = ab_ref[0]
        bet_all[...] = jax.nn.sigmoid(ab)
        la_all[...] = -jnp.exp(alog_ref[...]) * jax.nn.softplus(ab + dtb_ref[...])

    for h in range(heads):
        own = pl.ds(h * seq, seq)
        lanes = slice(h * LANES, (h + 1) * LANES)
        qs[own, :] = l2norm(conv_silu(q_ref[0, h], cwq_ref[:, lanes])) * (LANES ** -0.5)
        ks[own, :] = l2norm(conv_silu(k_ref[0, h], cwk_ref[:, lanes]))
        vs[own, :] = conv_silu(v_ref[0, h], cwv_ref[:, lanes])
        head = step * heads + h
        shift = jnp.where(head == 0, 0, LANES - head)
        bet[own, :] = pltpu.roll(bet_all[...], shift, 1)
        log_a = pltpu.roll(la_all[...], shift, 1)
        hi = log_a.astype(BF16).astype(F32)
        mid = (log_a - hi).astype(BF16).astype(F32)
        low = (log_a - hi - mid).astype(BF16).astype(F32)
        part = lane & (DN_HEADS - 1)
        la[own, :] = jnp.where(part == 0, hi, jnp.where(part == 1, pltpu.roll(mid, 1, 1),
                                                        pltpu.roll(low, 2, 1))).astype(BF16)

    ii = lax.broadcasted_iota(jnp.int32, (CHUNK, CHUNK), 0)
    jj = lax.broadcasted_iota(jnp.int32, (CHUNK, CHUNK), 1)
    eye = _indicator(ii == jj)
    span = group * CHUNK
    n_groups = heads * n_chunks // group

    def aligned(start, size):
        return pl.ds(start if isinstance(start, int) else pl.multiple_of(start, size), size)

    def wy_direction(first, direction, kn, bn, qn, on, gl):
        rows = aligned(first * CHUNK, span)
        state_rows = aligned(first * LANES, group * LANES)
        chunks = lambda a: a.reshape(group, CHUNK, a.shape[-1])
        q, k, v = chunks(qs[rows, :]), chunks(ks[rows, :]), chunks(vs[rows, :])
        log_a = chunks(la[rows, :])
        col = DN_HEADS * direction
        if direction == 0:
            incl, strict = ii >= jj, ii > jj
        else:
            incl, strict = ii <= jj, ii < jj
        sums = jnp.concatenate([_indicator(incl), jnp.ones((CHUNK, CHUNK), F32)], axis=0).astype(BF16)
        parts = jnp.concatenate([_dot(sums, log_a[c]) for c in range(group)], axis=0)
        cums = parts + pltpu.roll(parts, LANES - 1, 1) + pltpu.roll(parts, LANES - 2, 1)
        cums = cums.reshape(group, 2 * CHUNK, LANES)
        gl[rows, :] = cums[:, CHUNK:].reshape(span, LANES)
        g = cums[:, :CHUNK, col:col + 1]
        g_last = cums[:, CHUNK:, col:col + 1]
        beta = chunks(bet[rows, :])[:, :, 2 * DN_HEADS + col:2 * DN_HEADS + col + 1]
        g_cols = jnp.swapaxes(jnp.broadcast_to(g, (group, CHUNK, LANES)), 1, 2)[:, :CHUNK, :]
        decay = jnp.exp(jnp.minimum(g - g_cols, 0.0))
        e_g = jnp.exp(g)
        k_beta = k * beta
        both = jnp.einsum('gik,gjk->gij', jnp.concatenate([k_beta, q], axis=1).astype(BF16), k.astype(BF16),
                          preferred_element_type=F32)
        m = jnp.where(strict, both[:, :CHUNK] * decay, 0.0)
        qk = jnp.where(incl, both[:, CHUNK:] * decay, 0.0).astype(BF16)
        p = -m
        t_inv = eye + p
        pb = p.astype(BF16)
        p = _bdot(pb, pb)
        for _ in range(CHUNK.bit_length() - 3):
            pb = p.astype(BF16)
            r = _bdot(jnp.concatenate([pb, t_inv.astype(BF16)], axis=1), pb)
            p, t_inv = r[:, :CHUNK], t_inv + r[:, CHUNK:]
        t_inv = t_inv + _bdot(t_inv.astype(BF16), p.astype(BF16))
        wu = _bdot(t_inv.astype(BF16),
                   jnp.concatenate([k_beta * e_g, v * beta], axis=2).astype(BF16)).astype(BF16)
        k_tail = (k * jnp.exp(g_last - g)).astype(BF16)
        kb = jnp.einsum('gik,gin->gkn', k_tail, wu, preferred_element_type=F32)
        qb = _bdot(qk, wu)
        kn[state_rows, :] = kb[:, :, :LANES].reshape(group * LANES, LANES).astype(BF16)
        bn[state_rows, :] = kb[:, :, LANES:].reshape(group * LANES, LANES)
        qn[rows, :] = (q * e_g - qb[:, :, :LANES]).reshape(span, LANES).astype(BF16)
        on[rows, :] = qb[:, :, LANES:].reshape(span, LANES)

    fwd = (kn_f, bn_f, qn_f, on_f, gl_f)
    bwd = (kn_b, bn_b, qn_b, on_b, gl_b)

    def scan_chunk(n, state, direction, kn, bn, qn, on, gl):
        col = DN_HEADS * direction
        rows = aligned(n * CHUNK, CHUNK)
        state_rows = aligned(n * LANES, LANES)
        s16 = state.astype(BF16)
        on[rows, :] = on[rows, :] + _dot(qn[rows, :], s16)
        s_decay = jnp.exp(gl[aligned(n * CHUNK, F32_ROWS), :][0:1, col:col + 1])
        return state * s_decay - _dot(kn[state_rows, :], s16) + bn[state_rows, :]

    def scan_step(i, states):
        out = []
        for h in range(heads):
            out.append(scan_chunk(h * n_chunks + i, states[2 * h], 0, *fwd))
            out.append(scan_chunk(h * n_chunks + n_chunks - 1 - i, states[2 * h + 1], 1, *bwd))
        return tuple(out)

    states = tuple(ref[0, h] for h in range(heads) for ref in (s0f_ref, s0b_ref))
    if heads == 1 and n_groups == 2:
        wy_direction(0, 0, *fwd)
        wy_direction(group, 1, *bwd)
        wy_direction(group, 0, *fwd)
        wy_direction(0, 1, *bwd)
        for i in range(group):
            states = scan_step(i, states)
        final = lax.fori_loop(group, n_chunks, scan_step, states)
    else:
        def wy_group(gi, carry):
            wy_direction(gi * group, 0, *fwd)
            wy_direction(gi * group, 1, *bwd)
            return carry

        lax.fori_loop(0, n_groups, wy_group, 0)
        final = lax.fori_loop(0, n_chunks, scan_step, states)
    for h in range(heads):
        sf_ref[0, h] = final[2 * h]
        sb_ref[0, h] = final[2 * h + 1]
        own = pl.ds(h * seq, seq)
        o = on_f[own, :] + on_b[own, :]
        y = o * lax.rsqrt(jnp.mean(o * o, axis=-1, keepdims=True) + EPS) * dnw_ref[...]
        o_ref[0, h] = (y * _silu(gate_ref[0, h])).astype(BF16)


def _delta_net(qkv, ab, gate, conv_w, a_log, dt_bias, dn_norm, s0f, s0b, heads):
    b, _, seq, hd = qkv.shape
    lane_vec = lambda a: jnp.pad(a.reshape(1, -1), ((0, 0), (0, LANES - a.size)))
    vec = pl.BlockSpec((1, LANES), lambda i, s: (0, 0))
    part = DN_HEADS // heads
    per_head = lambda off: pl.BlockSpec((1, heads, seq, hd), lambda i, s: (i, off * part + s, 0, 0))
    cw = lambda off: pl.BlockSpec((conv_w.shape[0], heads * hd), lambda i, s: (0, off * part + s))
    st = pl.BlockSpec((1, heads, hd, hd), lambda i, s: (i, s, 0, 0))
    rows = heads * seq
    group = min(DN_GROUP, rows // CHUNK)
    shared_f32 = pltpu.VMEM((seq, hd), F32)
    rows_f32 = pltpu.VMEM((rows, hd), F32)
    rows_bf16 = pltpu.VMEM((rows, hd), BF16)
    mats_f32 = pltpu.VMEM((rows // CHUNK * hd, hd), F32)
    mats_bf16 = pltpu.VMEM((rows // CHUNK * hd, hd), BF16)
    return pl.pallas_call(
        functools.partial(_dn_kernel, group=group),
        grid=(b, part),
        in_specs=[vec, vec, per_head(0), per_head(1), per_head(2),
                  pl.BlockSpec((1, seq, LANES), lambda i, s: (i, 0, 0)),
                  per_head(0), cw(0), cw(1), cw(2),
                  pl.BlockSpec((1, hd), lambda i, s: (0, 0)), st, st],
        out_specs=[per_head(0), st, st],
        out_shape=[jax.ShapeDtypeStruct((b, DN_HEADS, seq, hd), BF16),
                   jax.ShapeDtypeStruct((b, DN_HEADS, hd, hd), F32),
                   jax.ShapeDtypeStruct((b, DN_HEADS, hd, hd), F32)],
        scratch_shapes=[rows_f32] * 3 + [shared_f32] * 2 + [rows_f32] * 3 + [rows_bf16]
                       + [mats_bf16] * 2 + [mats_f32] * 2 + [rows_bf16] * 2 + [rows_f32] * 2
                       + [pltpu.VMEM((seq + 2 * F32_ROWS, hd), F32)],
        compiler_params=_params("arbitrary", "arbitrary"),
        name="delta_net",
    )(lane_vec(a_log), lane_vec(dt_bias), qkv, qkv, qkv, ab, gate, conv_w, conv_w, conv_w, dn_norm, s0f, s0b)


def _window_sum(x, pos, limit, half, stride, bufs):
    seq = x.shape[0]
    body = pl.ds(POOL_PAD, seq)
    shifted = lambda ref, off: ref[pl.ds(POOL_PAD + off, seq), :]
    keep = (lambda ok, v: jnp.where(ok, v, 0.0)) if stride == 1 else (lambda ok, v: v)
    a, l, r = bufs
    a[body, :] = x
    left = keep(pos >= 1, shifted(a, -stride))
    right = x
    k = 1
    while k < half:
        l[body, :] = left
        r[body, :] = right
        left = left + keep(pos >= k, shifted(l, -k * stride))
        right = right + keep(pos + k < limit, shifted(r, k * stride))
        k *= 2
    return left + right


def _window_count(pos, limit, half):
    return (jnp.minimum(pos + half, limit) - jnp.maximum(pos - half, 0)).astype(F32)


def _pool_kernel(u_ref, pw_ref, ps_ref, o_ref, *bufs, rows):
    seq = u_ref.shape[1]
    t = lax.broadcasted_iota(jnp.int32, (seq, LANES), 0)
    border = jnp.zeros((POOL_PAD, LANES), F32)
    for buf in bufs:
        buf[0:POOL_PAD, :] = border
        buf[pl.ds(POOL_PAD + seq, POOL_PAD), :] = border
    for g, w in enumerate(POOL_WINDOWS):
        lanes = slice(g * LANES, (g + 1) * LANES)
        x = u_ref[0, :, lanes]
        half = w // 2
        if rows is None:
            total = _window_sum(x, t, seq, half, 1, bufs)
            count = _window_count(t, seq, half)
        else:
            c, r = t & (GRID_W - 1), t >> (GRID_W.bit_length() - 1)
            total = _window_sum(_window_sum(x, c, GRID_W, half, 1, bufs), r, rows, half, GRID_W, bufs)
            in_cols = _window_count(lax.broadcasted_iota(jnp.int32, (1, GRID_W, LANES), 1), GRID_W, half)
            in_rows = _window_count(lax.broadcasted_iota(jnp.int32, (rows, 1, LANES), 0), rows, half)
            count = (in_rows * in_cols).reshape(seq, LANES)
        m = (total / count - x).astype(BF16)
        o_ref[0, :, lanes] = (_dot(m, pw_ref[g]) * ps_ref[:, lanes]).astype(BF16)


def _pool_mixer(u, pool_w, pool_scale, rows):
    b, seq, width = u.shape
    blk = pl.BlockSpec((1, seq, width), lambda i: (i, 0, 0))
    return pl.pallas_call(
        functools.partial(_pool_kernel, rows=rows),
        grid=(b,),
        in_specs=[blk, pl.BlockSpec(pool_w.shape, lambda i: (0, 0, 0)), pl.BlockSpec((1, width), lambda i: (0, 0))],
        out_specs=blk,
        out_shape=jax.ShapeDtypeStruct((b, seq, width), BF16),
        scratch_shapes=[pltpu.VMEM((seq + 2 * POOL_PAD, LANES), F32)] * 3,
        compiler_params=_params("arbitrary"),
        name="pool_mixer",
    )(u, pool_w, pool_scale)


def _out_kernel(o_ref, p_ref, x_ref, wo_ref, g1_ref, nw_ref, sh_ref, sc_ref, wr_ref, x1_ref, h2_ref, aff_ref):
    sample = pl.program_id(1)
    o = jnp.concatenate([o_ref[0, h] for h in range(o_ref.shape[1])], axis=1)
    half = o.shape[1]
    y = _dot(o, wo_ref[:half, :]) + _dot(p_ref[0], wo_ref[half:, :])
    x1 = x_ref[0] + g1_ref[0] * y
    x1_ref[0] = x1
    h2 = _norm_mod(x1, nw_ref[...], sh_ref[0], sc_ref[0]).astype(BF16)
    h2_ref[0] = h2
    lane = lax.broadcasted_iota(jnp.int32, (1, LANES), 1)
    logits = jnp.where(lane < N_EXPERTS, _dot(h2, wr_ref[...]), -jnp.inf)
    ex = jnp.exp(logits - jnp.max(logits, axis=-1, keepdims=True))
    aff = pltpu.roll(ex / jnp.sum(ex, axis=-1, keepdims=True), sample * N_EXPERTS, 1)

    @pl.when(sample == 0)
    def _():
        aff_ref[...] = aff

    @pl.when(sample > 0)
    def _():
        aff_ref[...] += aff


def _out_proj(o, pooled, x, w_out, g1, nw, shift, scale, w_router, tm):
    b, l, d = x.shape
    assert b * N_EXPERTS <= LANES
    half = pooled.shape[2]
    row = lambda j, i: (i, 0, 0)
    tok = lambda j, i: (i, j, 0)
    fixed = lambda j, i: (0, 0)
    vec = pl.BlockSpec((1, 1, d), row)
    return pl.pallas_call(
        _out_kernel,
        grid=(l // tm, b),
        in_specs=[pl.BlockSpec((1, o.shape[1], tm, LANES), lambda j, i: (i, 0, j, 0)),
                  pl.BlockSpec((1, tm, half), tok), pl.BlockSpec((1, tm, d), tok),
                  pl.BlockSpec(w_out.shape, fixed), vec, pl.BlockSpec((1, d), fixed), vec, vec,
                  pl.BlockSpec(w_router.shape, fixed)],
        out_specs=[pl.BlockSpec((1, tm, d), tok), pl.BlockSpec((1, tm, d), tok),
                   pl.BlockSpec((tm, LANES), lambda j, i: (j, 0))],
        out_shape=[jax.ShapeDtypeStruct((b, l, d), F32), jax.ShapeDtypeStruct((b, l, d), BF16),
                   jax.ShapeDtypeStruct((l, LANES), F32)],
        compiler_params=_params("arbitrary", "arbitrary"),
        name="out_proj",
    )(o, pooled, x, w_out, g1, nw, shift, scale, w_router)


TOKEN_BLOCK = 256
COMBINE_TILES = 4
MXU_DEPTH = 256


def _route_kernel(aff_ref, slotc_ref, slotr_ref, affr_ref, starts_ref, *, cap):
    seq = aff_ref.shape[0]
    aff = aff_ref[...]

    def bit_step(it, lo_bits):
        cand_bits = lo_bits | jnp.left_shift(jnp.int32(1), 30 - it)
        cand = lax.bitcast_convert_type(cand_bits, F32)
        count = jnp.sum(_indicator(aff >= cand), axis=0, keepdims=True)
        return jnp.where(count >= cap, cand_bits, lo_bits)

    lo_bits = lax.fori_loop(0, 31, bit_step, jnp.zeros((1, LANES), jnp.int32))
    lo = lax.bitcast_convert_type(lo_bits, F32)
    hi = lax.bitcast_convert_type(lo_bits + 1, F32)
    above = aff >= hi
    tied = (aff >= lo) & (aff < hi)
    need = cap - jnp.sum(_indicator(above), axis=0, keepdims=True)
    flags = jnp.concatenate([_indicator(above), _indicator(tied)], axis=1)

    r = lax.broadcasted_iota(jnp.int32, (TOKEN_BLOCK, TOKEN_BLOCK), 0)
    c = lax.broadcasted_iota(jnp.int32, (TOKEN_BLOCK, TOKEN_BLOCK), 1)
    tri = _indicator(r > c, BF16)
    offset = jnp.zeros((1, 2 * LANES), F32)
    pieces = []
    for j in range(seq // TOKEN_BLOCK):
        blk = flags[j * TOKEN_BLOCK:(j + 1) * TOKEN_BLOCK]
        pieces.append(_dot(tri, blk.astype(BF16)) + offset)
        offset = offset + jnp.sum(blk, axis=0, keepdims=True)
    before = jnp.concatenate(pieces, axis=0)
    tied_before = before[:, LANES:]
    rank = before[:, :LANES] + jnp.minimum(tied_before, need)
    chosen = above | (tied & (tied_before < need))
    slot = jnp.where(chosen, rank, -1.0)
    slotc_ref[...] = slot
    slotr_ref[...] = slot.T
    affr_ref[...] = aff.T
    n_blocks = seq // TOKEN_BLOCK
    for j in range(n_blocks):
        starts_ref[j:j + 1, :] = rank[j * TOKEN_BLOCK:j * TOKEN_BLOCK + 1, :]
    starts_ref[n_blocks:, :] = jnp.full((starts_ref.shape[0] - n_blocks, LANES), cap, F32)


def _route(aff, cap):
    seq = aff.shape[0]
    start_rows = -(-(seq // TOKEN_BLOCK + 1) // 8) * 8
    return pl.pallas_call(
        functools.partial(_route_kernel, cap=cap),
        out_shape=[jax.ShapeDtypeStruct((seq, LANES), F32), jax.ShapeDtypeStruct((LANES, seq), F32),
                   jax.ShapeDtypeStruct((LANES, seq), F32), jax.ShapeDtypeStruct((start_rows, LANES), F32)],
        compiler_params=pltpu.CompilerParams(vmem_limit_bytes=VMEM_LIMIT),
        name="route",
    )(aff)


def _affinity_of(hit, aff):
    return jnp.broadcast_to(jnp.sum(jnp.where(hit, aff, 0.0), axis=1, keepdims=True), (hit.shape[0], LANES))


def _gather_full(h_ref, pos_ref, aff_ref, row0, xe_s, w_s, cap):
    samples, seq, _ = h_ref.shape
    slots = lax.broadcasted_iota(jnp.int32, (cap, seq), 0).astype(F32)
    for s in range(samples):
        row = pl.ds(row0 + s * N_EXPERTS, 1)
        hit = slots == pos_ref[row, :]
        xe_s[s * cap:(s + 1) * cap, :] = _dot(_indicator(hit, BF16), h_ref[s])
        w_s[s * cap:(s + 1) * cap, :] = _affinity_of(hit, aff_ref[row, :])


def _gather_blocks(starts_ref, base, h_ref, pos_ref, aff_ref, row0, xe_s, w_s, cap, window):
    seq = h_ref.shape[1]
    n_blocks = seq // TOKEN_BLOCK
    if n_blocks == 1 or window >= cap:
        _gather_full(h_ref, pos_ref, aff_ref, row0, xe_s, w_s, cap)
        return
    first, fits = [], None
    for j in range(n_blocks):
        lo, hi = starts_ref[j * LANES + base], starts_ref[(j + 1) * LANES + base]
        start = jnp.minimum(lo // F32_ROWS * F32_ROWS, cap - window)
        first.append(start)
        ok = hi - start <= window
        fits = ok if fits is None else fits & ok

    @pl.when(fits)
    def _():
        xe_s[...] = jnp.zeros(xe_s.shape, F32)
        w_s[...] = jnp.zeros(w_s.shape, F32)
        rel = lax.broadcasted_iota(jnp.int32, (window, TOKEN_BLOCK), 0).astype(F32)
        pos, aff = pos_ref[pl.ds(row0, 1), :], aff_ref[pl.ds(row0, 1), :]
        for j in range(n_blocks):
            tokens = slice(j * TOKEN_BLOCK, (j + 1) * TOKEN_BLOCK)
            hit = rel + first[j].astype(F32) == pos[:, tokens]
            rows = pl.ds(pl.multiple_of(first[j], F32_ROWS), window)
            xe_s[rows, :] += _dot(_indicator(hit, BF16), h_ref[0, tokens, :])
            w_s[rows, :] += _affinity_of(hit, aff[:, tokens])

    @pl.when(jnp.logical_not(fits))
    def _():
        _gather_full(h_ref, pos_ref, aff_ref, row0, xe_s, w_s, cap)


def _expert_kernel(starts_ref, hx_ref, px_ref, ax_ref, *refs, cap_x, cap_z, window):
    if cap_z:
        hz_ref, pz_ref, az_ref, wg_ref, wu_ref, wd_ref, yx_ref, yz_ref, wg_s, wu_s, wd_s, xe_s, w_s = refs
    else:
        wg_ref, wu_ref, wd_ref, yx_ref, wg_s, wu_s, wd_s, xe_s, w_s = refs
    expert, step = pl.program_id(0), pl.program_id(1)
    n_latent = pl.num_programs(1) - (1 if cap_z else 0)

    @pl.when(step == 0)
    def _():
        wg_s[...] = wg_ref[0, 0].astype(BF16)
        wu_s[...] = wu_ref[0, 0].astype(BF16)
        wd_s[...] = wd_ref[0, 0].astype(BF16)

    @pl.when(step < n_latent)
    def _():
        lane = step * N_EXPERTS + expert
        _gather_blocks(starts_ref, lane, hx_ref, px_ref, ax_ref, lane % F32_ROWS, xe_s, w_s, cap_x, window)

    if cap_z:
        @pl.when(step == n_latent)
        def _():
            _gather_full(hz_ref, pz_ref, az_ref, expert, xe_s, w_s, cap_z)

    xe = xe_s[...].astype(BF16)
    hid = (_silu(_dot(xe, wg_s[...])) * _dot(xe, wu_s[...])).astype(BF16)
    ye = (_dot(hid, wd_s[...]) * w_s[:, 0:1]).astype(BF16)

    @pl.when(step < n_latent)
    def _():
        yx_ref[0, 0] = ye

    if cap_z:
        @pl.when(step == n_latent)
        def _():
            for s in range(yz_ref.shape[0]):
                yz_ref[s, 0] = ye[s * cap_z:(s + 1) * cap_z]


def _experts(w_gate, w_up, w_down, layer, x, z=None):
    b, seq_x, d = x["h2"].shape
    _, n_exp, _, ff = w_gate.shape
    cap_x = EC_CAPACITY * seq_x // n_exp
    sample = lambda i: jnp.minimum(i, b - 1)
    wspec = lambda shape: pl.BlockSpec((1, 1) + shape, lambda e, i: (layer, e, 0, 0))
    wbuf = lambda shape: pltpu.VMEM(shape, BF16)
    row_x = pl.BlockSpec((F32_ROWS, seq_x), lambda e, i: ((sample(i) * n_exp + e) // F32_ROWS, 0))
    in_specs = [pl.BlockSpec(memory_space=pltpu.SMEM),
                pl.BlockSpec((1, seq_x, d), lambda e, i: (sample(i), 0, 0)), row_x, row_x]
    args = [x["starts"], x["h2"], x["slot_rows"], x["aff_rows"]]
    out_specs = [pl.BlockSpec((1, 1, cap_x, d), lambda e, i: (sample(i), e, 0, 0))]
    out_shape = [jax.ShapeDtypeStruct((b, n_exp, cap_x, d), BF16)]
    cap_z = 0
    if z is not None:
        seq_z = z["h2"].shape[1]
        cap_z = EC_CAPACITY * seq_z // n_exp
        assert b * cap_z == cap_x
        row_z = pl.BlockSpec((LANES, seq_z), lambda e, i: (0, 0))
        in_specs += [pl.BlockSpec((b, seq_z, d), lambda e, i: (0, 0, 0)), row_z, row_z]
        args += [z["h2"], z["slot_rows"], z["aff_rows"]]
        out_specs.append(pl.BlockSpec((b, 1, cap_z, d), lambda e, i: (0, e, 0, 0)))
        out_shape.append(jax.ShapeDtypeStruct((b, n_exp, cap_z, d), BF16))
    return pl.pallas_call(
        functools.partial(_expert_kernel, cap_x=cap_x, cap_z=cap_z, window=min(64, cap_x)),
        grid=(n_exp, b + (1 if z is not None else 0)),
        in_specs=in_specs + [wspec((d, ff)), wspec((d, ff)), wspec((ff, d))],
        out_specs=out_specs,
        out_shape=out_shape,
        scratch_shapes=[wbuf((d, ff)), wbuf((d, ff)), wbuf((ff, d)),
                        pltpu.VMEM((cap_x, d), F32), pltpu.VMEM((cap_x, LANES), F32)],
        compiler_params=_params("arbitrary", "arbitrary"),
        name="experts",
    )(*args, w_gate, w_up, w_down)


def _combine_kernel(starts_ref, slot_ref, ye_ref, x_ref, g2_ref, nf_ref, o_ref, *, cap, window, final):
    sample, step = pl.program_id(0), pl.program_id(1)
    tm = TOKEN_BLOCK
    per_pass = MXU_DEPTH // window
    shift = jnp.where(sample == 0, 0, LANES - sample * N_EXPERTS)
    for t in range(x_ref.shape[1] // tm):
        tokens = slice(t * tm, (t + 1) * tm)
        tile = step * (x_ref.shape[1] // tm) + t
        base = tile * LANES + sample * N_EXPERTS
        first, fits = [], None
        for e in range(N_EXPERTS):
            lo, hi = starts_ref[base + e], starts_ref[base + LANES + e]
            start = jnp.minimum(lo // BF16_ROWS * BF16_ROWS, cap - window)
            first.append(start)
            ok = hi - start <= window
            fits = ok if fits is None else fits & ok
        pos = pltpu.roll(slot_ref[tokens, :], shift, 1)

        def finish(acc, tokens=tokens):
            x2 = x_ref[0, tokens, :] + g2_ref[0] * acc
            if final:
                x2 = x2 * lax.rsqrt(jnp.mean(x2 * x2, axis=-1, keepdims=True) + EPS) * nf_ref[...]
            o_ref[0, tokens, :] = x2

        @pl.when(fits)
        def _(pos=pos, first=first, finish=finish):
            lane = lax.broadcasted_iota(jnp.int32, (1, MXU_DEPTH), 1)
            rel = (lane % window).astype(F32)
            acc = jnp.zeros((tm, x_ref.shape[2]), F32)
            for g in range(N_EXPERTS // per_pass):
                experts = range(g * per_pass, (g + 1) * per_pass)
                val = jnp.broadcast_to(pos[:, experts[0]:experts[0] + 1] - first[experts[0]].astype(F32),
                                       (tm, MXU_DEPTH))
                for k, e in enumerate(experts[1:], start=1):
                    val = jnp.where(lane >= k * window, pos[:, e:e + 1] - first[e].astype(F32), val)
                rows = jnp.concatenate([ye_ref[0, e, pl.ds(pl.multiple_of(first[e], BF16_ROWS), window), :]
                                        for e in experts], axis=0)
                acc = acc + _dot(_indicator(val == rel, BF16), rows)
            finish(acc)

        @pl.when(jnp.logical_not(fits))
        def _(pos=pos, finish=finish):
            slots = lax.broadcasted_iota(jnp.int32, (tm, cap), 1).astype(F32)
            acc = jnp.zeros((tm, x_ref.shape[2]), F32)
            for e in range(N_EXPERTS):
                acc = acc + _dot(_indicator(pos[:, e:e + 1] == slots, BF16), ye_ref[0, e])
            finish(acc)


def _combine(starts, slot_cols, ye, x1, g2, norm_f, cap, final):
    b, l, d = x1.shape
    tm = min(COMBINE_TILES * TOKEN_BLOCK, l)
    window = min(64, cap)
    tok = lambda i, j: (i, j, 0)
    return pl.pallas_call(
        functools.partial(_combine_kernel, cap=cap, window=window, final=final),
        grid=(b, l // tm),
        in_specs=[pl.BlockSpec(memory_space=pltpu.SMEM),
                  pl.BlockSpec((tm, LANES), lambda i, j: (j, 0)),
                  pl.BlockSpec((1, N_EXPERTS, cap, d), lambda i, j: (i, 0, 0, 0)),
                  pl.BlockSpec((1, tm, d), tok), pl.BlockSpec((1, 1, d), lambda i, j: (i, 0, 0)),
                  pl.BlockSpec((1, d), lambda i, j: (0, 0))],
        out_specs=pl.BlockSpec((1, tm, d), tok),
        out_shape=jax.ShapeDtypeStruct((b, l, d), F32),
        compiler_params=_params("arbitrary", "arbitrary"),
        name="combine",
    )(starts, slot_cols, ye, x1, g2, norm_f)


def _token_mixer(x, mod, lw, states, rows, tm, heads):
    sh1, sc1 = mod[0], mod[1]
    qkv, ab, gate, pool = _in_proj(x, sh1, sc1, lw["norm1"], lw["wqkv"], lw["wab"], lw["wgate"], lw["wpool"], tm)
    o, s_f, s_b = _delta_net(qkv, ab, gate, lw["conv_w"], lw["a_log"], lw["dt_bias"], lw["dn_norm"], *states, heads)
    pooled = _pool_mixer(pool, lw["pool_w"], lw["pool_scale"], rows)
    return o, pooled, (s_f, s_b)


def _route_tokens(o, pooled, x, mod, lw, tm):
    seq = x.shape[1]
    x1, h2, aff = _out_proj(o, pooled, x, lw["w_out"], mod[2], lw["norm2"], mod[3], mod[4], lw["w_router"], tm)
    slot_cols, slot_rows, aff_rows, starts = _route(aff, EC_CAPACITY * seq // N_EXPERTS)
    return dict(x1=x1, h2=h2, slot_cols=slot_cols, slot_rows=slot_rows, aff_rows=aff_rows,
                starts=starts.astype(jnp.int32).reshape(-1), g2=mod[5])


def _scatter(r, ye, lw, final):
    cap = EC_CAPACITY * r["x1"].shape[1] // N_EXPERTS
    return _combine(r["starts"], r["slot_cols"], ye, r["x1"], r["g2"], lw["norm_f"], cap, final)


def kernel(x, c, ctx, c_ctx, w_mod, b_mod, norm1, norm2, w_in, conv_w, a_log, dt_bias, dn_norm, pool_w, pool_scale,
           w_out, w_router, w_gate, w_up, w_down, norm_f):
    batch, seq, d = x.shape
    depth = w_mod.shape[0]
    ctx_len = ctx.shape[1]
    dn_width = DN_HEADS * LANES
    qkv_cols = 3 * dn_width
    gate_cols = 2 * N_DIR * DN_HEADS
    state_cols = qkv_cols + gate_cols
    rows = seq // GRID_W
    tm_x, tm_z = 1024, ctx_len

    cond_rows = 16
    cond = jnp.zeros((cond_rows, d), F32).at[:batch].set(c).at[batch].set(c_ctx)
    mod_all = _adaln_all(cond, w_mod, b_mod)

    zero_state = jnp.zeros((batch, DN_HEADS, LANES, LANES), F32)
    z = ctx
    for l in range(depth):
        wl = w_in[l]
        lw = dict(
            norm1=norm1[l][None], norm2=norm2[l][None],
            wqkv=wl[:, :qkv_cols].astype(BF16),
            wab=jnp.pad(wl[:, qkv_cols:state_cols], ((0, 0), (0, LANES - gate_cols))).astype(BF16),
            wgate=wl[:, state_cols:state_cols + dn_width].astype(BF16),
            wpool=wl[:, state_cols + dn_width:].astype(BF16),
            conv_w=conv_w[l], a_log=a_log[l], dt_bias=dt_bias[l], dn_norm=dn_norm[l][None],
            pool_w=pool_w[l].astype(BF16), pool_scale=pool_scale[l][None],
            w_out=w_out[l].astype(BF16),
            w_router=jnp.pad(w_router[l], ((0, 0), (0, LANES - N_EXPERTS))).astype(BF16),
            norm_f=norm_f[None],
        )
        mods = mod_all[l].reshape(cond_rows, 6, d)
        mod_x = [mods[:batch, i][:, None, :] for i in range(6)]
        mod_z = [jnp.broadcast_to(mods[batch, i][None, None, :], (batch, 1, d)) for i in range(6)]

        o_z, pooled_z, ctx_states = _token_mixer(z, mod_z, lw, (zero_state, zero_state), None, tm_z, DN_HEADS)
        o_x, pooled_x, _ = _token_mixer(x, mod_x, lw, ctx_states, rows, tm_x, 1)
        rx = _route_tokens(o_x, pooled_x, x, mod_x, lw, tm_x)
        if l < depth - 1:
            rz = _route_tokens(o_z, pooled_z, z, mod_z, lw, tm_z)
            ye_x, ye_z = _experts(w_gate, w_up, w_down, l, rx, rz)
            z = _scatter(rz, ye_z, lw, False)
        else:
            ye_x, = _experts(w_gate, w_up, w_down, l, rx)
        x = _scatter(rx, ye_x, lw, l == depth - 1)
    return x
```

```python
import functools

import jax
import jax.numpy as jnp
from jax import lax
from jax.experimental import pallas as pl
from jax.experimental.pallas import tpu as pltpu

F32 = jnp.float32
BF16 = jnp.bfloat16

LANES = 128
F32_ROWS = 8
BF16_ROWS = 16
GRID_W = 64
DN_HEADS = 4
N_DIR = 2
CHUNK = 64
DN_GROUP = 16
POOL_WINDOWS = (2, 4, 8, 16)
POOL_PAD = GRID_W * max(POOL_WINDOWS) // 4
N_EXPERTS = 16
EC_CAPACITY = 2
EPS = 1e-6
VMEM_LIMIT = 56 * 1024 * 1024


def _params(*semantics):
    return pltpu.CompilerParams(dimension_semantics=semantics, vmem_limit_bytes=VMEM_LIMIT)


def _silu(x):
    return x * jax.nn.sigmoid(x)


def _dot(a, b):
    return jnp.dot(a, b, preferred_element_type=F32)


def _indicator(mask, dtype=F32):
    return jnp.where(mask, 1.0, 0.0).astype(dtype)


def _mod_kernel(cond_ref, w_ref, b_ref, o_ref):
    s = _silu(cond_ref[...])
    s_hi = s.astype(BF16)
    s_lo = (s - s_hi.astype(F32)).astype(BF16)
    w = w_ref[0]
    w_hi = w.astype(BF16)
    w_lo = (w - w_hi.astype(F32)).astype(BF16)
    rows = s.shape[0]
    both = _dot(jnp.concatenate([s_hi, s_lo], axis=0), w_hi)
    o_ref[0] = both[:rows] + both[rows:] + _dot(s_hi, w_lo) + b_ref[0]


def _adaln_all(cond, w_mod, b_mod):
    depth, d, n = w_mod.shape
    rows = cond.shape[0]
    tn = 1536
    return pl.pallas_call(
        _mod_kernel,
        grid=(depth, n // tn),
        in_specs=[pl.BlockSpec((rows, d), lambda l, j: (0, 0)),
                  pl.BlockSpec((1, d, tn), lambda l, j: (l, 0, j)),
                  pl.BlockSpec((1, 1, tn), lambda l, j: (l, 0, j))],
        out_specs=pl.BlockSpec((1, rows, tn), lambda l, j: (l, 0, j)),
        out_shape=jax.ShapeDtypeStruct((depth, rows, n), F32),
        compiler_params=_params("arbitrary", "arbitrary"),
        name="adaln",
    )(cond, w_mod, b_mod.reshape(depth, 1, n))


def _norm_mod(x, nw, shift, scale):
    ms = jnp.mean(x * x, axis=-1, keepdims=True)
    return (x * lax.rsqrt(ms + EPS) * nw) * (1.0 + scale) + shift


def _in_kernel(x_ref, sh_ref, sc_ref, nw_ref, wqkv_ref, wab_ref, wgate_ref, wpool_ref,
               qkv_ref, ab_ref, gate_ref, pool_ref):
    h = _norm_mod(x_ref[0], nw_ref[...], sh_ref[0], sc_ref[0]).astype(BF16)
    qkv = _dot(h, wqkv_ref[...])
    for c in range(qkv_ref.shape[1]):
        qkv_ref[0, c] = qkv[:, c * LANES:(c + 1) * LANES]
    gate = _dot(h, wgate_ref[...])
    for c in range(gate_ref.shape[1]):
        gate_ref[0, c] = gate[:, c * LANES:(c + 1) * LANES]
    ab_ref[0] = _dot(h, wab_ref[...])
    pool_ref[0] = _dot(h, wpool_ref[...])


def _in_proj(x, shift, scale, nw, wqkv, wab, wgate, wpool, tm):
    b, l, d = x.shape
    row = lambda i, j: (i, 0, 0)
    tok = lambda i, j: (i, j, 0)
    fixed = lambda i, j: (0, 0)
    heads = lambda w: w.shape[1] // LANES
    per_head = lambda w: pl.BlockSpec((1, heads(w), tm, LANES), lambda i, j: (i, 0, j, 0))
    return pl.pallas_call(
        _in_kernel,
        grid=(b, l // tm),
        in_specs=[pl.BlockSpec((1, tm, d), tok),
                  pl.BlockSpec((1, 1, d), row), pl.BlockSpec((1, 1, d), row),
                  pl.BlockSpec((1, d), fixed)]
                 + [pl.BlockSpec(w.shape, fixed) for w in (wqkv, wab, wgate, wpool)],
        out_specs=[per_head(wqkv), pl.BlockSpec((1, tm, wab.shape[1]), tok), per_head(wgate),
                   pl.BlockSpec((1, tm, wpool.shape[1]), tok)],
        out_shape=[jax.ShapeDtypeStruct((b, heads(wqkv), l, LANES), F32),
                   jax.ShapeDtypeStruct((b, l, wab.shape[1]), F32),
                   jax.ShapeDtypeStruct((b, heads(wgate), l, LANES), F32),
                   jax.ShapeDtypeStruct((b, l, wpool.shape[1]), F32)],
        compiler_params=_params("arbitrary", "arbitrary"),
        name="in_proj",
    )(x, shift, scale, nw, wqkv, wab, wgate, wpool)


def _bdot(a, b):
    return jnp.einsum('gij,gjk->gik', a, b, preferred_element_type=F32)


def _dn_kernel(alog_ref, dtb_ref, q_ref, k_ref, v_ref, ab_ref, gate_ref, cwq_ref, cwk_ref, cwv_ref,
               dnw_ref, s0f_ref, s0b_ref, o_ref, sf_ref, sb_ref,
               qs, ks, vs, bet_all, la_all, bet, gl_f, gl_b, la, kn_f, kn_b, bn_f, bn_b, qn_f, qn_b, on_f, on_b, pad,
               *, group):
    step = pl.program_id(1)
    _, heads, seq, _ = q_ref.shape
    n_chunks = seq // CHUNK
    lane = lax.broadcasted_iota(jnp.int32, (1, LANES), 1)

    edge = jnp.zeros((F32_ROWS, LANES), F32)
    pad[0:F32_ROWS, :] = edge
    pad[pl.ds(F32_ROWS + seq, F32_ROWS), :] = edge

    def conv_silu(x, w):
        pad[pl.ds(F32_ROWS, seq), :] = x
        y = (pad[pl.ds(F32_ROWS - 2, seq), :] * w[0:1] + pad[pl.ds(F32_ROWS - 1, seq), :] * w[1:2]
             + x * w[2:3] + pad[pl.ds(F32_ROWS + 1, seq), :] * w[3:4])
        return _silu(y)

    def l2norm(x):
        return x * lax.rsqrt(jnp.sum(x * x, axis=-1, keepdims=True) + EPS)

    @pl.when(step == 0)
    def _():
        ab = ab_ref[0]
        bet_all[...] = jax.nn.sigmoid(ab)
        la_all[...] = -jnp.exp(alog_ref[...]) * jax.nn.softplus(ab + dtb_ref[...])

    for h in range(heads):
        own = pl.ds(h * seq, seq)
        lanes = slice(h * LANES, (h + 1) * LANES)
        qs[own, :] = l2norm(conv_silu(q_ref[0, h], cwq_ref[:, lanes])) * (LANES ** -0.5)
        ks[own, :] = l2norm(conv_silu(k_ref[0, h], cwk_ref[:, lanes]))
        vs[own, :] = conv_silu(v_ref[0, h], cwv_ref[:, lanes])
        head = step * heads + h
        shift = jnp.where(head == 0, 0, LANES - head)
        bet[own, :] = pltpu.roll(bet_all[...], shift, 1)
        log_a = pltpu.roll(la_all[...], shift, 1)
        hi = log_a.astype(BF16).astype(F32)
        mid = (log_a - hi).astype(BF16).astype(F32)
        low = (log_a - hi - mid).astype(BF16).astype(F32)
        part = lane & (DN_HEADS - 1)
        la[own, :] = jnp.where(part == 0, hi, jnp.where(part == 1, pltpu.roll(mid, 1, 1),
                                                        pltpu.roll(low, 2, 1))).astype(BF16)

    ii = lax.broadcasted_iota(jnp.int32, (CHUNK, CHUNK), 0)
    jj = lax.broadcasted_iota(jnp.int32, (CHUNK, CHUNK), 1)
    eye = _indicator(ii == jj)
    span = group * CHUNK
    n_groups = heads * n_chunks // group

    def aligned(start, size):
        return pl.ds(start if isinstance(start, int) else pl.multiple_of(start, size), size)

    def wy_direction(first, direction, kn, bn, qn, on, gl):
        rows = aligned(first * CHUNK, span)
        state_rows = aligned(first * LANES, group * LANES)
        chunks = lambda a: a.reshape(group, CHUNK, a.shape[-1])
        q, k, v = chunks(qs[rows, :]), chunks(ks[rows, :]), chunks(vs[rows, :])
        log_a = chunks(la[rows, :])
        col = DN_HEADS * direction
        if direction == 0:
            incl, strict = ii >= jj, ii > jj
        else:
            incl, strict = ii <= jj, ii < jj
        sums = jnp.concatenate([_indicator(incl), jnp.ones((CHUNK, CHUNK), F32)], axis=0).astype(BF16)
        parts = jnp.concatenate([_dot(sums, log_a[c]) for c in range(group)], axis=0)
        cums = parts + pltpu.roll(parts, LANES - 1, 1) + pltpu.roll(parts, LANES - 2, 1)
        cums = cums.reshape(group, 2 * CHUNK, LANES)
        gl[rows, :] = cums[:, CHUNK:].reshape(span, LANES)
        g = cums[:, :CHUNK, col:col + 1]
        g_last = cums[:, CHUNK:, col:col + 1]
        beta = chunks(bet[rows, :])[:, :, 2 * DN_HEADS + col:2 * DN_HEADS + col + 1]
        g_cols = jnp.swapaxes(jnp.broadcast_to(g, (group, CHUNK, LANES)), 1, 2)[:, :CHUNK, :]
        decay = jnp.exp(jnp.minimum(g - g_cols, 0.0))
        e_g = jnp.exp(g)
        k_beta = k * beta
        both = jnp.einsum('gik,gjk->gij', jnp.concatenate([k_beta, q], axis=1).astype(BF16), k.astype(BF16),
                          preferred_element_type=F32)
        m = jnp.where(strict, both[:, :CHUNK] * decay, 0.0)
        qk = jnp.where(incl, both[:, CHUNK:] * decay, 0.0).astype(BF16)
        p = -m
        t_inv = eye + p
        pb = p.astype(BF16)
        p = _bdot(pb, pb)
        for _ in range(CHUNK.bit_length() - 3):
            pb = p.astype(BF16)
            r = _bdot(jnp.concatenate([pb, t_inv.astype(BF16)], axis=1), pb)
            p, t_inv = r[:, :CHUNK], t_inv + r[:, CHUNK:]
        t_inv = t_inv + _bdot(t_inv.astype(BF16), p.astype(BF16))
        wu = _bdot(t_inv.astype(BF16),
                   jnp.concatenate([k_beta * e_g, v * beta], axis=2).astype(BF16)).astype(BF16)
        k_tail = (k * jnp.exp(g_last - g)).astype(BF16)
        kb = jnp.einsum('gik,gin->gkn', k_tail, wu, preferred_element_type=F32)
        qb = _bdot(qk, wu)
        kn[state_rows, :] = kb[:, :, :LANES].reshape(group * LANES, LANES).astype(BF16)
        bn[state_rows, :] = kb[:, :, LANES:].reshape(group * LANES, LANES)
        qn[rows, :] = (q * e_g - qb[:, :, :LANES]).reshape(span, LANES).astype(BF16)
        on[rows, :] = qb[:, :, LANES:].reshape(span, LANES)

    fwd = (kn_f, bn_f, qn_f, on_f, gl_f)
    bwd = (kn_b, bn_b, qn_b, on_b, gl_b)

    def scan_chunk(n, state, direction, kn, bn, qn, on, gl):
        col = DN_HEADS * direction
        rows = aligned(n * CHUNK, CHUNK)
        state_rows = aligned(n * LANES, LANES)
        s16 = state.astype(BF16)
        on[rows, :] = on[rows, :] + _dot(qn[rows, :], s16)
        s_decay = jnp.exp(gl[aligned(n * CHUNK, F32_ROWS), :][0:1, col:col + 1])
        return state * s_decay - _dot(kn[state_rows, :], s16) + bn[state_rows, :]

    def scan_step(i, states):
        out = []
        for h in range(heads):
            out.append(scan_chunk(h * n_chunks + i, states[2 * h], 0, *fwd))
            out.append(scan_chunk(h * n_chunks + n_chunks - 1 - i, states[2 * h + 1], 1, *bwd))
        return tuple(out)

    states = tuple(ref[0, h] for h in range(heads) for ref in (s0f_ref, s0b_ref))
    if heads == 1 and n_groups == 2:
        wy_direction(0, 0, *fwd)
        wy_direction(group, 1, *bwd)
        wy_direction(group, 0, *fwd)
        wy_direction(0, 1, *bwd)
        for i in range(group):
            states = scan_step(i, states)
        final = lax.fori_loop(group, n_chunks, scan_step, states)
    else:
        def wy_group(gi, carry):
            wy_direction(gi * group, 0, *fwd)
            wy_direction(gi * group, 1, *bwd)
            return carry

        lax.fori_loop(0, n_groups, wy_group, 0)
        final = lax.fori_loop(0, n_chunks, scan_step, states)
    for h in range(heads):
        sf_ref[0, h] = final[2 * h]
        sb_ref[0, h] = final[2 * h + 1]
        own = pl.ds(h * seq, seq)
        o = on_f[own, :] + on_b[own, :]
        y = o * lax.rsqrt(jnp.mean(o * o, axis=-1, keepdims=True) + EPS) * dnw_ref[...]
        o_ref[0, h] = (y * _silu(gate_ref[0, h])).astype(BF16)


def _delta_net(qkv, ab, gate, conv_w, a_log, dt_bias, dn_norm, s0f, s0b, heads):
    b, _, seq, hd = qkv.shape
    lane_vec = lambda a: jnp.pad(a.reshape(1, -1), ((0, 0), (0, LANES - a.size)))
    vec = pl.BlockSpec((1, LANES), lambda i, s: (0, 0))
    part = DN_HEADS // heads
    per_head = lambda off: pl.BlockSpec((1, heads, seq, hd), lambda i, s: (i, off * part + s, 0, 0))
    cw = lambda off: pl.BlockSpec((conv_w.shape[0], heads * hd), lambda i, s: (0, off * part + s))
    st = pl.BlockSpec((1, heads, hd, hd), lambda i, s: (i, s, 0, 0))
    rows = heads * seq
    group = min(DN_GROUP, rows // CHUNK)
    shared_f32 = pltpu.VMEM((seq, hd), F32)
    rows_f32 = pltpu.VMEM((rows, hd), F32)
    rows_bf16 = pltpu.VMEM((rows, hd), BF16)
    mats_f32 = pltpu.VMEM((rows // CHUNK * hd, hd), F32)
    mats_bf16 = pltpu.VMEM((rows // CHUNK * hd, hd), BF16)
    return pl.pallas_call(
        functools.partial(_dn_kernel, group=group),
        grid=(b, part),
        in_specs=[vec, vec, per_head(0), per_head(1), per_head(2),
                  pl.BlockSpec((1, seq, LANES), lambda i, s: (i, 0, 0)),
                  per_head(0), cw(0), cw(1), cw(2),
                  pl.BlockSpec((1, hd), lambda i, s: (0, 0)), st, st],
        out_specs=[per_head(0), st, st],
        out_shape=[jax.ShapeDtypeStruct((b, DN_HEADS, seq, hd), BF16),
                   jax.ShapeDtypeStruct((b, DN_HEADS, hd, hd), F32),
                   jax.ShapeDtypeStruct((b, DN_HEADS, hd, hd), F32)],
        scratch_shapes=[rows_f32] * 3 + [shared_f32] * 2 + [rows_f32] * 3 + [rows_bf16]
                       + [mats_bf16] * 2 + [mats_f32] * 2 + [rows_bf16] * 2 + [rows_f32] * 2
                       + [pltpu.VMEM((seq + 2 * F32_ROWS, hd), F32)],
        compiler_params=_params("arbitrary", "arbitrary"),
        name="delta_net",
    )(lane_vec(a_log), lane_vec(dt_bias), qkv, qkv, qkv, ab, gate, conv_w, conv_w, conv_w, dn_norm, s0f, s0b)


def _window_sum(x, pos, limit, half, stride, bufs):
    seq = x.shape[0]
    body = pl.ds(POOL_PAD, seq)
    shifted = lambda ref, off: ref[pl.ds(POOL_PAD + off, seq), :]
    keep = (lambda ok, v: jnp.where(ok, v, 0.0)) if stride == 1 else (lambda ok, v: v)
    a, l, r = bufs
    a[body, :] = x
    left = keep(pos >= 1, shifted(a, -stride))
    right = x
    k = 1
    while k < half:
        l[body, :] = left
        r[body, :] = right
        left = left + keep(pos >= k, shifted(l, -k * stride))
        right = right + keep(pos + k < limit, shifted(r, k * stride))
        k *= 2
    return left + right


def _window_count(pos, limit, half):
    return (jnp.minimum(pos + half, limit) - jnp.maximum(pos - half, 0)).astype(F32)


def _pool_kernel(u_ref, pw_ref, ps_ref, o_ref, *bufs, rows):
    seq = u_ref.shape[1]
    t = lax.broadcasted_iota(jnp.int32, (seq, LANES), 0)
    border = jnp.zeros((POOL_PAD, LANES), F32)
    for buf in bufs:
        buf[0:POOL_PAD, :] = border
        buf[pl.ds(POOL_PAD + seq, POOL_PAD), :] = border
    for g, w in enumerate(POOL_WINDOWS):
        lanes = slice(g * LANES, (g + 1) * LANES)
        x = u_ref[0, :, lanes]
        half = w // 2
        if rows is None:
            total = _window_sum(x, t, seq, half, 1, bufs)
            count = _window_count(t, seq, half)
        else:
            c, r = t & (GRID_W - 1), t >> (GRID_W.bit_length() - 1)
            total = _window_sum(_window_sum(x, c, GRID_W, half, 1, bufs), r, rows, half, GRID_W, bufs)
            in_cols = _window_count(lax.broadcasted_iota(jnp.int32, (1, GRID_W, LANES), 1), GRID_W, half)
            in_rows = _window_count(lax.broadcasted_iota(jnp.int32, (rows, 1, LANES), 0), rows, half)
            count = (in_rows * in_cols).reshape(seq, LANES)
        m = (total / count - x).astype(BF16)
        o_ref[0, :, lanes] = (_dot(m, pw_ref[g]) * ps_ref[:, lanes]).astype(BF16)


def _pool_mixer(u, pool_w, pool_scale, rows):
    b, seq, width = u.shape
    blk = pl.BlockSpec((1, seq, width), lambda i: (i, 0, 0))
    return pl.pallas_call(
        functools.partial(_pool_kernel, rows=rows),
        grid=(b,),
        in_specs=[blk, pl.BlockSpec(pool_w.shape, lambda i: (0, 0, 0)), pl.BlockSpec((1, width), lambda i: (0, 0))],
        out_specs=blk,
        out_shape=jax.ShapeDtypeStruct((b, seq, width), BF16),
        scratch_shapes=[pltpu.VMEM((seq + 2 * POOL_PAD, LANES), F32)] * 3,
        compiler_params=_params("arbitrary"),
        name="pool_mixer",
    )(u, pool_w, pool_scale)


def _out_kernel(o_ref, p_ref, x_ref, wo_ref, g1_ref, nw_ref, sh_ref, sc_ref, wr_ref, x1_ref, h2_ref, aff_ref):
    sample = pl.program_id(1)
    lane = lax.broadcasted_iota(jnp.int32, (1, LANES), 1)
    tm = x_ref.shape[1]
    sub = min(OUT_ROWS, tm)
    parts = []
    for r in range(tm // sub):
        rows = slice(r * sub, (r + 1) * sub)
        o = jnp.concatenate([o_ref[0, h, rows, :] for h in range(o_ref.shape[1])], axis=1)
        half = o.shape[1]
        y = _dot(o, wo_ref[:half, :]) + _dot(p_ref[0, rows, :], wo_ref[half:, :])
        x1 = x_ref[0, rows, :] + g1_ref[0] * y
        x1_ref[0, rows, :] = x1
        h2 = _norm_mod(x1, nw_ref[...], sh_ref[0], sc_ref[0]).astype(BF16)
        h2_ref[0, rows, :] = h2
        logits = jnp.where(lane < N_EXPERTS, _dot(h2, wr_ref[...]), -jnp.inf)
        ex = jnp.exp(logits - jnp.max(logits, axis=-1, keepdims=True))
        parts.append(pltpu.roll(ex / jnp.sum(ex, axis=-1, keepdims=True), sample * N_EXPERTS, 1))
    aff = jnp.concatenate(parts, axis=0)

    @pl.when(sample == 0)
    def _():
        aff_ref[...] = aff

    @pl.when(sample > 0)
    def _():
        aff_ref[...] += aff


def _out_proj(o, pooled, x, w_out, g1, nw, shift, scale, w_router, tm):
    b, l, d = x.shape
    assert b * N_EXPERTS <= LANES
    half = pooled.shape[2]
    row = lambda j, i: (i, 0, 0)
    tok = lambda j, i: (i, j, 0)
    fixed = lambda j, i: (0, 0)
    vec = pl.BlockSpec((1, 1, d), row)
    return pl.pallas_call(
        _out_kernel,
        grid=(l // tm, b),
        in_specs=[pl.BlockSpec((1, o.shape[1], tm, LANES), lambda j, i: (i, 0, j, 0)),
                  pl.BlockSpec((1, tm, half), tok), pl.BlockSpec((1, tm, d), tok),
                  pl.BlockSpec(w_out.shape, fixed), vec, pl.BlockSpec((1, d), fixed), vec, vec,
                  pl.BlockSpec(w_router.shape, fixed)],
        out_specs=[pl.BlockSpec((1, tm, d), tok), pl.BlockSpec((1, tm, d), tok),
                   pl.BlockSpec((tm, LANES), lambda j, i: (j, 0))],
        out_shape=[jax.ShapeDtypeStruct((b, l, d), F32), jax.ShapeDtypeStruct((b, l, d), BF16),
                   jax.ShapeDtypeStruct((l, LANES), F32)],
        compiler_params=_params("arbitrary", "arbitrary"),
        name="out_proj",
    )(o, pooled, x, w_out, g1, nw, shift, scale, w_router)


TOKEN_BLOCK = 256
COMBINE_TILES = 4
OUT_ROWS = 256
MXU_DEPTH = 256


def _route_kernel(aff_ref, slotc_ref, slotr_ref, affr_ref, starts_ref, *, cap):
    seq = aff_ref.shape[0]
    aff = aff_ref[...]

    def bit_step(it, lo_bits):
        cand_bits = lo_bits | jnp.left_shift(jnp.int32(1), 30 - it)
        cand = lax.bitcast_convert_type(cand_bits, F32)
        count = jnp.sum(_indicator(aff >= cand), axis=0, keepdims=True)
        return jnp.where(count >= cap, cand_bits, lo_bits)

    lo_bits = lax.fori_loop(0, 31, bit_step, jnp.zeros((1, LANES), jnp.int32))
    lo = lax.bitcast_convert_type(lo_bits, F32)
    hi = lax.bitcast_convert_type(lo_bits + 1, F32)
    above = aff >= hi
    tied = (aff >= lo) & (aff < hi)
    need = cap - jnp.sum(_indicator(above), axis=0, keepdims=True)
    flags = jnp.concatenate([_indicator(above), _indicator(tied)], axis=1)

    r = lax.broadcasted_iota(jnp.int32, (TOKEN_BLOCK, TOKEN_BLOCK), 0)
    c = lax.broadcasted_iota(jnp.int32, (TOKEN_BLOCK, TOKEN_BLOCK), 1)
    tri = _indicator(r > c, BF16)
    offset = jnp.zeros((1, 2 * LANES), F32)
    pieces = []
    for j in range(seq // TOKEN_BLOCK):
        blk = flags[j * TOKEN_BLOCK:(j + 1) * TOKEN_BLOCK]
        pieces.append(_dot(tri, blk.astype(BF16)) + offset)
        offset = offset + jnp.sum(blk, axis=0, keepdims=True)
    before = jnp.concatenate(pieces, axis=0)
    tied_before = before[:, LANES:]
    rank = before[:, :LANES] + jnp.minimum(tied_before, need)
    chosen = above | (tied & (tied_before < need))
    slot = jnp.where(chosen, rank, -1.0)
    slotc_ref[...] = slot
    slotr_ref[...] = slot.T
    affr_ref[...] = aff.T
    n_blocks = seq // TOKEN_BLOCK
    for j in range(n_blocks):
        starts_ref[j:j + 1, :] = rank[j * TOKEN_BLOCK:j * TOKEN_BLOCK + 1, :]
    starts_ref[n_blocks:, :] = jnp.full((starts_ref.shape[0] - n_blocks, LANES), cap, F32)


def _route(aff, cap):
    seq = aff.shape[0]
    start_rows = -(-(seq // TOKEN_BLOCK + 1) // 8) * 8
    return pl.pallas_call(
        functools.partial(_route_kernel, cap=cap),
        out_shape=[jax.ShapeDtypeStruct((seq, LANES), F32), jax.ShapeDtypeStruct((LANES, seq), F32),
                   jax.ShapeDtypeStruct((LANES, seq), F32), jax.ShapeDtypeStruct((start_rows, LANES), F32)],
        compiler_params=pltpu.CompilerParams(vmem_limit_bytes=VMEM_LIMIT),
        name="route",
    )(aff)


def _affinity_of(hit, aff):
    return jnp.broadcast_to(jnp.sum(jnp.where(hit, aff, 0.0), axis=1, keepdims=True), (hit.shape[0], LANES))


def _gather_full(h_ref, pos_ref, aff_ref, row0, xe_s, w_s, cap):
    samples, seq, _ = h_ref.shape
    slots = lax.broadcasted_iota(jnp.int32, (cap, seq), 0).astype(F32)
    for s in range(samples):
        row = pl.ds(row0 + s * N_EXPERTS, 1)
        hit = slots == pos_ref[row, :]
        xe_s[s * cap:(s + 1) * cap, :] = _dot(_indicator(hit, BF16), h_ref[s])
        w_s[s * cap:(s + 1) * cap, :] = _affinity_of(hit, aff_ref[row, :])


def _gather_blocks(starts_ref, base, h_ref, pos_ref, aff_ref, row0, xe_s, w_s, cap, window):
    seq = h_ref.shape[1]
    n_blocks = seq // TOKEN_BLOCK
    if n_blocks == 1 or window >= cap:
        _gather_full(h_ref, pos_ref, aff_ref, row0, xe_s, w_s, cap)
        return
    first, fits = [], None
    for j in range(n_blocks):
        lo, hi = starts_ref[j * LANES + base], starts_ref[(j + 1) * LANES + base]
        start = jnp.minimum(lo // F32_ROWS * F32_ROWS, cap - window)
        first.append(start)
        ok = hi - start <= window
        fits = ok if fits is None else fits & ok

    @pl.when(fits)
    def _():
        xe_s[...] = jnp.zeros(xe_s.shape, F32)
        w_s[...] = jnp.zeros(w_s.shape, F32)
        rel = lax.broadcasted_iota(jnp.int32, (window, TOKEN_BLOCK), 0).astype(F32)
        pos, aff = pos_ref[pl.ds(row0, 1), :], aff_ref[pl.ds(row0, 1), :]
        for j in range(n_blocks):
            tokens = slice(j * TOKEN_BLOCK, (j + 1) * TOKEN_BLOCK)
            hit = rel + first[j].astype(F32) == pos[:, tokens]
            rows = pl.ds(pl.multiple_of(first[j], F32_ROWS), window)
            xe_s[rows, :] += _dot(_indicator(hit, BF16), h_ref[0, tokens, :])
            w_s[rows, :] += _affinity_of(hit, aff[:, tokens])

    @pl.when(jnp.logical_not(fits))
    def _():
        _gather_full(h_ref, pos_ref, aff_ref, row0, xe_s, w_s, cap)


def _expert_kernel(starts_ref, hx_ref, px_ref, ax_ref, *refs, cap_x, cap_z, window):
    if cap_z:
        hz_ref, pz_ref, az_ref, wg_ref, wu_ref, wd_ref, yx_ref, yz_ref, wg_s, wu_s, wd_s, xe_s, w_s = refs
    else:
        wg_ref, wu_ref, wd_ref, yx_ref, wg_s, wu_s, wd_s, xe_s, w_s = refs
    expert, step = pl.program_id(0), pl.program_id(1)
    n_latent = pl.num_programs(1) - (1 if cap_z else 0)

    @pl.when(step == 0)
    def _():
        wg_s[...] = wg_ref[0, 0].astype(BF16)
        wu_s[...] = wu_ref[0, 0].astype(BF16)
        wd_s[...] = wd_ref[0, 0].astype(BF16)

    @pl.when(step < n_latent)
    def _():
        lane = step * N_EXPERTS + expert
        _gather_blocks(starts_ref, lane, hx_ref, px_ref, ax_ref, lane % F32_ROWS, xe_s, w_s, cap_x, window)

    if cap_z:
        @pl.when(step == n_latent)
        def _():
            _gather_full(hz_ref, pz_ref, az_ref, expert, xe_s, w_s, cap_z)

    xe = xe_s[...].astype(BF16)
    hid = (_silu(_dot(xe, wg_s[...])) * _dot(xe, wu_s[...])).astype(BF16)
    ye = (_dot(hid, wd_s[...]) * w_s[:, 0:1]).astype(BF16)

    @pl.when(step < n_latent)
    def _():
        yx_ref[0, 0] = ye

    if cap_z:
        @pl.when(step == n_latent)
        def _():
            for s in range(yz_ref.shape[0]):
                yz_ref[s, 0] = ye[s * cap_z:(s + 1) * cap_z]


def _experts(w_gate, w_up, w_down, layer, x, z=None):
    b, seq_x, d = x["h2"].shape
    _, n_exp, _, ff = w_gate.shape
    cap_x = EC_CAPACITY * seq_x // n_exp
    sample = lambda i: jnp.minimum(i, b - 1)
    wspec = lambda shape: pl.BlockSpec((1, 1) + shape, lambda e, i: (layer, e, 0, 0))
    wbuf = lambda shape: pltpu.VMEM(shape, BF16)
    row_x = pl.BlockSpec((F32_ROWS, seq_x), lambda e, i: ((sample(i) * n_exp + e) // F32_ROWS, 0))
    in_specs = [pl.BlockSpec(memory_space=pltpu.SMEM),
                pl.BlockSpec((1, seq_x, d), lambda e, i: (sample(i), 0, 0)), row_x, row_x]
    args = [x["starts"], x["h2"], x["slot_rows"], x["aff_rows"]]
    out_specs = [pl.BlockSpec((1, 1, cap_x, d), lambda e, i: (sample(i), e, 0, 0))]
    out_shape = [jax.ShapeDtypeStruct((b, n_exp, cap_x, d), BF16)]
    cap_z = 0
    if z is not None:
        seq_z = z["h2"].shape[1]
        cap_z = EC_CAPACITY * seq_z // n_exp
        assert b * cap_z == cap_x
        row_z = pl.BlockSpec((LANES, seq_z), lambda e, i: (0, 0))
        in_specs += [pl.BlockSpec((b, seq_z, d), lambda e, i: (0, 0, 0)), row_z, row_z]
        args += [z["h2"], z["slot_rows"], z["aff_rows"]]
        out_specs.append(pl.BlockSpec((b, 1, cap_z, d), lambda e, i: (0, e, 0, 0)))
        out_shape.append(jax.ShapeDtypeStruct((b, n_exp, cap_z, d), BF16))
    return pl.pallas_call(
        functools.partial(_expert_kernel, cap_x=cap_x, cap_z=cap_z, window=min(64, cap_x)),
        grid=(n_exp, b + (1 if z is not None else 0)),
        in_specs=in_specs + [wspec((d, ff)), wspec((d, ff)), wspec((ff, d))],
        out_specs=out_specs,
        out_shape=out_shape,
        scratch_shapes=[wbuf((d, ff)), wbuf((d, ff)), wbuf((ff, d)),
                        pltpu.VMEM((cap_x, d), F32), pltpu.VMEM((cap_x, LANES), F32)],
        compiler_params=_params("arbitrary", "arbitrary"),
        name="experts",
    )(*args, w_gate, w_up, w_down)


def _combine_kernel(starts_ref, slot_ref, ye_ref, x_ref, g2_ref, nf_ref, o_ref, *, cap, window, final):
    sample, step = pl.program_id(0), pl.program_id(1)
    tm = TOKEN_BLOCK
    per_pass = MXU_DEPTH // window
    shift = jnp.where(sample == 0, 0, LANES - sample * N_EXPERTS)
    for t in range(x_ref.shape[1] // tm):
        tokens = slice(t * tm, (t + 1) * tm)
        tile = step * (x_ref.shape[1] // tm) + t
        base = tile * LANES + sample * N_EXPERTS
        first, fits = [], None
        for e in range(N_EXPERTS):
            lo, hi = starts_ref[base + e], starts_ref[base + LANES + e]
            start = jnp.minimum(lo // BF16_ROWS * BF16_ROWS, cap - window)
            first.append(start)
            ok = hi - start <= window
            fits = ok if fits is None else fits & ok
        pos = pltpu.roll(slot_ref[tokens, :], shift, 1)

        def finish(acc, tokens=tokens):
            x2 = x_ref[0, tokens, :] + g2_ref[0] * acc
            if final:
                x2 = x2 * lax.rsqrt(jnp.mean(x2 * x2, axis=-1, keepdims=True) + EPS) * nf_ref[...]
            o_ref[0, tokens, :] = x2

        @pl.when(fits)
        def _(pos=pos, first=first, finish=finish):
            lane = lax.broadcasted_iota(jnp.int32, (1, MXU_DEPTH), 1)
            rel = (lane % window).astype(F32)
            acc = jnp.zeros((tm, x_ref.shape[2]), F32)
            for g in range(N_EXPERTS // per_pass):
                experts = range(g * per_pass, (g + 1) * per_pass)
                val = jnp.broadcast_to(pos[:, experts[0]:experts[0] + 1] - first[experts[0]].astype(F32),
                                       (tm, MXU_DEPTH))
                for k, e in enumerate(experts[1:], start=1):
                    val = jnp.where(lane >= k * window, pos[:, e:e + 1] - first[e].astype(F32), val)
                rows = jnp.concatenate([ye_ref[0, e, pl.ds(pl.multiple_of(first[e], BF16_ROWS), window), :]
                                        for e in experts], axis=0)
                acc = acc + _dot(_indicator(val == rel, BF16), rows)
            finish(acc)

        @pl.when(jnp.logical_not(fits))
        def _(pos=pos, finish=finish):
            slots = lax.broadcasted_iota(jnp.int32, (tm, cap), 1).astype(F32)
            acc = jnp.zeros((tm, x_ref.shape[2]), F32)
            for e in range(N_EXPERTS):
                acc = acc + _dot(_indicator(pos[:, e:e + 1] == slots, BF16), ye_ref[0, e])
            finish(acc)


def _combine(starts, slot_cols, ye, x1, g2, norm_f, cap, final):
    b, l, d = x1.shape
    tm = min(COMBINE_TILES * TOKEN_BLOCK, l)
    window = min(64, cap)
    tok = lambda i, j: (i, j, 0)
    return pl.pallas_call(
        functools.partial(_combine_kernel, cap=cap, window=window, final=final),
        grid=(b, l // tm),
        in_specs=[pl.BlockSpec(memory_space=pltpu.SMEM),
                  pl.BlockSpec((tm, LANES), lambda i, j: (j, 0)),
                  pl.BlockSpec((1, N_EXPERTS, cap, d), lambda i, j: (i, 0, 0, 0)),
                  pl.BlockSpec((1, tm, d), tok), pl.BlockSpec((1, 1, d), lambda i, j: (i, 0, 0)),
                  pl.BlockSpec((1, d), lambda i, j: (0, 0))],
        out_specs=pl.BlockSpec((1, tm, d), tok),
        out_shape=jax.ShapeDtypeStruct((b, l, d), F32),
        compiler_params=_params("arbitrary", "arbitrary"),
        name="combine",
    )(starts, slot_cols, ye, x1, g2, norm_f)


def _token_mixer(x, mod, lw, states, rows, tm, heads):
    sh1, sc1 = mod[0], mod[1]
    qkv, ab, gate, pool = _in_proj(x, sh1, sc1, lw["norm1"], lw["wqkv"], lw["wab"], lw["wgate"], lw["wpool"], tm)
    o, s_f, s_b = _delta_net(qkv, ab, gate, lw["conv_w"], lw["a_log"], lw["dt_bias"], lw["dn_norm"], *states, heads)
    pooled = _pool_mixer(pool, lw["pool_w"], lw["pool_scale"], rows)
    return o, pooled, (s_f, s_b)


def _route_tokens(o, pooled, x, mod, lw, tm):
    seq = x.shape[1]
    x1, h2, aff = _out_proj(o, pooled, x, lw["w_out"], mod[2], lw["norm2"], mod[3], mod[4], lw["w_router"], tm)
    slot_cols, slot_rows, aff_rows, starts = _route(aff, EC_CAPACITY * seq // N_EXPERTS)
    return dict(x1=x1, h2=h2, slot_cols=slot_cols, slot_rows=slot_rows, aff_rows=aff_rows,
                starts=starts.astype(jnp.int32).reshape(-1), g2=mod[5])


def _scatter(r, ye, lw, final):
    cap = EC_CAPACITY * r["x1"].shape[1] // N_EXPERTS
    return _combine(r["starts"], r["slot_cols"], ye, r["x1"], r["g2"], lw["norm_f"], cap, final)


def kernel(x, c, ctx, c_ctx, w_mod, b_mod, norm1, norm2, w_in, conv_w, a_log, dt_bias, dn_norm, pool_w, pool_scale,
           w_out, w_router, w_gate, w_up, w_down, norm_f):
    batch, seq, d = x.shape
    depth = w_mod.shape[0]
    ctx_len = ctx.shape[1]
    dn_width = DN_HEADS * LANES
    qkv_cols = 3 * dn_width
    gate_cols = 2 * N_DIR * DN_HEADS
    state_cols = qkv_cols + gate_cols
    rows = seq // GRID_W
    tm_x, tm_z = 1024, ctx_len

    cond_rows = 16
    cond = jnp.zeros((cond_rows, d), F32).at[:batch].set(c).at[batch].set(c_ctx)
    mod_all = _adaln_all(cond, w_mod, b_mod)

    zero_state = jnp.zeros((batch, DN_HEADS, LANES, LANES), F32)
    z = ctx
    for l in range(depth):
        wl = w_in[l]
        lw = dict(
            norm1=norm1[l][None], norm2=norm2[l][None],
            wqkv=wl[:, :qkv_cols].astype(BF16),
            wab=jnp.pad(wl[:, qkv_cols:state_cols], ((0, 0), (0, LANES - gate_cols))).astype(BF16),
            wgate=wl[:, state_cols:state_cols + dn_width].astype(BF16),
            wpool=wl[:, state_cols + dn_width:].astype(BF16),
            conv_w=conv_w[l], a_log=a_log[l], dt_bias=dt_bias[l], dn_norm=dn_norm[l][None],
            pool_w=pool_w[l].astype(BF16), pool_scale=pool_scale[l][None],
            w_out=w_out[l].astype(BF16),
            w_router=jnp.pad(w_router[l], ((0, 0), (0, LANES - N_EXPERTS))).astype(BF16),
            norm_f=norm_f[None],
        )
        mods = mod_all[l].reshape(cond_rows, 6, d)
        mod_x = [mods[:batch, i][:, None, :] for i in range(6)]
        mod_z = [jnp.broadcast_to(mods[batch, i][None, None, :], (batch, 1, d)) for i in range(6)]

        o_z, pooled_z, ctx_states = _token_mixer(z, mod_z, lw, (zero_state, zero_state), None, tm_z, DN_HEADS)
        o_x, pooled_x, _ = _token_mixer(x, mod_x, lw, ctx_states, rows, tm_x, 1)
        rx = _route_tokens(o_x, pooled_x, x, mod_x, lw, tm_x)
        if l < depth - 1:
            rz = _route_tokens(o_z, pooled_z, z, mod_z, lw, tm_z)
            ye_x, ye_z = _experts(w_gate, w_up, w_down, l, rx, rz)
            z = _scatter(rz, ye_z, lw, False)
        else:
            ye_x, = _experts(w_gate, w_up, w_down, l, rx)
        x = _scatter(rx, ye_x, lw, l == depth - 1)
    return x
```

```python
import functools

import jax
import jax.numpy as jnp
from jax import lax
from jax.experimental import pallas as pl
from jax.experimental.pallas import tpu as pltpu

F32 = jnp.float32
BF16 = jnp.bfloat16

LANES = 128
F32_ROWS = 8
BF16_ROWS = 16
GRID_W = 64
DN_HEADS = 4
N_DIR = 2
CHUNK = 64
DN_GROUP = 16
POOL_WINDOWS = (2, 4, 8, 16)
POOL_PAD = GRID_W * max(POOL_WINDOWS) // 4
N_EXPERTS = 16
EC_CAPACITY = 2
EPS = 1e-6
VMEM_LIMIT = 56 * 1024 * 1024


def _params(*semantics):
    return pltpu.CompilerParams(dimension_semantics=semantics, vmem_limit_bytes=VMEM_LIMIT)


def _silu(x):
    return x * jax.nn.sigmoid(x)


def _dot(a, b):
    return jnp.dot(a, b, preferred_element_type=F32)


def _indicator(mask, dtype=F32):
    return jnp.where(mask, 1.0, 0.0).astype(dtype)


def _mod_kernel(cond_ref, w_ref, b_ref, o_ref):
    s = _silu(cond_ref[...])
    s_hi = s.astype(BF16)
    s_lo = (s - s_hi.astype(F32)).astype(BF16)
    w = w_ref[0]
    w_hi = w.astype(BF16)
    w_lo = (w - w_hi.astype(F32)).astype(BF16)
    rows = s.shape[0]
    both = _dot(jnp.concatenate([s_hi, s_lo], axis=0), w_hi)
    o_ref[0] = both[:rows] + both[rows:] + _dot(s_hi, w_lo) + b_ref[0]


def _adaln_all(cond, w_mod, b_mod):
    depth, d, n = w_mod.shape
    rows = cond.shape[0]
    tn = 1536
    return pl.pallas_call(
        _mod_kernel,
        grid=(depth, n // tn),
        in_specs=[pl.BlockSpec((rows, d), lambda l, j: (0, 0)),
                  pl.BlockSpec((1, d, tn), lambda l, j: (l, 0, j)),
                  pl.BlockSpec((1, 1, tn), lambda l, j: (l, 0, j))],
        out_specs=pl.BlockSpec((1, rows, tn), lambda l, j: (l, 0, j)),
        out_shape=jax.ShapeDtypeStruct((depth, rows, n), F32),
        compiler_params=_params("arbitrary", "arbitrary"),
        name="adaln",
    )(cond, w_mod, b_mod.reshape(depth, 1, n))


def _norm_mod(x, nw, shift, scale):
    ms = jnp.mean(x * x, axis=-1, keepdims=True)
    return (x * lax.rsqrt(ms + EPS) * nw) * (1.0 + scale) + shift


def _in_kernel(x_ref, sh_ref, sc_ref, nw_ref, wqkv_ref, wab_ref, wgate_ref, wpool_ref,
               qkv_ref, ab_ref, gate_ref, pool_ref):
    h = _norm_mod(x_ref[0], nw_ref[...], sh_ref[0], sc_ref[0]).astype(BF16)
    qkv = _dot(h, wqkv_ref[...])
    for c in range(qkv_ref.shape[1]):
        qkv_ref[0, c] = qkv[:, c * LANES:(c + 1) * LANES]
    gate = _dot(h, wgate_ref[...])
    for c in range(gate_ref.shape[1]):
        gate_ref[0, c] = gate[:, c * LANES:(c + 1) * LANES]
    ab_ref[0] = _dot(h, wab_ref[...])
    pool_ref[0] = _dot(h, wpool_ref[...])


def _in_proj(x, shift, scale, nw, wqkv, wab, wgate, wpool, tm):
    b, l, d = x.shape
    row = lambda i, j: (i, 0, 0)
    tok = lambda i, j: (i, j, 0)
    fixed = lambda i, j: (0, 0)
    heads = lambda w: w.shape[1] // LANES
    per_head = lambda w: pl.BlockSpec((1, heads(w), tm, LANES), lambda i, j: (i, 0, j, 0))
    return pl.pallas_call(
        _in_kernel,
        grid=(b, l // tm),
        in_specs=[pl.BlockSpec((1, tm, d), tok),
                  pl.BlockSpec((1, 1, d), row), pl.BlockSpec((1, 1, d), row),
                  pl.BlockSpec((1, d), fixed)]
                 + [pl.BlockSpec(w.shape, fixed) for w in (wqkv, wab, wgate, wpool)],
        out_specs=[per_head(wqkv), pl.BlockSpec((1, tm, wab.shape[1]), tok), per_head(wgate),
                   pl.BlockSpec((1, tm, wpool.shape[1]), tok)],
        out_shape=[jax.ShapeDtypeStruct((b, heads(wqkv), l, LANES), F32),
                   jax.ShapeDtypeStruct((b, l, wab.shape[1]), F32),
                   jax.ShapeDtypeStruct((b, heads(wgate), l, LANES), F32),
                   jax.ShapeDtypeStruct((b, l, wpool.shape[1]), F32)],
        compiler_params=_params("arbitrary", "arbitrary"),
        name="in_proj",
    )(x, shift, scale, nw, wqkv, wab, wgate, wpool)


def _bdot(a, b):
    return jnp.einsum('gij,gjk->gik', a, b, preferred_element_type=F32)


def _dn_kernel(alog_ref, dtb_ref, q_ref, k_ref, v_ref, ab_ref, gate_ref, cwq_ref, cwk_ref, cwv_ref,
               dnw_ref, s0f_ref, s0b_ref, o_ref, sf_ref, sb_ref,
               qs, ks, vs, bet_all, la_all, bet, gl_f, gl_b, la, kn_f, kn_b, bn_f, bn_b, qn_f, qn_b, on_f, on_b, pad,
               *, group):
    step = pl.program_id(1)
    _, heads, seq, _ = q_ref.shape
    n_chunks = seq // CHUNK
    lane = lax.broadcasted_iota(jnp.int32, (1, LANES), 1)

    edge = jnp.zeros((F32_ROWS, LANES), F32)
    pad[0:F32_ROWS, :] = edge
    pad[pl.ds(F32_ROWS + seq, F32_ROWS), :] = edge

    def conv_silu(x, w):
        pad[pl.ds(F32_ROWS, seq), :] = x
        y = (pad[pl.ds(F32_ROWS - 2, seq), :] * w[0:1] + pad[pl.ds(F32_ROWS - 1, seq), :] * w[1:2]
             + x * w[2:3] + pad[pl.ds(F32_ROWS + 1, seq), :] * w[3:4])
        return _silu(y)

    def l2norm(x):
        return x * lax.rsqrt(jnp.sum(x * x, axis=-1, keepdims=True) + EPS)

    @pl.when(step == 0)
    def _():
        ab = ab_ref[0]
        bet_all[...] = jax.nn.sigmoid(ab)
        la_all[...] = -jnp.exp(alog_ref[...]) * jax.nn.softplus(ab + dtb_ref[...])

    for h in range(heads):
        own = pl.ds(h * seq, seq)
        lanes = slice(h * LANES, (h + 1) * LANES)
        qs[own, :] = l2norm(conv_silu(q_ref[0, h], cwq_ref[:, lanes])) * (LANES ** -0.5)
        ks[own, :] = l2norm(conv_silu(k_ref[0, h], cwk_ref[:, lanes]))
        vs[own, :] = conv_silu(v_ref[0, h], cwv_ref[:, lanes])
        head = step * heads + h
        shift = jnp.where(head == 0, 0, LANES - head)
        bet[own, :] = pltpu.roll(bet_all[...], shift, 1)
        log_a = pltpu.roll(la_all[...], shift, 1)
        hi = log_a.astype(BF16).astype(F32)
        mid = (log_a - hi).astype(BF16).astype(F32)
        low = (log_a - hi - mid).astype(BF16).astype(F32)
        part = lane & (DN_HEADS - 1)
        la[own, :] = jnp.where(part == 0, hi, jnp.where(part == 1, pltpu.roll(mid, 1, 1),
                                                        pltpu.roll(low, 2, 1))).astype(BF16)

    ii = lax.broadcasted_iota(jnp.int32, (CHUNK, CHUNK), 0)
    jj = lax.broadcasted_iota(jnp.int32, (CHUNK, CHUNK), 1)
    eye = _indicator(ii == jj)
    span = group * CHUNK
    n_groups = heads * n_chunks // group

    def aligned(start, size):
        return pl.ds(start if isinstance(start, int) else pl.multiple_of(start, size), size)

    def wy_direction(first, direction, kn, bn, qn, on, gl):
        rows = aligned(first * CHUNK, span)
        state_rows = aligned(first * LANES, group * LANES)
        chunks = lambda a: a.reshape(group, CHUNK, a.shape[-1])
        q, k, v = chunks(qs[rows, :]), chunks(ks[rows, :]), chunks(vs[rows, :])
        log_a = chunks(la[rows, :])
        col = DN_HEADS * direction
        if direction == 0:
            incl, strict = ii >= jj, ii > jj
        else:
            incl, strict = ii <= jj, ii < jj
        sums = jnp.concatenate([_indicator(incl), jnp.ones((CHUNK, CHUNK), F32)], axis=0).astype(BF16)
        parts = jnp.concatenate([_dot(sums, log_a[c]) for c in range(group)], axis=0)
        cums = parts + pltpu.roll(parts, LANES - 1, 1) + pltpu.roll(parts, LANES - 2, 1)
        cums = cums.reshape(group, 2 * CHUNK, LANES)
        gl[rows, :] = cums[:, CHUNK:].reshape(span, LANES)
        g = cums[:, :CHUNK, col:col + 1]
        g_last = cums[:, CHUNK:, col:col + 1]
        beta = chunks(bet[rows, :])[:, :, 2 * DN_HEADS + col:2 * DN_HEADS + col + 1]
        g_cols = jnp.swapaxes(jnp.broadcast_to(g, (group, CHUNK, LANES)), 1, 2)[:, :CHUNK, :]
        decay = jnp.exp(jnp.minimum(g - g_cols, 0.0))
        e_g = jnp.exp(g)
        k_beta = k * beta
        both = jnp.einsum('gik,gjk->gij', jnp.concatenate([k_beta, q], axis=1).astype(BF16), k.astype(BF16),
                          preferred_element_type=F32)
        m = jnp.where(strict, both[:, :CHUNK] * decay, 0.0)
        qk = jnp.where(incl, both[:, CHUNK:] * decay, 0.0).astype(BF16)
        p = -m
        t_inv = eye + p
        pb = p.astype(BF16)
        p = _bdot(pb, pb)
        for _ in range(CHUNK.bit_length() - 3):
            pb = p.astype(BF16)
            r = _bdot(jnp.concatenate([pb, t_inv.astype(BF16)], axis=1), pb)
            p, t_inv = r[:, :CHUNK], t_inv + r[:, CHUNK:]
        t_inv = t_inv + _bdot(t_inv.astype(BF16), p.astype(BF16))
        wu = _bdot(t_inv.astype(BF16),
                   jnp.concatenate([k_beta * e_g, v * beta], axis=2).astype(BF16)).astype(BF16)
        k_tail = (k * jnp.exp(g_last - g)).astype(BF16)
        kb = jnp.einsum('gik,gin->gkn', k_tail, wu, preferred_element_type=F32)
        qb = _bdot(qk, wu)
        kn[state_rows, :] = kb[:, :, :LANES].reshape(group * LANES, LANES).astype(BF16)
        bn[state_rows, :] = kb[:, :, LANES:].reshape(group * LANES, LANES)
        qn[rows, :] = (q * e_g - qb[:, :, :LANES]).reshape(span, LANES).astype(BF16)
        on[rows, :] = qb[:, :, LANES:].reshape(span, LANES)

    fwd = (kn_f, bn_f, qn_f, on_f, gl_f)
    bwd = (kn_b, bn_b, qn_b, on_b, gl_b)

    def scan_chunk(n, state, direction, kn, bn, qn, on, gl):
        col = DN_HEADS * direction
        rows = aligned(n * CHUNK, CHUNK)
        state_rows = aligned(n * LANES, LANES)
        s16 = state.astype(BF16)
        on[rows, :] = on[rows, :] + _dot(qn[rows, :], s16)
        s_decay = jnp.exp(gl[aligned(n * CHUNK, F32_ROWS), :][0:1, col:col + 1])
        return state * s_decay - _dot(kn[state_rows, :], s16) + bn[state_rows, :]

    def scan_step(i, states):
        out = []
        for h in range(heads):
            out.append(scan_chunk(h * n_chunks + i, states[2 * h], 0, *fwd))
            out.append(scan_chunk(h * n_chunks + n_chunks - 1 - i, states[2 * h + 1], 1, *bwd))
        return tuple(out)

    states = tuple(ref[0, h] for h in range(heads) for ref in (s0f_ref, s0b_ref))
    if heads == 1 and n_groups == 2:
        wy_direction(0, 0, *fwd)
        wy_direction(group, 1, *bwd)
        wy_direction(group, 0, *fwd)
        wy_direction(0, 1, *bwd)
        for i in range(group):
            states = scan_step(i, states)
        final = lax.fori_loop(group, n_chunks, scan_step, states)
    else:
        def wy_group(gi, carry):
            wy_direction(gi * group, 0, *fwd)
            wy_direction(gi * group, 1, *bwd)
            return carry

        lax.fori_loop(0, n_groups, wy_group, 0)
        final = lax.fori_loop(0, n_chunks, scan_step, states)
    for h in range(heads):
        sf_ref[0, h] = final[2 * h]
        sb_ref[0, h] = final[2 * h + 1]
        own = pl.ds(h * seq, seq)
        o = on_f[own, :] + on_b[own, :]
        y = o * lax.rsqrt(jnp.mean(o * o, axis=-1, keepdims=True) + EPS) * dnw_ref[...]
        o_ref[0, h] = (y * _silu(gate_ref[0, h])).astype(BF16)


def _delta_net(qkv, ab, gate, conv_w, a_log, dt_bias, dn_norm, s0f, s0b, heads):
    b, _, seq, hd = qkv.shape
    lane_vec = lambda a: jnp.pad(a.reshape(1, -1), ((0, 0), (0, LANES - a.size)))
    vec = pl.BlockSpec((1, LANES), lambda i, s: (0, 0))
    part = DN_HEADS // heads
    per_head = lambda off: pl.BlockSpec((1, heads, seq, hd), lambda i, s: (i, off * part + s, 0, 0))
    cw = lambda off: pl.BlockSpec((conv_w.shape[0], heads * hd), lambda i, s: (0, off * part + s))
    st = pl.BlockSpec((1, heads, hd, hd), lambda i, s: (i, s, 0, 0))
    rows = heads * seq
    group = min(DN_GROUP, rows // CHUNK)
    shared_f32 = pltpu.VMEM((seq, hd), F32)
    rows_f32 = pltpu.VMEM((rows, hd), F32)
    rows_bf16 = pltpu.VMEM((rows, hd), BF16)
    mats_f32 = pltpu.VMEM((rows // CHUNK * hd, hd), F32)
    mats_bf16 = pltpu.VMEM((rows // CHUNK * hd, hd), BF16)
    return pl.pallas_call(
        functools.partial(_dn_kernel, group=group),
        grid=(b, part),
        in_specs=[vec, vec, per_head(0), per_head(1), per_head(2),
                  pl.BlockSpec((1, seq, LANES), lambda i, s: (i, 0, 0)),
                  per_head(0), cw(0), cw(1), cw(2),
                  pl.BlockSpec((1, hd), lambda i, s: (0, 0)), st, st],
        out_specs=[per_head(0), st, st],
        out_shape=[jax.ShapeDtypeStruct((b, DN_HEADS, seq, hd), BF16),
                   jax.ShapeDtypeStruct((b, DN_HEADS, hd, hd), F32),
                   jax.ShapeDtypeStruct((b, DN_HEADS, hd, hd), F32)],
        scratch_shapes=[rows_f32] * 3 + [shared_f32] * 2 + [rows_f32] * 3 + [rows_bf16]
                       + [mats_bf16] * 2 + [mats_f32] * 2 + [rows_bf16] * 2 + [rows_f32] * 2
                       + [pltpu.VMEM((seq + 2 * F32_ROWS, hd), F32)],
        compiler_params=_params("arbitrary", "arbitrary"),
        name="delta_net",
    )(lane_vec(a_log), lane_vec(dt_bias), qkv, qkv, qkv, ab, gate, conv_w, conv_w, conv_w, dn_norm, s0f, s0b)


def _window_sum(x, pos, limit, half, stride, bufs):
    seq = x.shape[0]
    body = pl.ds(POOL_PAD, seq)
    shifted = lambda ref, off: ref[pl.ds(POOL_PAD + off, seq), :]
    keep = (lambda ok, v: jnp.where(ok, v, 0.0)) if stride == 1 else (lambda ok, v: v)
    a, l, r = bufs
    a[body, :] = x
    left = keep(pos >= 1, shifted(a, -stride))
    right = x
    k = 1
    while k < half:
        l[body, :] = left
        r[body, :] = right
        left = left + keep(pos >= k, shifted(l, -k * stride))
        right = right + keep(pos + k < limit, shifted(r, k * stride))
        k *= 2
    return left + right


def _window_count(pos, limit, half):
    return (jnp.minimum(pos + half, limit) - jnp.maximum(pos - half, 0)).astype(F32)


def _pool_kernel(u_ref, pw_ref, ps_ref, o_ref, *bufs, rows):
    seq = u_ref.shape[1]
    t = lax.broadcasted_iota(jnp.int32, (seq, LANES), 0)
    border = jnp.zeros((POOL_PAD, LANES), F32)
    for buf in bufs:
        buf[0:POOL_PAD, :] = border
        buf[pl.ds(POOL_PAD + seq, POOL_PAD), :] = border
    for g, w in enumerate(POOL_WINDOWS):
        lanes = slice(g * LANES, (g + 1) * LANES)
        x = u_ref[0, :, lanes]
        half = w // 2
        if rows is None:
            total = _window_sum(x, t, seq, half, 1, bufs)
            count = _window_count(t, seq, half)
        else:
            c, r = t & (GRID_W - 1), t >> (GRID_W.bit_length() - 1)
            total = _window_sum(_window_sum(x, c, GRID_W, half, 1, bufs), r, rows, half, GRID_W, bufs)
            in_cols = _window_count(lax.broadcasted_iota(jnp.int32, (1, GRID_W, LANES), 1), GRID_W, half)
            in_rows = _window_count(lax.broadcasted_iota(jnp.int32, (rows, 1, LANES), 0), rows, half)
            count = (in_rows * in_cols).reshape(seq, LANES)
        m = (total / count - x).astype(BF16)
        o_ref[0, :, lanes] = (_dot(m, pw_ref[g]) * ps_ref[:, lanes]).astype(BF16)


def _pool_mixer(u, pool_w, pool_scale, rows):
    b, seq, width = u.shape
    blk = pl.BlockSpec((1, seq, width), lambda i: (i, 0, 0))
    return pl.pallas_call(
        functools.partial(_pool_kernel, rows=rows),
        grid=(b,),
        in_specs=[blk, pl.BlockSpec(pool_w.shape, lambda i: (0, 0, 0)), pl.BlockSpec((1, width), lambda i: (0, 0))],
        out_specs=blk,
        out_shape=jax.ShapeDtypeStruct((b, seq, width), BF16),
        scratch_shapes=[pltpu.VMEM((seq + 2 * POOL_PAD, LANES), F32)] * 3,
        compiler_params=_params("arbitrary"),
        name="pool_mixer",
    )(u, pool_w, pool_scale)


def _out_kernel(o_ref, p_ref, x_ref, wo_ref, g1_ref, nw_ref, sh_ref, sc_ref, wr_ref, x1_ref, h2_ref, aff_ref):
    sample = pl.program_id(1)
    lane = lax.broadcasted_iota(jnp.int32, (1, LANES), 1)
    tm = x_ref.shape[1]
    sub = min(OUT_ROWS, tm)
    parts = []
    for r in range(tm // sub):
        rows = slice(r * sub, (r + 1) * sub)
        o = jnp.concatenate([o_ref[0, h, rows, :] for h in range(o_ref.shape[1])], axis=1)
        half = o.shape[1]
        y = _dot(o, wo_ref[:half, :]) + _dot(p_ref[0, rows, :], wo_ref[half:, :])
        x1 = x_ref[0, rows, :] + g1_ref[0] * y
        x1_ref[0, rows, :] = x1
        h2 = _norm_mod(x1, nw_ref[...], sh_ref[0], sc_ref[0]).astype(BF16)
        h2_ref[0, rows, :] = h2
        logits = jnp.where(lane < N_EXPERTS, _dot(h2, wr_ref[...]), -jnp.inf)
        ex = jnp.exp(logits - jnp.max(logits, axis=-1, keepdims=True))
        parts.append(pltpu.roll(ex / jnp.sum(ex, axis=-1, keepdims=True), sample * N_EXPERTS, 1))
    aff = jnp.concatenate(parts, axis=0)

    @pl.when(sample == 0)
    def _():
        aff_ref[...] = aff

    @pl.when(sample > 0)
    def _():
        aff_ref[...] += aff


def _out_proj(o, pooled, x, w_out, g1, nw, shift, scale, w_router, tm):
    b, l, d = x.shape
    assert b * N_EXPERTS <= LANES
    half = pooled.shape[2]
    row = lambda j, i: (i, 0, 0)
    tok = lambda j, i: (i, j, 0)
    fixed = lambda j, i: (0, 0)
    vec = pl.BlockSpec((1, 1, d), row)
    return pl.pallas_call(
        _out_kernel,
        grid=(l // tm, b),
        in_specs=[pl.BlockSpec((1, o.shape[1], tm, LANES), lambda j, i: (i, 0, j, 0)),
                  pl.BlockSpec((1, tm, half), tok), pl.BlockSpec((1, tm, d), tok),
                  pl.BlockSpec(w_out.shape, fixed), vec, pl.BlockSpec((1, d), fixed), vec, vec,
                  pl.BlockSpec(w_router.shape, fixed)],
        out_specs=[pl.BlockSpec((1, tm, d), tok), pl.BlockSpec((1, tm, d), tok),
                   pl.BlockSpec((tm, LANES), lambda j, i: (j, 0))],
        out_shape=[jax.ShapeDtypeStruct((b, l, d), F32), jax.ShapeDtypeStruct((b, l, d), BF16),
                   jax.ShapeDtypeStruct((l, LANES), F32)],
        compiler_params=_params("arbitrary", "arbitrary"),
        name="out_proj",
    )(o, pooled, x, w_out, g1, nw, shift, scale, w_router)


TOKEN_BLOCK = 256
COMBINE_TILES = 4
OUT_ROWS = 256
MXU_DEPTH = 256


def _route_kernel(aff_ref, slotc_ref, slotr_ref, affr_ref, starts_ref, *, cap):
    seq = aff_ref.shape[0]
    aff = aff_ref[...]

    def bit_step(it, lo_bits):
        cand_bits = lo_bits | jnp.left_shift(jnp.int32(1), 30 - it)
        cand = lax.bitcast_convert_type(cand_bits, F32)
        count = jnp.sum(_indicator(aff >= cand), axis=0, keepdims=True)
        return jnp.where(count >= cap, cand_bits, lo_bits)

    lo_bits = lax.fori_loop(0, 31, bit_step, jnp.zeros((1, LANES), jnp.int32))
    lo = lax.bitcast_convert_type(lo_bits, F32)
    hi = lax.bitcast_convert_type(lo_bits + 1, F32)
    above = aff >= hi
    tied = (aff >= lo) & (aff < hi)
    need = cap - jnp.sum(_indicator(above), axis=0, keepdims=True)
    flags = jnp.concatenate([_indicator(above), _indicator(tied)], axis=1)

    r = lax.broadcasted_iota(jnp.int32, (TOKEN_BLOCK, TOKEN_BLOCK), 0)
    c = lax.broadcasted_iota(jnp.int32, (TOKEN_BLOCK, TOKEN_BLOCK), 1)
    tri = _indicator(r > c, BF16)
    offset = jnp.zeros((1, 2 * LANES), F32)
    pieces = []
    for j in range(seq // TOKEN_BLOCK):
        blk = flags[j * TOKEN_BLOCK:(j + 1) * TOKEN_BLOCK]
        pieces.append(_dot(tri, blk.astype(BF16)) + offset)
        offset = offset + jnp.sum(blk, axis=0, keepdims=True)
    before = jnp.concatenate(pieces, axis=0)
    tied_before = before[:, LANES:]
    rank = before[:, :LANES] + jnp.minimum(tied_before, need)
    chosen = above | (tied & (tied_before < need))
    slot = jnp.where(chosen, rank, -1.0)
    slotc_ref[...] = slot
    slotr_ref[...] = slot.T
    affr_ref[...] = aff.T
    n_blocks = seq // TOKEN_BLOCK
    for j in range(n_blocks):
        starts_ref[j:j + 1, :] = rank[j * TOKEN_BLOCK:j * TOKEN_BLOCK + 1, :]
    starts_ref[n_blocks:, :] = jnp.full((starts_ref.shape[0] - n_blocks, LANES), cap, F32)


def _route(aff, cap):
    seq = aff.shape[0]
    start_rows = -(-(seq // TOKEN_BLOCK + 1) // 8) * 8
    return pl.pallas_call(
        functools.partial(_route_kernel, cap=cap),
        out_shape=[jax.ShapeDtypeStruct((seq, LANES), F32), jax.ShapeDtypeStruct((LANES, seq), F32),
                   jax.ShapeDtypeStruct((LANES, seq), F32), jax.ShapeDtypeStruct((start_rows, LANES), F32)],
        compiler_params=pltpu.CompilerParams(vmem_limit_bytes=VMEM_LIMIT),
        name="route",
    )(aff)


def _affinity_of(hit, aff):
    return jnp.broadcast_to(jnp.sum(jnp.where(hit, aff, 0.0), axis=1, keepdims=True), (hit.shape[0], LANES))


def _gather_full(h_ref, pos_ref, aff_ref, row0, xe_s, w_s, cap):
    samples, seq, _ = h_ref.shape
    slots = lax.broadcasted_iota(jnp.int32, (cap, seq), 0).astype(F32)
    for s in range(samples):
        row = pl.ds(row0 + s * N_EXPERTS, 1)
        hit = slots == pos_ref[row, :]
        xe_s[s * cap:(s + 1) * cap, :] = _dot(_indicator(hit, BF16), h_ref[s])
        w_s[s * cap:(s + 1) * cap, :] = _affinity_of(hit, aff_ref[row, :])


def _gather_blocks(starts_ref, base, h_ref, pos_ref, aff_ref, row0, xe_s, w_s, cap, window):
    seq = h_ref.shape[1]
    n_blocks = seq // TOKEN_BLOCK
    if n_blocks == 1 or window >= cap:
        _gather_full(h_ref, pos_ref, aff_ref, row0, xe_s, w_s, cap)
        return
    first, fits = [], None
    for j in range(n_blocks):
        lo, hi = starts_ref[j * LANES + base], starts_ref[(j + 1) * LANES + base]
        start = jnp.minimum(lo // F32_ROWS * F32_ROWS, cap - window)
        first.append(start)
        ok = hi - start <= window
        fits = ok if fits is None else fits & ok

    @pl.when(fits)
    def _():
        xe_s[...] = jnp.zeros(xe_s.shape, F32)
        w_s[...] = jnp.zeros(w_s.shape, F32)
        rel = lax.broadcasted_iota(jnp.int32, (window, TOKEN_BLOCK), 0).astype(F32)
        pos, aff = pos_ref[pl.ds(row0, 1), :], aff_ref[pl.ds(row0, 1), :]
        for j in range(n_blocks):
            tokens = slice(j * TOKEN_BLOCK, (j + 1) * TOKEN_BLOCK)
            hit = rel + first[j].astype(F32) == pos[:, tokens]
            rows = pl.ds(pl.multiple_of(first[j], F32_ROWS), window)
            xe_s[rows, :] += _dot(_indicator(hit, BF16), h_ref[0, tokens, :])
            w_s[rows, :] += _affinity_of(hit, aff[:, tokens])

    @pl.when(jnp.logical_not(fits))
    def _():
        _gather_full(h_ref, pos_ref, aff_ref, row0, xe_s, w_s, cap)


def _expert_kernel(starts_ref, hx_ref, px_ref, ax_ref, *refs, cap_x, cap_z, window):
    if cap_z:
        hz_ref, pz_ref, az_ref, wg_ref, wu_ref, wd_ref, yx_ref, yz_ref, wg_s, wu_s, wd_s, xe_s, w_s = refs
    else:
        wg_ref, wu_ref, wd_ref, yx_ref, wg_s, wu_s, wd_s, xe_s, w_s = refs
    expert, step = pl.program_id(0), pl.program_id(1)
    first_latent = 1 if cap_z else 0

    @pl.when(step == 0)
    def _():
        wg_s[...] = wg_ref[0, 0].astype(BF16)
        wu_s[...] = wu_ref[0, 0].astype(BF16)
        wd_s[...] = wd_ref[0, 0].astype(BF16)

    @pl.when(step >= first_latent)
    def _():
        lane = (step - first_latent) * N_EXPERTS + expert
        _gather_blocks(starts_ref, lane, hx_ref, px_ref, ax_ref, lane % F32_ROWS, xe_s, w_s, cap_x, window)

    if cap_z:
        @pl.when(step == 0)
        def _():
            _gather_full(hz_ref, pz_ref, az_ref, expert, xe_s, w_s, cap_z)

    xe = xe_s[...].astype(BF16)
    hid = (_silu(_dot(xe, wg_s[...])) * _dot(xe, wu_s[...])).astype(BF16)
    ye = (_dot(hid, wd_s[...]) * w_s[:, 0:1]).astype(BF16)

    @pl.when(step >= first_latent)
    def _():
        yx_ref[0, 0] = ye

    if cap_z:
        @pl.when(step == 0)
        def _():
            for s in range(yz_ref.shape[0]):
                yz_ref[s, 0] = ye[s * cap_z:(s + 1) * cap_z]


def _experts(w_gate, w_up, w_down, layer, x, z=None):
    b, seq_x, d = x["h2"].shape
    _, n_exp, _, ff = w_gate.shape
    cap_x = EC_CAPACITY * seq_x // n_exp
    if z is None:
        held = ahead = lambda i: i
    else:
        held = lambda i: jnp.where(i == 0, b - 1, i - 1)
        ahead = lambda i: jnp.maximum(i - 1, 0)
    wspec = lambda shape: pl.BlockSpec((1, 1) + shape, lambda e, i: (layer, e, 0, 0))
    wbuf = lambda shape: pltpu.VMEM(shape, BF16)
    row_x = pl.BlockSpec((F32_ROWS, seq_x), lambda e, i: ((ahead(i) * n_exp + e) // F32_ROWS, 0))
    in_specs = [pl.BlockSpec(memory_space=pltpu.SMEM),
                pl.BlockSpec((1, seq_x, d), lambda e, i: (held(i), 0, 0)), row_x, row_x]
    args = [x["starts"], x["h2"], x["slot_rows"], x["aff_rows"]]
    out_specs = [pl.BlockSpec((1, 1, cap_x, d), lambda e, i: (ahead(i), e, 0, 0))]
    out_shape = [jax.ShapeDtypeStruct((b, n_exp, cap_x, d), BF16)]
    cap_z = 0
    if z is not None:
        seq_z = z["h2"].shape[1]
        cap_z = EC_CAPACITY * seq_z // n_exp
        assert b * cap_z == cap_x
        row_z = pl.BlockSpec((LANES, seq_z), lambda e, i: (0, 0))
        in_specs += [pl.BlockSpec((b, seq_z, d), lambda e, i: (0, 0, 0)), row_z, row_z]
        args += [z["h2"], z["slot_rows"], z["aff_rows"]]
        out_specs.append(pl.BlockSpec((b, 1, cap_z, d), lambda e, i: (0, e, 0, 0)))
        out_shape.append(jax.ShapeDtypeStruct((b, n_exp, cap_z, d), BF16))
    return pl.pallas_call(
        functools.partial(_expert_kernel, cap_x=cap_x, cap_z=cap_z, window=min(64, cap_x)),
        grid=(n_exp, b + (1 if z is not None else 0)),
        in_specs=in_specs + [wspec((d, ff)), wspec((d, ff)), wspec((ff, d))],
        out_specs=out_specs,
        out_shape=out_shape,
        scratch_shapes=[wbuf((d, ff)), wbuf((d, ff)), wbuf((ff, d)),
                        pltpu.VMEM((cap_x, d), F32), pltpu.VMEM((cap_x, LANES), F32)],
        compiler_params=_params("arbitrary", "arbitrary"),
        name="experts",
    )(*args, w_gate, w_up, w_down)


def _combine_kernel(starts_ref, slot_ref, ye_ref, x_ref, g2_ref, nf_ref, o_ref, *, cap, window, final):
    sample, step = pl.program_id(0), pl.program_id(1)
    tm = TOKEN_BLOCK
    per_pass = MXU_DEPTH // window
    shift = jnp.where(sample == 0, 0, LANES - sample * N_EXPERTS)
    for t in range(x_ref.shape[1] // tm):
        tokens = slice(t * tm, (t + 1) * tm)
        tile = step * (x_ref.shape[1] // tm) + t
        base = tile * LANES + sample * N_EXPERTS
        first, fits = [], None
        for e in range(N_EXPERTS):
            lo, hi = starts_ref[base + e], starts_ref[base + LANES + e]
            start = jnp.minimum(lo // BF16_ROWS * BF16_ROWS, cap - window)
            first.append(start)
            ok = hi - start <= window
            fits = ok if fits is None else fits & ok
        pos = pltpu.roll(slot_ref[tokens, :], shift, 1)

        def finish(acc, tokens=tokens):
            x2 = x_ref[0, tokens, :] + g2_ref[0] * acc
            if final:
                x2 = x2 * lax.rsqrt(jnp.mean(x2 * x2, axis=-1, keepdims=True) + EPS) * nf_ref[...]
            o_ref[0, tokens, :] = x2

        @pl.when(fits)
        def _(pos=pos, first=first, finish=finish):
            lane = lax.broadcasted_iota(jnp.int32, (1, MXU_DEPTH), 1)
            rel = (lane % window).astype(F32)
            acc = jnp.zeros((tm, x_ref.shape[2]), F32)
            for g in range(N_EXPERTS // per_pass):
                experts = range(g * per_pass, (g + 1) * per_pass)
                val = jnp.broadcast_to(pos[:, experts[0]:experts[0] + 1] - first[experts[0]].astype(F32),
                                       (tm, MXU_DEPTH))
                for k, e in enumerate(experts[1:], start=1):
                    val = jnp.where(lane >= k * window, pos[:, e:e + 1] - first[e].astype(F32), val)
                rows = jnp.concatenate([ye_ref[0, e, pl.ds(pl.multiple_of(first[e], BF16_ROWS), window), :]
                                        for e in experts], axis=0)
                acc = acc + _dot(_indicator(val == rel, BF16), rows)
            finish(acc)

        @pl.when(jnp.logical_not(fits))
        def _(pos=pos, finish=finish):
            slots = lax.broadcasted_iota(jnp.int32, (tm, cap), 1).astype(F32)
            acc = jnp.zeros((tm, x_ref.shape[2]), F32)
            for e in range(N_EXPERTS):
                acc = acc + _dot(_indicator(pos[:, e:e + 1] == slots, BF16), ye_ref[0, e])
            finish(acc)


def _combine(starts, slot_cols, ye, x1, g2, norm_f, cap, final):
    b, l, d = x1.shape
    tm = min(COMBINE_TILES * TOKEN_BLOCK, l)
    window = min(64, cap)
    tok = lambda i, j: (i, j, 0)
    return pl.pallas_call(
        functools.partial(_combine_kernel, cap=cap, window=window, final=final),
        grid=(b, l // tm),
        in_specs=[pl.BlockSpec(memory_space=pltpu.SMEM),
                  pl.BlockSpec((tm, LANES), lambda i, j: (j, 0)),
                  pl.BlockSpec((1, N_EXPERTS, cap, d), lambda i, j: (i, 0, 0, 0)),
                  pl.BlockSpec((1, tm, d), tok), pl.BlockSpec((1, 1, d), lambda i, j: (i, 0, 0)),
                  pl.BlockSpec((1, d), lambda i, j: (0, 0))],
        out_specs=pl.BlockSpec((1, tm, d), tok),
        out_shape=jax.ShapeDtypeStruct((b, l, d), F32),
        compiler_params=_params("arbitrary", "arbitrary"),
        name="combine",
    )(starts, slot_cols, ye, x1, g2, norm_f)


def _token_mixer(x, mod, lw, states, rows, tm, heads):
    sh1, sc1 = mod[0], mod[1]
    qkv, ab, gate, pool = _in_proj(x, sh1, sc1, lw["norm1"], lw["wqkv"], lw["wab"], lw["wgate"], lw["wpool"], tm)
    o, s_f, s_b = _delta_net(qkv, ab, gate, lw["conv_w"], lw["a_log"], lw["dt_bias"], lw["dn_norm"], *states, heads)
    pooled = _pool_mixer(pool, lw["pool_w"], lw["pool_scale"], rows)
    return o, pooled, (s_f, s_b)


def _route_tokens(o, pooled, x, mod, lw, tm):
    seq = x.shape[1]
    x1, h2, aff = _out_proj(o, pooled, x, lw["w_out"], mod[2], lw["norm2"], mod[3], mod[4], lw["w_router"], tm)
    slot_cols, slot_rows, aff_rows, starts = _route(aff, EC_CAPACITY * seq // N_EXPERTS)
    return dict(x1=x1, h2=h2, slot_cols=slot_cols, slot_rows=slot_rows, aff_rows=aff_rows,
                starts=starts.astype(jnp.int32).reshape(-1), g2=mod[5])


def _scatter(r, ye, lw, final):
    cap = EC_CAPACITY * r["x1"].shape[1] // N_EXPERTS
    return _combine(r["starts"], r["slot_cols"], ye, r["x1"], r["g2"], lw["norm_f"], cap, final)


def kernel(x, c, ctx, c_ctx, w_mod, b_mod, norm1, norm2, w_in, conv_w, a_log, dt_bias, dn_norm, pool_w, pool_scale,
           w_out, w_router, w_gate, w_up, w_down, norm_f):
    batch, seq, d = x.shape
    depth = w_mod.shape[0]
    ctx_len = ctx.shape[1]
    dn_width = DN_HEADS * LANES
    qkv_cols = 3 * dn_width
    gate_cols = 2 * N_DIR * DN_HEADS
    state_cols = qkv_cols + gate_cols
    rows = seq // GRID_W
    tm_x, tm_z = 1024, ctx_len

    cond_rows = 16
    cond = jnp.zeros((cond_rows, d), F32).at[:batch].set(c).at[batch].set(c_ctx)
    mod_all = _adaln_all(cond, w_mod, b_mod)

    zero_state = jnp.zeros((batch, DN_HEADS, LANES, LANES), F32)
    z = ctx
    for l in range(depth):
        wl = w_in[l]
        lw = dict(
            norm1=norm1[l][None], norm2=norm2[l][None],
            wqkv=wl[:, :qkv_cols].astype(BF16),
            wab=jnp.pad(wl[:, qkv_cols:state_cols], ((0, 0), (0, LANES - gate_cols))).astype(BF16),
            wgate=wl[:, state_cols:state_cols + dn_width].astype(BF16),
            wpool=wl[:, state_cols + dn_width:].astype(BF16),
            conv_w=conv_w[l], a_log=a_log[l], dt_bias=dt_bias[l], dn_norm=dn_norm[l][None],
            pool_w=pool_w[l].astype(BF16), pool_scale=pool_scale[l][None],
            w_out=w_out[l].astype(BF16),
            w_router=jnp.pad(w_router[l], ((0, 0), (0, LANES - N_EXPERTS))).astype(BF16),
            norm_f=norm_f[None],
        )
        mods = mod_all[l].reshape(cond_rows, 6, d)
        mod_x = [mods[:batch, i][:, None, :] for i in range(6)]
        mod_z = [jnp.broadcast_to(mods[batch, i][None, None, :], (batch, 1, d)) for i in range(6)]

        o_z, pooled_z, ctx_states = _token_mixer(z, mod_z, lw, (zero_state, zero_state), None, tm_z, DN_HEADS)
        o_x, pooled_x, _ = _token_mixer(x, mod_x, lw, ctx_states, rows, tm_x, 1)
        rx = _route_tokens(o_x, pooled_x, x, mod_x, lw, tm_x)
        if l < depth - 1:
            rz = _route_tokens(o_z, pooled_z, z, mod_z, lw, tm_z)
            ye_x, ye_z = _experts(w_gate, w_up, w_down, l, rx, rz)
            z = _scatter(rz, ye_z, lw, False)
        else:
            ye_x, = _experts(w_gate, w_up, w_down, l, rx)
        x = _scatter(rx, ye_x, lw, l == depth - 1)
    return x
```
